```python
import math
import jax, jax.numpy as jnp
from jax import lax
import numpy as np

D_MODEL = 1024
BATCH = 4
SEQ = 8192
DEPTH = 1

HEAD_DIM = 64
MIX_WIDTH = D_MODEL
NSA_WIDTH = MIX_WIDTH // 2
RWKV_WIDTH = MIX_WIDTH - NSA_WIDTH
NSA_HEADS = NSA_WIDTH // HEAD_DIM
NSA_KV_GROUPS = 2
NSA_HPG = NSA_HEADS // NSA_KV_GROUPS
NSA_KV_WIDTH = NSA_KV_GROUPS * HEAD_DIM
CMP_BLOCK = 32
CMP_STRIDE = 16
CMP_HIDDEN = 2 * HEAD_DIM
SLC_BLOCK = 64
SLC_TOPK = 16
WINDOW = 512
Q_BLOCK = 128
NSA_COLS = NSA_WIDTH + 6 * NSA_KV_WIDTH + 3 * NSA_HEADS
RWKV_HEADS = RWKV_WIDTH // HEAD_DIM
RWKV_W_LORA = 32
RWKV_A_LORA = 32
RWKV_G_LORA = 96
RWKV_COLS = 3 * RWKV_WIDTH + RWKV_W_LORA + RWKV_A_LORA + RWKV_G_LORA
IN_COLS = NSA_COLS + RWKV_COLS
DECAY_SCALE = math.exp(-0.5)
GN_EPS = 64e-5
N_EXPERTS = 32
TOP_K = 4
D_EXPERT = D_MODEL
SWIGLU_LIMIT = 7.0
SWIGLU_ALPHA = 1.702
MOE_BLOCK = 512
ROPE_THETA = 10000.0
RMS_EPS = 1e-6
NEG_INF = -1e30
FORCED = 1e9

kernel_name = 'hybrid_nsa_rwkv7_moe_block'


def _split(x, sizes):
    out, off = [], 0
    for s in sizes:
        out.append(x[..., off:off + s])
        off += s
    return out


def rms_norm(x, g):
    x32 = x.astype(jnp.float32)
    y = x32 * lax.rsqrt(jnp.mean(x32 * x32, axis=-1, keepdims=True) + RMS_EPS)
    return (y * g.astype(jnp.float32)).astype(x.dtype)


def rope_tables(seq_len):
    half = HEAD_DIM // 2
    inv_freq = ROPE_THETA ** (-jnp.arange(half, dtype=jnp.float32) / half)
    ang = jnp.arange(seq_len, dtype=jnp.float32)[:, None] * inv_freq[None, :]
    return jnp.cos(ang), jnp.sin(ang)


def apply_rope(x, cos, sin):
    half = HEAD_DIM // 2
    x32 = x.astype(jnp.float32)
    x1, x2 = x32[..., :half], x32[..., half:]
    c = cos[None, :, None, :]
    s = sin[None, :, None, :]
    return jnp.concatenate([x1 * c - x2 * s, x2 * c + x1 * s], axis=-1).astype(x.dtype)


def cmp_to_slc_overlap(seq_len):
    nc = (seq_len - CMP_BLOCK) // CMP_STRIDE + 1
    ns = seq_len // SLC_BLOCK
    c0 = jnp.arange(nc)[:, None] * CMP_STRIDE
    s0 = jnp.arange(ns)[None, :] * SLC_BLOCK
    ov = jnp.minimum(c0 + CMP_BLOCK, s0 + SLC_BLOCK) - jnp.maximum(c0, s0)
    return jnp.maximum(ov, 0).astype(jnp.float32) / CMP_BLOCK


def compress_blocks(tok, pe, w1, w2):
    B, S, G, Dh = tok.shape
    nc = (S - CMP_BLOCK) // CMP_STRIDE + 1
    idx = jnp.arange(nc)[:, None] * CMP_STRIDE + jnp.arange(CMP_BLOCK)[None, :]
    blk = tok[:, idx] + pe[None, None, :, None, :].astype(tok.dtype)
    blk = blk.transpose(0, 1, 3, 2, 4).reshape(B, nc, G, CMP_BLOCK * Dh)
    return jax.nn.gelu(blk @ w1) @ w2


def nsa_attention(feat, cos, sin, overlap, k_pe, k_w1, k_w2, v_pe, v_w1, v_w2):
    B, S, _ = feat.shape
    G, HPG, H, Dh = NSA_KV_GROUPS, NSA_HPG, NSA_HEADS, HEAD_DIM
    f32 = jnp.float32
    q, kc, vc, ks, vs, kw, vw, gl = _split(feat, (NSA_WIDTH,) + (NSA_KV_WIDTH,) * 6 + (3 * H,))
    kv = lambda t: t.reshape(B, S, G, Dh)
    q = apply_rope(q.reshape(B, S, H, Dh), cos, sin)
    kc, ks, kw = (apply_rope(kv(t), cos, sin) for t in (kc, ks, kw))
    vc, vs, vw = kv(vc), kv(vs), kv(vw)
    k_cmp = compress_blocks(kc, k_pe, k_w1, k_w2)
    v_cmp = compress_blocks(vc, v_pe, v_w1, v_w2)
    nc = k_cmp.shape[1]
    ns = S // SLC_BLOCK
    n_sel = min(SLC_TOPK, ns)
    to_blocks = lambda t: t.reshape(B, ns, SLC_BLOCK, G, Dh).transpose(0, 3, 1, 2, 4)
    k_slc, v_slc = to_blocks(ks), to_blocks(vs)
    pad = lambda t: jnp.pad(t.transpose(0, 2, 1, 3), ((0, 0), (0, 0), (WINDOW, 0), (0, 0)))
    k_win, v_win = pad(kw), pad(vw)
    qg = q.reshape(B, S, G, HPG, Dh).transpose(0, 2, 3, 1, 4)
    gates = jax.nn.sigmoid(gl.astype(f32)).reshape(B, S, G, HPG, 3).transpose(0, 2, 3, 1, 4)
    scale = HEAD_DIM ** -0.5
    cmp_last = jnp.arange(nc) * CMP_STRIDE + CMP_BLOCK - 1
    blk_ids = jnp.arange(ns)
    b_idx = jnp.arange(B)[:, None, None, None]
    g_idx = jnp.arange(G)[None, :, None, None]
    win_off = jnp.arange(WINDOW + Q_BLOCK) - WINDOW
    slc_off = jnp.arange(SLC_BLOCK)

    def chunk(ci):
        s0 = ci * Q_BLOCK
        t = s0 + jnp.arange(Q_BLOCK)
        qc = lax.dynamic_slice_in_dim(qg, s0, Q_BLOCK, axis=3)
        gc = lax.dynamic_slice_in_dim(gates, s0, Q_BLOCK, axis=3)
        sc = jnp.einsum('bghqd,bngd->bghqn', qc, k_cmp, preferred_element_type=f32) * scale
        cmask = cmp_last[None, :] <= t[:, None]
        p_cmp = jnp.where(cmask, jax.nn.softmax(jnp.where(cmask, sc, NEG_INF), axis=-1), 0.0)
        o_cmp = jnp.einsum('bghqn,bngd->bghqd', p_cmp.astype(v_cmp.dtype), v_cmp)
        imp = jnp.einsum('bghqn,nj->bgqj', p_cmp, overlap)
        cur = t // SLC_BLOCK
        forced = ((blk_ids[None, :] == 0) | (blk_ids[None, :] == cur[:, None])
                  | (blk_ids[None, :] == cur[:, None] - 1))
        future = blk_ids[None, :] * SLC_BLOCK > t[:, None]
        imp = jnp.where(future, NEG_INF, jnp.where(forced, FORCED, imp))
        _, sel = lax.top_k(imp, n_sel)
        k_sel = k_slc[b_idx, g_idx, sel].reshape(B, G, Q_BLOCK, n_sel * SLC_BLOCK, Dh)
        v_sel = v_slc[b_idx, g_idx, sel].reshape(B, G, Q_BLOCK, n_sel * SLC_BLOCK, Dh)
        sel_pos = (sel[..., None] * SLC_BLOCK + slc_off).reshape(B, G, Q_BLOCK, n_sel * SLC_BLOCK)
        smask = (sel_pos <= t[None, None, :, None])[:, :, None]
        ss = jnp.einsum('bghqd,bgqnd->bghqn', qc, k_sel, preferred_element_type=f32) * scale
        p_slc = jax.nn.softmax(jnp.where(smask, ss, NEG_INF), axis=-1)
        o_slc = jnp.einsum('bghqn,bgqnd->bghqd', p_slc.astype(v_sel.dtype), v_sel)
        k_w = lax.dynamic_slice_in_dim(k_win, s0, WINDOW + Q_BLOCK, axis=2)
        v_w = lax.dynamic_slice_in_dim(v_win, s0, WINDOW + Q_BLOCK, axis=2)
        wpos = s0 + win_off
        wmask = ((wpos[None, :] <= t[:, None]) & (wpos[None, :] > t[:, None] - WINDOW)
                 & (wpos[None, :] >= 0))
        sw = jnp.einsum('bghqd,bgkd->bghqk', qc, k_w, preferred_element_type=f32) * scale
        p_win = jax.nn.softmax(jnp.where(wmask, sw, NEG_INF), axis=-1)
        o_win = jnp.einsum('bghqk,bgkd->bghqd', p_win.astype(v_w.dtype), v_w)
        return gc[..., 0:1] * o_cmp + gc[..., 1:2] * o_slc + gc[..., 2:3] * o_win

    out = lax.map(chunk, jnp.arange(S // Q_BLOCK))
    return out.transpose(1, 0, 4, 2, 3, 5).reshape(B, S, NSA_WIDTH).astype(feat.dtype)


def rwkv7_time_mix(feat, mu, w0, w2, a0, a2, g2, k_k, k_a, r_k, ln_w, ln_b):
    B, S, _ = feat.shape
    H, N = RWKV_HEADS, HEAD_DIM
    f32 = jnp.float32
    prev = jnp.pad(feat[:, :-1], ((0, 0), (1, 0), (0, 0)))
    feat = feat + (prev - feat) * mu
    r, k, v, wd, ad, gd = _split(feat, (RWKV_WIDTH,) * 3 + (RWKV_W_LORA, RWKV_A_LORA, RWKV_G_LORA))
    decay = jnp.exp(-DECAY_SCALE * jax.nn.sigmoid((w0 + jnp.tanh(wd) @ w2).astype(f32)))
    a = jax.nn.sigmoid((a0 + ad @ a2).astype(f32))
    g = jax.nn.sigmoid(gd) @ g2
    heads = lambda t: t.astype(f32).reshape(B, S, H, N)
    kk = heads(k * k_k)
    kk = kk / jnp.maximum(jnp.sqrt(jnp.sum(kk * kk, axis=-1, keepdims=True)), 1e-12)
    k = k.astype(f32) * (1.0 + (a - 1.0) * k_a.astype(f32))
    r_h, w_h, k_h, v_h, a_h = heads(r), heads(decay), heads(k), heads(v), heads(a)

    def step(state, inp):
        r_t, w_t, k_t, v_t, kk_t, a_t = inp
        sa = jnp.einsum('bhij,bhj->bhi', state, -kk_t)
        state = (state * w_t[:, :, None, :] + sa[..., None] * (kk_t * a_t)[:, :, None, :]
                 + v_t[..., None] * k_t[:, :, None, :])
        return state, jnp.einsum('bhij,bhj->bhi', state, r_t)

    xs = tuple(jnp.moveaxis(t, 1, 0) for t in (r_h, w_h, k_h, v_h, kk, a_h))
    _, ys = lax.scan(step, jnp.zeros((B, H, N, N), f32), xs)
    y = jnp.moveaxis(ys, 0, 1)
    mean = jnp.mean(y, axis=-1, keepdims=True)
    var = jnp.mean(jnp.square(y - mean), axis=-1, keepdims=True)
    y = ((y - mean) * lax.rsqrt(var + GN_EPS)).reshape(B, S, RWKV_WIDTH)
    y = y * ln_w.astype(f32) + ln_b.astype(f32)
    bonus = (jnp.sum(r_h * k_h * r_k.astype(f32), axis=-1, keepdims=True) * v_h).reshape(B, S, RWKV_WIDTH)
    return ((y + bonus) * g.astype(f32)).astype(feat.dtype)


def moe_ffn(h, router_w, router_b, w_gate_up, b_gate_up, w_down, b_down):
    B, S, D = h.shape
    T = B * S
    xf = h.reshape(T, D)
    logits = (xf @ router_w).astype(jnp.float32) + router_b.astype(jnp.float32)
    top_val, top_idx = lax.top_k(logits, TOP_K)
    top_w = jax.nn.softmax(top_val, axis=-1)
    n_assign = T * TOP_K
    e_flat = top_idx.reshape(-1)
    tok_flat = jnp.arange(n_assign, dtype=jnp.int32) // TOP_K
    w_flat = top_w.reshape(-1)
    order = jnp.argsort(e_flat)
    e_sorted = e_flat[order]
    counts = jnp.bincount(e_flat, length=N_EXPERTS)
    starts = jnp.cumsum(counts) - counts
    padded = ((counts + MOE_BLOCK - 1) // MOE_BLOCK) * MOE_BLOCK
    pad_ends = jnp.cumsum(padded)
    pad_starts = pad_ends - padded
    dest = pad_starts[e_sorted] + (jnp.arange(n_assign) - starts[e_sorted])
    n_blocks = -(-n_assign // MOE_BLOCK) + N_EXPERTS
    P = n_blocks * MOE_BLOCK
    slot_tok = jnp.full((P,), T, jnp.int32).at[dest].set(tok_flat[order])
    slot_w = jnp.zeros((P,), jnp.float32).at[dest].set(w_flat[order])
    blk_expert = jnp.minimum(
        jnp.searchsorted(pad_ends, jnp.arange(n_blocks) * MOE_BLOCK, side='right'), N_EXPERTS - 1)
    x_pad = jnp.concatenate([xf, jnp.zeros((1, D), xf.dtype)], axis=0)

    def expert_block(args):
        tok, wt, e = args
        xb = x_pad[tok]
        gu = xb @ w_gate_up[e] + b_gate_up[e]
        gate, up = gu[:, :D_EXPERT], gu[:, D_EXPERT:]
        gate = jnp.minimum(gate, SWIGLU_LIMIT)
        up = jnp.clip(up, -SWIGLU_LIMIT, SWIGLU_LIMIT)
        glu = gate * jax.nn.sigmoid(gate * SWIGLU_ALPHA)
        out = ((up + 1.0) * glu) @ w_down[e] + b_down[e]
        return out * wt[:, None].astype(out.dtype)

    outs = lax.map(expert_block, (slot_tok.reshape(n_blocks, MOE_BLOCK),
                                  slot_w.reshape(n_blocks, MOE_BLOCK), blk_expert))
    y = jnp.zeros((T + 1, D), outs.dtype).at[slot_tok].add(outs.reshape(P, D))
    return y[:T].reshape(B, S, D)


def setup_inputs(seed: int = 0) -> dict:
    key = jax.random.key(seed)
    keys = iter(jax.random.split(key, 64))

    def nrm(shape, std):
        return std * jax.random.normal(next(keys), shape, jnp.float32)

    def gain(shape):
        return 1.0 + nrm(shape, 0.05)

    L, D, F, E = DEPTH, D_MODEL, D_EXPERT, N_EXPERTS
    LB = CMP_BLOCK * HEAD_DIM
    return {
        'x': nrm((BATCH, SEQ, D), 1.0),
        'c': nrm((BATCH, D), 1.0),
        'ada_w': nrm((L, D, 6 * D), 0.5 * D ** -0.5),
        'ada_b': nrm((L, 6 * D), 0.02),
        'mix_pre_norm': gain((L, D)),
        'mix_post_norm': gain((L, D)),
        'ffn_pre_norm': gain((L, D)),
        'ffn_post_norm': gain((L, D)),
        'w_in': nrm((L, D, IN_COLS), D ** -0.5),
        'cmp_k_pe': nrm((L, CMP_BLOCK, HEAD_DIM), 0.1),
        'cmp_k_w1': nrm((L, LB, CMP_HIDDEN), LB ** -0.5),
        'cmp_k_w2': nrm((L, CMP_HIDDEN, HEAD_DIM), CMP_HIDDEN ** -0.5),
        'cmp_v_pe': nrm((L, CMP_BLOCK, HEAD_DIM), 0.1),
        'cmp_v_w1': nrm((L, LB, CMP_HIDDEN), LB ** -0.5),
        'cmp_v_w2': nrm((L, CMP_HIDDEN, HEAD_DIM), CMP_HIDDEN ** -0.5),
        'nsa_out_norm': gain((L, NSA_WIDTH)),
        'rwkv_mu': jax.random.uniform(next(keys), (L, RWKV_COLS), jnp.float32),
        'rwkv_w0': nrm((L, RWKV_WIDTH), 0.5),
        'rwkv_w2': nrm((L, RWKV_W_LORA, RWKV_WIDTH), 0.5 * RWKV_W_LORA ** -0.5),
        'rwkv_a0': nrm((L, RWKV_WIDTH), 0.1),
        'rwkv_a2': nrm((L, RWKV_A_LORA, RWKV_WIDTH), 0.5 * RWKV_A_LORA ** -0.5),
        'rwkv_g2': nrm((L, RWKV_G_LORA, RWKV_WIDTH), RWKV_G_LORA ** -0.5),
        'rwkv_k_k': 0.85 + nrm((L, RWKV_WIDTH), 0.05),
        'rwkv_k_a': gain((L, RWKV_WIDTH)),
        'rwkv_r_k': nrm((L, RWKV_HEADS, HEAD_DIM), 0.1),
        'rwkv_ln_w': gain((L, RWKV_WIDTH)),
        'rwkv_ln_b': nrm((L, RWKV_WIDTH), 0.02),
        'w_out': nrm((L, MIX_WIDTH, D), MIX_WIDTH ** -0.5),
        'router_w': nrm((L, D, E), D ** -0.5),
        'router_b': nrm((L, E), 0.01),
        'expert_w_gate_up': nrm((L, E, D, 2 * F), D ** -0.5),
        'expert_b_gate_up': nrm((L, E, 2 * F), 0.01),
        'expert_w_down': nrm((L, E, F, D), F ** -0.5),
        'expert_b_down': nrm((L, E, D), 0.01),
    }


def reference(x, c, ada_w, ada_b, mix_pre_norm, mix_post_norm, ffn_pre_norm, ffn_post_norm,
              w_in, cmp_k_pe, cmp_k_w1, cmp_k_w2, cmp_v_pe, cmp_v_w1, cmp_v_w2, nsa_out_norm,
              rwkv_mu, rwkv_w0, rwkv_w2, rwkv_a0, rwkv_a2, rwkv_g2, rwkv_k_k, rwkv_k_a, rwkv_r_k,
              rwkv_ln_w, rwkv_ln_b, w_out, router_w, router_b, expert_w_gate_up, expert_b_gate_up,
              expert_w_down, expert_b_down):
    S = x.shape[1]
    cos, sin = rope_tables(S)
    overlap = cmp_to_slc_overlap(S)
    c_act = jax.nn.silu(c)
    for l in range(DEPTH):
        ada = (c_act @ ada_w[l] + ada_b[l])[:, None, :]
        sh_m, sc_m, g_m, sh_f, sc_f, g_f = jnp.split(ada, 6, axis=-1)
        h = rms_norm(x, mix_pre_norm[l]) * (1.0 + sc_m) + sh_m
        proj = h @ w_in[l]
        o_nsa = nsa_attention(proj[..., :NSA_COLS], cos, sin, overlap, cmp_k_pe[l], cmp_k_w1[l],
                              cmp_k_w2[l], cmp_v_pe[l], cmp_v_w1[l], cmp_v_w2[l])
        o_rwkv = rwkv7_time_mix(proj[..., NSA_COLS:], rwkv_mu[l], rwkv_w0[l], rwkv_w2[l], rwkv_a0[l],
                                rwkv_a2[l], rwkv_g2[l], rwkv_k_k[l], rwkv_k_a[l], rwkv_r_k[l],
                                rwkv_ln_w[l], rwkv_ln_b[l])
        mixed = jnp.concatenate([rms_norm(o_nsa, nsa_out_norm[l]), o_rwkv], axis=-1) @ w_out[l]
        x = x + g_m * rms_norm(mixed, mix_post_norm[l])
        h = rms_norm(x, ffn_pre_norm[l]) * (1.0 + sc_f) + sh_f
        y = moe_ffn(h, router_w[l], router_b[l], expert_w_gate_up[l], expert_b_gate_up[l],
                    expert_w_down[l], expert_b_down[l])
        x = x + g_f * rms_norm(y, ffn_post_norm[l])
    return x
```

```python
import functools
import math

import jax
import jax.numpy as jnp
from jax import lax
from jax.experimental import pallas as pl
from jax.experimental.pallas import tpu as pltpu

F32 = jnp.float32
BF16 = jnp.bfloat16
I32 = jnp.int32

HEAD_DIM = 64
NSA_HEADS = 8
NSA_KV_GROUPS = 2
NSA_HPG = NSA_HEADS // NSA_KV_GROUPS
NSA_WIDTH = NSA_HEADS * HEAD_DIM
NSA_KV_WIDTH = NSA_KV_GROUPS * HEAD_DIM
CMP_BLOCK = 32
CMP_STRIDE = 16
SLC_BLOCK = 64
SLC_TOPK = 16
WINDOW = 512
Q_BLOCK = 128
RWKV_HEADS = 8
RWKV_WIDTH = RWKV_HEADS * HEAD_DIM
RWKV_W_LORA = 32
RWKV_A_LORA = 32
RWKV_G_LORA = 96
DECAY_SCALE = math.exp(-0.5)
GN_EPS = 64e-5
N_EXPERTS = 32
TOP_K = 4
SWIGLU_LIMIT = 7.0
SWIGLU_ALPHA = 1.702
ROPE_THETA = 10000.0
RMS_EPS = 1e-6
NEG_INF = -1e30
FORCED = 1e9
LOWEST = -3e38

LANE = 128
ROPE_COLS = NSA_WIDTH + 3 * NSA_KV_WIDTH
REST_COLS = 4 * LANE
RW_COLS = 3 * RWKV_WIDTH + 3 * LANE
SLC_KT = 256
RWKV_CHUNK = 64
MOE_BLOCK = 512
VMEM_LIMIT = 56 * 1024 * 1024


def _bdot(a, b):
    return jnp.dot(a.astype(BF16), b.astype(BF16), preferred_element_type=F32)


def _split2(a):
    hi = a.astype(BF16)
    lo = (a - hi.astype(F32)).astype(BF16)
    return hi, lo


def _dot_lhs2(a, b_bf16):
    hi, lo = _split2(a)
    return (jnp.dot(hi, b_bf16, preferred_element_type=F32)
            + jnp.dot(lo, b_bf16, preferred_element_type=F32))


def _dot3(a, b):
    ah, al = _split2(a)
    bh, bl = _split2(b)
    return (jnp.dot(ah, bh, preferred_element_type=F32)
            + jnp.dot(al, bh, preferred_element_type=F32)
            + jnp.dot(ah, bl, preferred_element_type=F32))


def _params(*sem):
    return pltpu.CompilerParams(dimension_semantics=sem, vmem_limit_bytes=VMEM_LIMIT)


def _ada_kernel(c_ref, w_ref, b_ref, o_ref):
    c = c_ref[...]
    act = c * jax.nn.sigmoid(c)
    o_ref[...] = _dot3(act, w_ref[...]) + b_ref[...]


def _ada(c, w, b):
    B, D = c.shape
    N = w.shape[1]
    tn = 1536
    return pl.pallas_call(
        _ada_kernel,
        grid=(N // tn,),
        in_specs=[pl.BlockSpec((B, D), lambda j: (0, 0)),
                  pl.BlockSpec((D, tn), lambda j: (0, j)),
                  pl.BlockSpec((1, tn), lambda j: (0, j))],
        out_specs=pl.BlockSpec((B, tn), lambda j: (0, j)),
        out_shape=jax.ShapeDtypeStruct((B, N), F32),
        compiler_params=_params("arbitrary"),
        name="ada",
    )(c, w, b.reshape(1, N))


def _inproj_kernel(x_ref, g_ref, sc_ref, sh_ref, w_ref, cos_ref, sin_ref,
                   q_ref, kc_ref, ksw_ref, vc_ref, vsw_ref, gl_ref, rw_ref):
    x = x_ref[0]
    ms = jnp.mean(x * x, axis=-1, keepdims=True)
    h = x * lax.rsqrt(ms + RMS_EPS) * g_ref[...]
    h = h * sc_ref[0] + sh_ref[0]
    hb = h.astype(BF16)
    R = ROPE_COLS
    main = jnp.dot(hb, w_ref[:, 0:R], preferred_element_type=F32)
    rot = jnp.dot(hb, w_ref[:, R:2 * R], preferred_element_type=F32)
    roped = main * cos_ref[...] + rot * sin_ref[...]
    q_ref[0] = roped[:, 0:NSA_WIDTH].astype(BF16)
    kc_ref[0] = roped[:, NSA_WIDTH:NSA_WIDTH + LANE]
    ksw_ref[0] = roped[:, NSA_WIDTH + LANE:R].astype(BF16)
    rest = jnp.dot(hb, w_ref[:, 2 * R:2 * R + REST_COLS], preferred_element_type=F32)
    vc_ref[0] = rest[:, 0:LANE]
    vsw_ref[0] = rest[:, LANE:3 * LANE].astype(BF16)
    gl_ref[0] = rest[:, 3 * LANE:4 * LANE]
    rw_ref[0] = jnp.dot(hb, w_ref[:, 2 * R + REST_COLS:], preferred_element_type=F32)


def _inproj(x, gain, sc1, sh, w_all, cos_t, sin_t):
    B, S, D = x.shape
    tm = min(256, S)
    NW = w_all.shape[1]
    R = ROPE_COLS
    tok = lambda n: pl.BlockSpec((1, tm, n), lambda b, i: (b, i, 0))
    vec = pl.BlockSpec((1, 1, D), lambda b, i: (b, 0, 0))
    outs = [(NSA_WIDTH, BF16), (LANE, F32), (2 * LANE, BF16), (LANE, F32), (2 * LANE, BF16),
            (LANE, F32), (RW_COLS, F32)]
    return pl.pallas_call(
        _inproj_kernel,
        grid=(B, S // tm),
        in_specs=[tok(D), pl.BlockSpec((1, D), lambda b, i: (0, 0)), vec, vec,
                  pl.BlockSpec((D, NW), lambda b, i: (0, 0)),
                  pl.BlockSpec((tm, R), lambda b, i: (i, 0)),
                  pl.BlockSpec((tm, R), lambda b, i: (i, 0))],
        out_specs=[tok(n) for n, _ in outs],
        out_shape=[jax.ShapeDtypeStruct((B, S, n), dt) for n, dt in outs],
        compiler_params=_params("parallel", "arbitrary"),
        name="inproj",
    )(x, gain.reshape(1, D), sc1, sh, w_all, cos_t, sin_t)


def _gelu_tanh(x):
    return 0.5 * x * (1.0 + jnp.tanh(math.sqrt(2.0 / math.pi) * (x + 0.044715 * (x * x * x))))


def _compress_kernel(r_ref, w1_ref, w2_ref, pe_ref, o_ref):
    R = r_ref[0, 0, 0].astype(BF16)
    nb, half = R.shape
    w1 = w1_ref[0].astype(BF16)
    top = jnp.dot(R, w1[0:half], preferred_element_type=F32)
    bot = jnp.dot(R, w1[half:2 * half], preferred_element_type=F32)
    pe_term = jnp.dot(pe_ref[0].astype(BF16), w1, preferred_element_type=F32)[0:1]
    bot_next = pltpu.roll(bot, nb - 1, 0)
    hid = _gelu_tanh(top + bot_next + pe_term)
    o_ref[0, 0, 0] = jnp.dot(hid.astype(BF16), w2_ref[0].astype(BF16),
                             preferred_element_type=F32).astype(BF16)


def _compress(rr, w1, w2, pe):
    _, B, G, nb, half = rr.shape
    hid = w1.shape[-1]
    return pl.pallas_call(
        _compress_kernel,
        grid=(2, B, G),
        in_specs=[pl.BlockSpec((1, 1, 1, nb, half), lambda a, b, g: (a, b, g, 0, 0)),
                  pl.BlockSpec((1, 2 * half, hid), lambda a, b, g: (a, 0, 0)),
                  pl.BlockSpec((1, hid, HEAD_DIM), lambda a, b, g: (a, 0, 0)),
                  pl.BlockSpec((1, 8, 2 * half), lambda a, b, g: (a, 0, 0))],
        out_specs=pl.BlockSpec((1, 1, 1, nb, HEAD_DIM), lambda a, b, g: (a, b, g, 0, 0)),
        out_shape=jax.ShapeDtypeStruct((2, B, G, nb, HEAD_DIM), BF16),
        compiler_params=_params("arbitrary", "arbitrary", "arbitrary"),
        name="compress",
    )(rr, w1, w2, pe)


def _nt_dot(a, b):
    return lax.dot_general(a, b, (((1,), (1,)), ((), ())), preferred_element_type=F32)


def _tile4(a):
    return jnp.concatenate([a, a, a, a], axis=0)


def _nsa_kernel(q_ref, ksw_ref, vsw_ref, cmp_ref, ov_ref, gl_ref, o_ref, *, seq_len):
    QB, Dh, HPG = Q_BLOCK, HEAD_DIM, NSA_HPG
    qi = pl.program_id(1)
    s0 = qi * QB
    ns = seq_len // SLC_BLOCK
    ncp = seq_len // CMP_STRIDE
    n_sel = min(SLC_TOPK, ns)
    kt_slc = min(SLC_KT, seq_len)
    t_col = s0 + lax.broadcasted_iota(I32, (QB, 1), 0)
    t4 = _tile4(t_col)
    gates = jax.nn.sigmoid(gl_ref[0])
    blk = lax.broadcasted_iota(I32, (QB, ns), 1)
    cur = lax.shift_right_logical(t_col, 6)
    forced = (blk == 0) | (blk == cur) | (blk == cur - 1)
    future = blk * SLC_BLOCK > t_col
    cmp_last = lax.broadcasted_iota(I32, (1, ncp), 1) * CMP_STRIDE + (CMP_BLOCK - 1)
    cmask = cmp_last <= t4

    for g in range(NSA_KV_GROUPS):
        qg = jnp.concatenate(
            [q_ref[0, :, (g * HPG + h) * Dh:(g * HPG + h + 1) * Dh] for h in range(HPG)], axis=0)
        kc = cmp_ref[0, 0, g]
        vc = cmp_ref[1, 0, g]
        sc = jnp.where(cmask, _nt_dot(qg, kc), NEG_INF)
        m = jnp.max(sc, axis=-1, keepdims=True)
        p = jnp.where(cmask, jnp.exp(sc - m), 0.0)
        l = jnp.sum(p, axis=-1, keepdims=True)
        p = p * (1.0 / jnp.maximum(l, 1e-30))
        o_cmp = jnp.dot(p.astype(BF16), vc, preferred_element_type=F32)
        psum = p[0:QB] + p[QB:2 * QB] + p[2 * QB:3 * QB] + p[3 * QB:4 * QB]
        imp = _dot_lhs2(psum, ov_ref[...])
        vals = jnp.where(future, NEG_INF, jnp.where(forced, FORCED, imp))
        sel = jnp.zeros((QB, ns), F32)
        for _ in range(n_sel):
            mx = jnp.max(vals, axis=-1, keepdims=True)
            idx = jnp.min(jnp.where(vals == mx, blk, ns), axis=-1, keepdims=True)
            pick = blk == idx
            sel = jnp.where(pick, 1.0, sel)
            vals = jnp.where(pick, LOWEST, vals)
        sel_b = sel.astype(BF16)

        def slc_body(j, carry):
            m_i, l_i, acc = carry
            k0 = pl.multiple_of(j * kt_slc, kt_slc)
            kt = ksw_ref[0, pl.ds(k0, kt_slc), g * Dh:(g + 1) * Dh]
            vt = vsw_ref[0, pl.ds(k0, kt_slc), g * Dh:(g + 1) * Dh]
            s = _nt_dot(qg, kt)
            eb = lax.broadcasted_iota(I32, (ns, kt_slc), 0)
            ec = lax.broadcasted_iota(I32, (ns, kt_slc), 1)
            expand = (eb == j * (kt_slc // SLC_BLOCK) + lax.shift_right_logical(ec, 6)).astype(BF16)
            msel = jnp.dot(sel_b, expand, preferred_element_type=F32)
            kpos = k0 + lax.broadcasted_iota(I32, (1, kt_slc), 1)
            bias = jnp.where((msel > 0.5) & (kpos <= t_col), 0.0, NEG_INF)
            s = s + _tile4(bias)
            m_new = jnp.maximum(m_i, jnp.max(s, axis=-1, keepdims=True))
            alpha = jnp.exp(m_i - m_new)
            pj = jnp.exp(s - m_new)
            l_new = alpha * l_i + jnp.sum(pj, axis=-1, keepdims=True)
            acc_new = alpha * acc + jnp.dot(pj.astype(BF16), vt, preferred_element_type=F32)
            return m_new, l_new, acc_new

        init = (jnp.full((4 * QB, 1), NEG_INF, F32), jnp.zeros((4 * QB, 1), F32),
                jnp.zeros((4 * QB, Dh), F32))
        n_tiles = (s0 + QB - 1) // kt_slc + 1
        _, l_s, acc_s = lax.fori_loop(0, n_tiles, slc_body, init)
        o_slc = acc_s * (1.0 / l_s)

        def win_body(jj, carry):
            m_i, l_i, acc = carry
            k0 = pl.multiple_of((qi - jj) * QB, QB)
            kt = ksw_ref[0, pl.ds(k0, QB), LANE + g * Dh:LANE + (g + 1) * Dh]
            vt = vsw_ref[0, pl.ds(k0, QB), LANE + g * Dh:LANE + (g + 1) * Dh]
            s = _nt_dot(qg, kt)
            kpos = k0 + lax.broadcasted_iota(I32, (1, QB), 1)
            bias = jnp.where((kpos <= t_col) & (kpos > t_col - WINDOW), 0.0, NEG_INF)
            s = s + _tile4(bias)
            m_new = jnp.maximum(m_i, jnp.max(s, axis=-1, keepdims=True))
            alpha = jnp.exp(m_i - m_new)
            pj = jnp.exp(s - m_new)
            l_new = alpha * l_i + jnp.sum(pj, axis=-1, keepdims=True)
            acc_new = alpha * acc + jnp.dot(pj.astype(BF16), vt, preferred_element_type=F32)
            return m_new, l_new, acc_new

        n_win = jnp.minimum(qi, WINDOW // QB) + 1
        _, l_w, acc_w = lax.fori_loop(0, n_win, win_body, init)
        o_win = acc_w * (1.0 / l_w)

        for h in range(HPG):
            c0 = (g * HPG + h) * 3
            rows = slice(h * QB, (h + 1) * QB)
            o = (gates[:, c0:c0 + 1] * o_cmp[rows] + gates[:, c0 + 1:c0 + 2] * o_slc[rows]
                 + gates[:, c0 + 2:c0 + 3] * o_win[rows])
            o_ref[0, :, (g * HPG + h) * Dh:(g * HPG + h + 1) * Dh] = o


def _nsa(q, ksw, vsw, cmp_kv, overlap, gl):
    B, S, _ = q.shape
    ncp = S // CMP_STRIDE
    ns = S // SLC_BLOCK
    G = NSA_KV_GROUPS
    return pl.pallas_call(
        functools.partial(_nsa_kernel, seq_len=S),
        grid=(B, S // Q_BLOCK),
        in_specs=[pl.BlockSpec((1, Q_BLOCK, NSA_WIDTH), lambda b, i: (b, i, 0)),
                  pl.BlockSpec((1, S, 2 * LANE), lambda b, i: (b, 0, 0)),
                  pl.BlockSpec((1, S, 2 * LANE), lambda b, i: (b, 0, 0)),
                  pl.BlockSpec((2, 1, G, ncp, HEAD_DIM), lambda b, i: (0, b, 0, 0, 0)),
                  pl.BlockSpec((ncp, ns), lambda b, i: (0, 0)),
                  pl.BlockSpec((1, Q_BLOCK, LANE), lambda b, i: (b, i, 0))],
        out_specs=pl.BlockSpec((1, Q_BLOCK, NSA_WIDTH), lambda b, i: (b, i, 0)),
        out_shape=jax.ShapeDtypeStruct((B, S, NSA_WIDTH), F32),
        compiler_params=_params("parallel", "arbitrary"),
        name="nsa",
    )(q, ksw, vsw, cmp_kv, overlap, gl)


def _rwkv_prep_kernel(f_ref, p_ref, mu_ref, w0_ref, w2_ref, a0_ref, a2_ref, g2_ref, kk_ref, ka_ref,
                      bd_ref, r_o, lw_o, k_o, v_o, kk_o, a_o, g_o):
    W = RWKV_WIDTH
    i = pl.program_id(1)
    f = f_ref[0]
    tm = f.shape[0]
    prev_last = jnp.where(i > 0, p_ref[0, 7:8, :], 0.0)
    rolled = pltpu.roll(f, 1, 0)
    row = lax.broadcasted_iota(I32, (tm, 1), 0)
    prev = jnp.where(row == 0, prev_last, rolled)
    f = f + (prev - f) * mu_ref[...]
    r, k, v = f[:, 0:W], f[:, W:2 * W], f[:, 2 * W:3 * W]
    wd = f[:, 3 * W:3 * W + LANE]
    ad = f[:, 3 * W + LANE:3 * W + 2 * LANE]
    gd = f[:, 3 * W + 2 * LANE:3 * W + 3 * LANE]
    lw_o[0] = -DECAY_SCALE * jax.nn.sigmoid(w0_ref[...] + _dot3(jnp.tanh(wd), w2_ref[...]))
    a = jax.nn.sigmoid(a0_ref[...] + _dot3(ad, a2_ref[...]))
    g_o[0] = _dot3(jax.nn.sigmoid(gd), g2_ref[...])
    kk = k * kk_ref[...]
    ss = _dot_lhs2(kk * kk, bd_ref[...])
    kk_o[0] = kk / jnp.maximum(jnp.sqrt(ss), 1e-12)
    k_o[0] = k * (1.0 + (a - 1.0) * ka_ref[...])
    r_o[0] = r
    v_o[0] = v
    a_o[0] = a


def _rwkv_prep(rw, mu, w0, w2, a0, a2, g2, k_k, k_a, bd):
    B, S, C = rw.shape
    W = RWKV_WIDTH
    tm = min(256, S)
    row = lambda n: pl.BlockSpec((1, n), lambda b, i: (0, 0))
    mat = pl.BlockSpec((LANE, W), lambda b, i: (0, 0))
    tok = pl.BlockSpec((1, tm, W), lambda b, i: (b, i, 0))
    return pl.pallas_call(
        _rwkv_prep_kernel,
        grid=(B, S // tm),
        in_specs=[pl.BlockSpec((1, tm, C), lambda b, i: (b, i, 0)),
                  pl.BlockSpec((1, 8, C), lambda b, i: (b, jnp.maximum(i * (tm // 8) - 1, 0), 0)),
                  row(C), row(W), mat, row(W), mat, mat, row(W), row(W),
                  pl.BlockSpec((W, W), lambda b, i: (0, 0))],
        out_specs=[tok] * 7,
        out_shape=[jax.ShapeDtypeStruct((B, S, W), F32)] * 7,
        compiler_params=_params("parallel", "arbitrary"),
        name="rwkv_prep",
    )(rw, rw, mu, w0, w2, a0, a2, g2, k_k, k_a, bd)


def _rwkv_scan_kernel(r_ref, lw_ref, k_ref, v_ref, kk_ref, a_ref, g_ref, lnw_ref, lnb_ref, rk_ref,
                      o_ref, st_ref):
    Dh = HEAD_DIM

    @pl.when(pl.program_id(1) == 0)
    def _():
        st_ref[...] = jnp.zeros_like(st_ref)

    r, lw, k, v, kk, a, gate = (ref[0] for ref in (r_ref, lw_ref, k_ref, v_ref, kk_ref, a_ref, g_ref))
    C = r.shape[0]
    ri = lax.broadcasted_iota(I32, (C, C), 0)
    ci = lax.broadcasted_iota(I32, (C, C), 1)
    incl = ri >= ci
    strict = ri > ci
    tri = incl.astype(BF16)
    l_hi = lw.astype(BF16)
    l_r1 = lw - l_hi.astype(F32)
    l_mid = l_r1.astype(BF16)
    l_lo = (l_r1 - l_mid.astype(F32)).astype(BF16)
    L = (jnp.dot(tri, l_hi, preferred_element_type=F32) + jnp.dot(tri, l_mid, preferred_element_type=F32)
         + jnp.dot(tri, l_lo, preferred_element_type=F32))
    e_l = jnp.exp(L)
    e_inv = jnp.exp(-L)
    e_end = jnp.exp(L[C - 1:C, :] - L)
    b = kk * a
    x1 = jnp.concatenate([kk * jnp.exp(L - lw), r * e_l], axis=0).astype(BF16)
    x2t = jnp.concatenate([k * e_inv, b * e_inv], axis=0).T.astype(BF16)
    zt = jnp.concatenate([k * e_end, -(b * e_end)], axis=0).T.astype(BF16)
    gt = jnp.concatenate([e_l, e_l], axis=0).T
    eye = (ri == ci).astype(F32)
    n_double = max(int(math.log2(C)) - 1, 0)

    for h in range(RWKV_HEADS):
        sl = slice(h * Dh, (h + 1) * Dh)
        x1h = x1[:, sl]
        h0 = st_ref[h]
        amat = jnp.dot(x1h, x2t[sl, :], preferred_element_type=F32)
        p0 = jnp.dot(x1h, h0.astype(BF16), preferred_element_type=F32)
        a_kk = jnp.where(strict, amat[0:C, 0:C], 0.0)
        a_kb = jnp.where(strict, amat[0:C, C:2 * C], 0.0)
        a_rk = jnp.where(incl, amat[C:2 * C, 0:C], 0.0)
        a_rb = jnp.where(incl, amat[C:2 * C, C:2 * C], 0.0)
        vh = v[:, sl]
        nmat = -a_kb
        tinv = eye + nmat
        for _ in range(n_double):
            nmat = _bdot(nmat, nmat)
            tinv = tinv + _bdot(tinv, nmat)
        u = _bdot(tinv, p0[0:C] + _bdot(a_kk, vh))
        vu = jnp.concatenate([vh, u], axis=0).astype(BF16)
        y = p0[C:2 * C] + jnp.dot(jnp.concatenate([a_rk, -a_rb], axis=1).astype(BF16), vu,
                                  preferred_element_type=F32)
        st_ref[h] = gt[sl, C - 1:C] * h0 + jnp.dot(zt[sl, :], vu, preferred_element_type=F32)
        mean = jnp.mean(y, axis=-1, keepdims=True)
        yc = y - mean
        var = jnp.mean(yc * yc, axis=-1, keepdims=True)
        yn = yc * lax.rsqrt(var + GN_EPS) * lnw_ref[:, sl] + lnb_ref[:, sl]
        bonus = jnp.sum(r[:, sl] * k[:, sl] * rk_ref[:, sl], axis=-1, keepdims=True) * vh
        o_ref[0, :, sl] = (yn + bonus) * gate[:, sl]


def _rwkv_scan(r, lw, k, v, kk, a, g, ln_w, ln_b, r_k):
    B, S, W = r.shape
    C = min(RWKV_CHUNK, S)
    tok = pl.BlockSpec((1, C, W), lambda b, i: (b, i, 0))
    row = pl.BlockSpec((1, W), lambda b, i: (0, 0))
    return pl.pallas_call(
        _rwkv_scan_kernel,
        grid=(B, S // C),
        in_specs=[tok] * 7 + [row] * 3,
        out_specs=tok,
        out_shape=jax.ShapeDtypeStruct((B, S, W), F32),
        scratch_shapes=[pltpu.VMEM((RWKV_HEADS, HEAD_DIM, HEAD_DIM), F32)],
        compiler_params=_params("parallel", "arbitrary"),
        name="rwkv_scan",
    )(r, lw, k, v, kk, a, g, ln_w, ln_b, r_k)


def _rms(x, g):
    return x * lax.rsqrt(jnp.mean(x * x, axis=-1, keepdims=True) + RMS_EPS) * g


def _outproj_kernel(on_ref, or_ref, x_ref, gn_ref, wt_ref, wb_ref, gpost_ref, gm_ref, gpre_ref,
                    scf_ref, shf_ref, rw_ref, rb_ref, x1_ref, h2_ref, idx_ref, wgt_ref):
    on = _rms(on_ref[0], gn_ref[...])
    mixed = (jnp.dot(on.astype(BF16), wt_ref[...], preferred_element_type=F32)
             + jnp.dot(or_ref[0].astype(BF16), wb_ref[...], preferred_element_type=F32))
    x1 = x_ref[0] + gm_ref[0] * _rms(mixed, gpost_ref[...])
    x1_ref[0] = x1
    h2 = _rms(x1, gpre_ref[...]) * scf_ref[0] + shf_ref[0]
    h2_ref[0] = h2.astype(BF16)
    logits = _dot3(h2, rw_ref[...]) + rb_ref[...]
    lane = lax.broadcasted_iota(I32, logits.shape, 1)
    vals = logits
    top_v, top_i = [], []
    for _ in range(TOP_K):
        mx = jnp.max(vals, axis=-1, keepdims=True)
        ix = jnp.min(jnp.where(vals == mx, lane, LANE), axis=-1, keepdims=True)
        top_v.append(mx)
        top_i.append(ix)
        vals = jnp.where(lane == ix, LOWEST, vals)
    ex = [jnp.exp(tv - top_v[0]) for tv in top_v]
    inv = 1.0 / (ex[0] + ex[1] + ex[2] + ex[3])
    idx = jnp.zeros(logits.shape, I32)
    wgt = jnp.zeros(logits.shape, F32)
    for kk in range(TOP_K):
        idx = jnp.where(lane == kk, top_i[kk], idx)
        wgt = jnp.where(lane == kk, ex[kk] * inv, wgt)
    idx_ref[0] = idx
    wgt_ref[0] = wgt


def _outproj(o_nsa, o_rwkv, x, gn, w_top, w_bot, g_post, g_m, g_pre, sc1_f, sh_f, router_w, router_b):
    B, S, D = x.shape
    tm = min(256, S)
    W = o_nsa.shape[-1]
    tok = lambda n: pl.BlockSpec((1, tm, n), lambda b, i: (b, i, 0))
    vec = pl.BlockSpec((1, 1, D), lambda b, i: (b, 0, 0))
    row = lambda n: pl.BlockSpec((1, n), lambda b, i: (0, 0))
    return pl.pallas_call(
        _outproj_kernel,
        grid=(B, S // tm),
        in_specs=[tok(W), tok(W), tok(D), row(W),
                  pl.BlockSpec((W, D), lambda b, i: (0, 0)), pl.BlockSpec((W, D), lambda b, i: (0, 0)),
                  row(D), vec, row(D), vec, vec,
                  pl.BlockSpec((D, LANE), lambda b, i: (0, 0)), row(LANE)],
        out_specs=[tok(D), tok(D), tok(LANE), tok(LANE)],
        out_shape=[jax.ShapeDtypeStruct((B, S, D), F32), jax.ShapeDtypeStruct((B, S, D), BF16),
                   jax.ShapeDtypeStruct((B, S, LANE), I32), jax.ShapeDtypeStruct((B, S, LANE), F32)],
        compiler_params=_params("parallel", "arbitrary"),
        name="outproj",
    )(o_nsa, o_rwkv, x, gn, w_top, w_bot, g_post, g_m, g_pre, sc1_f, sh_f, router_w, router_b)


def _expert_kernel(be_ref, nu_ref, x_ref, wgu_ref, bgu_ref, wd_ref, bd_ref, o_ref):
    F = wd_ref.shape[1]

    @pl.when(pl.program_id(0) < nu_ref[0])
    def _():
        gu = jnp.dot(x_ref[...], wgu_ref[0], preferred_element_type=F32) + bgu_ref[0]
        gate = jnp.minimum(gu[:, 0:F], SWIGLU_LIMIT)
        up = jnp.clip(gu[:, F:2 * F], -SWIGLU_LIMIT, SWIGLU_LIMIT)
        glu = gate * jax.nn.sigmoid(gate * SWIGLU_ALPHA)
        o_ref[...] = jnp.dot(((up + 1.0) * glu).astype(BF16), wd_ref[0],
                             preferred_element_type=F32) + bd_ref[0]


def _experts(blk_expert, n_used, x_sorted, w_gu, b_gu, w_d, b_d):
    P, D = x_sorted.shape
    E, _, F2 = w_gu.shape
    F = F2 // 2
    nb = P // MOE_BLOCK
    grid_spec = pltpu.PrefetchScalarGridSpec(
        num_scalar_prefetch=2,
        grid=(nb,),
        in_specs=[pl.BlockSpec((MOE_BLOCK, D), lambda i, be, nu: (i, 0)),
                  pl.BlockSpec((1, D, F2), lambda i, be, nu: (be[i], 0, 0)),
                  pl.BlockSpec((1, 1, F2), lambda i, be, nu: (be[i], 0, 0)),
                  pl.BlockSpec((1, F, D), lambda i, be, nu: (be[i], 0, 0)),
                  pl.BlockSpec((1, 1, D), lambda i, be, nu: (be[i], 0, 0))],
        out_specs=pl.BlockSpec((MOE_BLOCK, D), lambda i, be, nu: (i, 0)),
    )
    return pl.pallas_call(
        _expert_kernel,
        grid_spec=grid_spec,
        out_shape=jax.ShapeDtypeStruct((P, D), F32),
        compiler_params=_params("arbitrary"),
        name="experts",
    )(blk_expert, n_used, x_sorted, w_gu, b_gu.reshape(E, 1, F2), w_d, b_d.reshape(E, 1, D))


def _combine_kernel(y_ref, w_ref, x1_ref, gpost_ref, gf_ref, o_ref):
    w = w_ref[0]
    y = (w[:, 0:1] * y_ref[0] + w[:, 1:2] * y_ref[1] + w[:, 2:3] * y_ref[2] + w[:, 3:4] * y_ref[3])
    o_ref[0] = x1_ref[0] + gf_ref[0] * _rms(y, gpost_ref[...])


def _combine(y4, wgt, x1, g_post, g_f):
    B, S, D = x1.shape
    tm = min(256, S)
    nt = S // tm
    return pl.pallas_call(
        _combine_kernel,
        grid=(B, nt),
        in_specs=[pl.BlockSpec((TOP_K, tm, D), lambda b, i: (0, b * nt + i, 0)),
                  pl.BlockSpec((1, tm, LANE), lambda b, i: (b, i, 0)),
                  pl.BlockSpec((1, tm, D), lambda b, i: (b, i, 0)),
                  pl.BlockSpec((1, D), lambda b, i: (0, 0)),
                  pl.BlockSpec((1, 1, D), lambda b, i: (b, 0, 0))],
        out_specs=pl.BlockSpec((1, tm, D), lambda b, i: (b, i, 0)),
        out_shape=jax.ShapeDtypeStruct((B, S, D), F32),
        compiler_params=_params("parallel", "arbitrary"),
        name="combine",
    )(y4, wgt, x1, g_post, g_f)


def _rot_cols(w):
    d, n = w.shape
    w4 = w.reshape(d, n // HEAD_DIM, 2, HEAD_DIM // 2)
    return jnp.concatenate([-w4[:, :, 1:2], w4[:, :, 0:1]], axis=2).reshape(d, n)


def _pad_cols(w, n):
    return jnp.pad(w, ((0, 0), (0, n - w.shape[1])))


def _pad_rows(w, n):
    return jnp.pad(w, ((0, n - w.shape[0]), (0, 0)))


def _layer(x, ada, l, mix_pre_norm, mix_post_norm, ffn_pre_norm, ffn_post_norm, w_in, cmp_k_pe, cmp_k_w1,
           cmp_k_w2, cmp_v_pe, cmp_v_w1, cmp_v_w2, nsa_out_norm, rwkv_mu, rwkv_w0, rwkv_w2, rwkv_a0, rwkv_a2,
           rwkv_g2, rwkv_k_k, rwkv_k_a, rwkv_r_k, rwkv_ln_w, rwkv_ln_b, w_out, router_w, router_b,
           expert_w_gate_up, expert_b_gate_up, expert_w_down, expert_b_down):
    B, S, D = x.shape
    T = B * S
    G, Dh, W = NSA_KV_GROUPS, HEAD_DIM, RWKV_WIDTH
    sh_m, sc_m, g_m, sh_f, sc_f, g_f = [a.reshape(B, 1, D) for a in jnp.split(ada, 6, axis=-1)]

    wi = w_in[l]
    KV = NSA_KV_WIDTH
    o = NSA_WIDTH
    wq, wkc, wvc, wks, wvs, wkw, wvw = (wi[:, 0:o],) + tuple(wi[:, o + j * KV:o + (j + 1) * KV] for j in range(6))
    wgl = wi[:, o + 6 * KV:o + 6 * KV + 3 * NSA_HEADS]
    wr = wi[:, o + 6 * KV + 3 * NSA_HEADS:]
    w_rope = jnp.concatenate([wq, wkc, wks, wkw], axis=1)
    w_rest = _pad_cols(jnp.concatenate([wvc, wvs, wvw, wgl], axis=1), REST_COLS)
    lora = [(3 * W, RWKV_W_LORA), (3 * W + RWKV_W_LORA, RWKV_A_LORA),
            (3 * W + RWKV_W_LORA + RWKV_A_LORA, RWKV_G_LORA)]
    pad_lora = lambda a: jnp.concatenate([_pad_cols(a[..., s:s + n], LANE) for s, n in lora], axis=-1)
    w_rw = jnp.concatenate([wr[:, 0:3 * W], pad_lora(wr)], axis=1)
    w_all = jnp.concatenate([w_rope, _rot_cols(w_rope), w_rest, w_rw], axis=1).astype(BF16)
    mu_row = rwkv_mu[l].reshape(1, -1)
    mu = jnp.concatenate([mu_row[:, 0:3 * W], pad_lora(mu_row)], axis=1)

    half = Dh // 2
    inv_freq = ROPE_THETA ** (-jnp.arange(half, dtype=F32) / half)
    ang = jnp.arange(S, dtype=F32)[:, None] * inv_freq[None, :]
    n_rope_heads = ROPE_COLS // Dh
    col_scale = jnp.where(jnp.arange(ROPE_COLS) < NSA_WIDTH, Dh ** -0.5, 1.0).astype(F32)
    cos_t = jnp.tile(jnp.cos(ang), (1, 2 * n_rope_heads)) * col_scale
    sin_t = jnp.tile(jnp.sin(ang), (1, 2 * n_rope_heads)) * col_scale

    q, kc, ksw, vc, vsw, gl, rw = _inproj(x, mix_pre_norm[l], 1.0 + sc_m, sh_m, w_all, cos_t, sin_t)

    nb = S // CMP_STRIDE
    regroup = lambda t: t.reshape(B, nb, CMP_STRIDE, G, Dh).transpose(0, 3, 1, 2, 4).reshape(B, G, nb, CMP_STRIDE * Dh)
    rr = jnp.stack([regroup(kc), regroup(vc)])
    pe = jnp.stack([cmp_k_pe[l].reshape(1, -1), cmp_v_pe[l].reshape(1, -1)])
    pe = jnp.broadcast_to(pe, (2, 8, pe.shape[-1]))
    cmp_kv = _compress(rr, jnp.stack([cmp_k_w1[l], cmp_v_w1[l]]), jnp.stack([cmp_k_w2[l], cmp_v_w2[l]]), pe)
    ns = S // SLC_BLOCK
    c0 = jnp.arange(nb)[:, None] * CMP_STRIDE
    b0 = jnp.arange(ns)[None, :] * SLC_BLOCK
    ov = jnp.maximum(jnp.minimum(c0 + CMP_BLOCK, b0 + SLC_BLOCK) - jnp.maximum(c0, b0), 0)
    overlap = (ov.astype(F32) / CMP_BLOCK).astype(BF16)
    o_nsa = _nsa(q, ksw, vsw, cmp_kv, overlap, gl)

    bd = (jnp.arange(W)[:, None] // Dh == jnp.arange(W)[None, :] // Dh).astype(BF16)
    row = lambda a: a.reshape(1, -1)
    r, lw, k, v, kk, a, g = _rwkv_prep(
        rw, mu, row(rwkv_w0[l]), _pad_rows(rwkv_w2[l], LANE), row(rwkv_a0[l]), _pad_rows(rwkv_a2[l], LANE),
        _pad_rows(rwkv_g2[l], LANE), row(rwkv_k_k[l]), row(rwkv_k_a[l]), bd)
    o_rwkv = _rwkv_scan(r, lw, k, v, kk, a, g, row(rwkv_ln_w[l]), row(rwkv_ln_b[l]), row(rwkv_r_k[l]))

    wo = w_out[l].astype(BF16)
    rb = jnp.concatenate([router_b[l], jnp.full((LANE - N_EXPERTS,), NEG_INF, F32)]).reshape(1, LANE)
    x1, h2, idx, wgt = _outproj(o_nsa, o_rwkv, x, row(nsa_out_norm[l]), wo[0:NSA_WIDTH], wo[NSA_WIDTH:],
                                row(mix_post_norm[l]), g_m, row(ffn_pre_norm[l]), 1.0 + sc_f, sh_f,
                                _pad_cols(router_w[l], LANE), rb)

    n_assign = T * TOP_K
    e_flat = idx.reshape(T, LANE)[:, 0:TOP_K].reshape(-1)
    order = jnp.argsort(e_flat)
    e_sorted = e_flat[order]
    counts = jnp.bincount(e_flat, length=N_EXPERTS)
    starts = jnp.cumsum(counts) - counts
    padded = ((counts + MOE_BLOCK - 1) // MOE_BLOCK) * MOE_BLOCK
    pad_ends = jnp.cumsum(padded)
    pad_starts = pad_ends - padded
    dest = (pad_starts[e_sorted] + (jnp.arange(n_assign) - starts[e_sorted])).astype(I32)
    n_blocks = -(-n_assign // MOE_BLOCK) + N_EXPERTS
    P = n_blocks * MOE_BLOCK
    slot_tok = jnp.full((P,), T, I32).at[dest].set((order // TOP_K).astype(I32))
    pos = jnp.zeros((n_assign,), I32).at[order].set(dest)
    blk_expert = jnp.minimum(
        jnp.searchsorted(pad_ends, jnp.arange(n_blocks) * MOE_BLOCK, side='right'), N_EXPERTS - 1).astype(I32)
    n_used = (pad_ends[-1] // MOE_BLOCK).astype(I32).reshape(1)

    h2_pad = jnp.concatenate([h2.reshape(T, D), jnp.zeros((1, D), BF16)], axis=0)
    x_sorted = h2_pad[slot_tok]
    y_sorted = _experts(blk_expert, n_used, x_sorted, expert_w_gate_up[l].astype(BF16), expert_b_gate_up[l],
                        expert_w_down[l].astype(BF16), expert_b_down[l])
    y4 = y_sorted[pos.reshape(T, TOP_K).T]
    return _combine(y4, wgt, x1, row(ffn_post_norm[l]), g_f)


def kernel(x, c, ada_w, ada_b, mix_pre_norm, mix_post_norm, ffn_pre_norm, ffn_post_norm, w_in, cmp_k_pe, cmp_k_w1, cmp_k_w2, cmp_v_pe, cmp_v_w1, cmp_v_w2, nsa_out_norm, rwkv_mu, rwkv_w0, rwkv_w2, rwkv_a0, rwkv_a2, rwkv_g2, rwkv_k_k, rwkv_k_a, rwkv_r_k, rwkv_ln_w, rwkv_ln_b, w_out, router_w, router_b, expert_w_gate_up, expert_b_gate_up, expert_w_down, expert_b_down):
    for l in range(ada_w.shape[0]):
        ada = _ada(c, ada_w[l], ada_b[l])
        x = _layer(x, ada, l, mix_pre_norm, mix_post_norm, ffn_pre_norm, ffn_post_norm, w_in, cmp_k_pe,
                   cmp_k_w1, cmp_k_w2, cmp_v_pe, cmp_v_w1, cmp_v_w2, nsa_out_norm, rwkv_mu, rwkv_w0, rwkv_w2,
                   rwkv_a0, rwkv_a2, rwkv_g2, rwkv_k_k, rwkv_k_a, rwkv_r_k, rwkv_ln_w, rwkv_ln_b, w_out,
                   router_w, router_b, expert_w_gate_up, expert_b_gate_up, expert_w_down, expert_b_down)
    return x
```

```python
import functools
import math

import jax
import jax.numpy as jnp
from jax import lax
from jax.experimental import pallas as pl
from jax.experimental.pallas import tpu as pltpu

F32 = jnp.float32
BF16 = jnp.bfloat16
I32 = jnp.int32

HEAD_DIM = 64
NSA_HEADS = 8
NSA_KV_GROUPS = 2
NSA_HPG = NSA_HEADS // NSA_KV_GROUPS
NSA_WIDTH = NSA_HEADS * HEAD_DIM
NSA_KV_WIDTH = NSA_KV_GROUPS * HEAD_DIM
CMP_BLOCK = 32
CMP_STRIDE = 16
SLC_BLOCK = 64
SLC_TOPK = 16
WINDOW = 512
Q_BLOCK = 128
RWKV_HEADS = 8
RWKV_WIDTH = RWKV_HEADS * HEAD_DIM
RWKV_W_LORA = 32
RWKV_A_LORA = 32
RWKV_G_LORA = 96
DECAY_SCALE = math.exp(-0.5)
GN_EPS = 64e-5
N_EXPERTS = 32
TOP_K = 4
SWIGLU_LIMIT = 7.0
SWIGLU_ALPHA = 1.702
ROPE_THETA = 10000.0
RMS_EPS = 1e-6
NEG_INF = -1e30
FORCED = 1e9
LOWEST = -3e38

LANE = 128
ROPE_COLS = NSA_WIDTH + 3 * NSA_KV_WIDTH
REST_COLS = 4 * LANE
RW_COLS = 3 * RWKV_WIDTH + 3 * LANE
SLC_KT = 256
SLC_UNROLL = 2
RWKV_CHUNK = 64
MOE_BLOCK = 512
VMEM_LIMIT = 56 * 1024 * 1024


def _bdot(a, b):
    return jnp.dot(a.astype(BF16), b.astype(BF16), preferred_element_type=F32)


def _split2(a):
    hi = a.astype(BF16)
    lo = (a - hi.astype(F32)).astype(BF16)
    return hi, lo


def _dot_lhs2(a, b_bf16):
    hi, lo = _split2(a)
    return (jnp.dot(hi, b_bf16, preferred_element_type=F32)
            + jnp.dot(lo, b_bf16, preferred_element_type=F32))


def _dot3(a, b):
    ah, al = _split2(a)
    bh, bl = _split2(b)
    return (jnp.dot(ah, bh, preferred_element_type=F32)
            + jnp.dot(al, bh, preferred_element_type=F32)
            + jnp.dot(ah, bl, preferred_element_type=F32))


def _params(*sem):
    return pltpu.CompilerParams(dimension_semantics=sem, vmem_limit_bytes=VMEM_LIMIT)


def _ada_kernel(c_ref, w_ref, b_ref, o_ref):
    c = c_ref[...]
    act = c * jax.nn.sigmoid(c)
    o_ref[...] = _dot3(act, w_ref[...]) + b_ref[...]


def _ada(c, w, b):
    B, D = c.shape
    N = w.shape[1]
    tn = 1536
    return pl.pallas_call(
        _ada_kernel,
        grid=(N // tn,),
        in_specs=[pl.BlockSpec((B, D), lambda j: (0, 0)),
                  pl.BlockSpec((D, tn), lambda j: (0, j)),
                  pl.BlockSpec((1, tn), lambda j: (0, j))],
        out_specs=pl.BlockSpec((B, tn), lambda j: (0, j)),
        out_shape=jax.ShapeDtypeStruct((B, N), F32),
        compiler_params=_params("arbitrary"),
        name="ada",
    )(c, w, b.reshape(1, N))


def _inproj_kernel(x_ref, g_ref, sc_ref, sh_ref, w_ref, cos_ref, sin_ref,
                   q_ref, kc_ref, ksw_ref, vc_ref, vsw_ref, gl_ref, rw_ref):
    x = x_ref[0]
    ms = jnp.mean(x * x, axis=-1, keepdims=True)
    h = x * lax.rsqrt(ms + RMS_EPS) * g_ref[...]
    h = h * sc_ref[0] + sh_ref[0]
    hb = h.astype(BF16)
    R = ROPE_COLS
    main = jnp.dot(hb, w_ref[:, 0:R], preferred_element_type=F32)
    rot = jnp.dot(hb, w_ref[:, R:2 * R], preferred_element_type=F32)
    roped = main * cos_ref[...] + rot * sin_ref[...]
    q_ref[0] = roped[:, 0:NSA_WIDTH].T.astype(BF16)
    kc_ref[0] = roped[:, NSA_WIDTH:NSA_WIDTH + LANE]
    ksw_ref[0] = roped[:, NSA_WIDTH + LANE:R].astype(BF16)
    rest = jnp.dot(hb, w_ref[:, 2 * R:2 * R + REST_COLS], preferred_element_type=F32)
    vc_ref[0] = rest[:, 0:LANE]
    vsw_ref[0] = rest[:, LANE:3 * LANE].T.astype(BF16)
    gl_ref[0] = rest[:, 3 * LANE:4 * LANE].T
    rw_ref[0] = jnp.dot(hb, w_ref[:, 2 * R + REST_COLS:], preferred_element_type=F32)


def _inproj(x, gain, sc1, sh, w_all, cos_t, sin_t):
    B, S, D = x.shape
    tm = min(256, S)
    NW = w_all.shape[1]
    R = ROPE_COLS
    tok = lambda n: pl.BlockSpec((1, tm, n), lambda b, i: (b, i, 0))
    tok_t = lambda n: pl.BlockSpec((1, n, tm), lambda b, i: (b, 0, i))
    vec = pl.BlockSpec((1, 1, D), lambda b, i: (b, 0, 0))
    outs = [(NSA_WIDTH, BF16, True), (LANE, F32, False), (2 * LANE, BF16, False), (LANE, F32, False),
            (2 * LANE, BF16, True), (LANE, F32, True), (RW_COLS, F32, False)]
    return pl.pallas_call(
        _inproj_kernel,
        grid=(B, S // tm),
        in_specs=[tok(D), pl.BlockSpec((1, D), lambda b, i: (0, 0)), vec, vec,
                  pl.BlockSpec((D, NW), lambda b, i: (0, 0)),
                  pl.BlockSpec((tm, R), lambda b, i: (i, 0)),
                  pl.BlockSpec((tm, R), lambda b, i: (i, 0))],
        out_specs=[tok_t(n) if tr else tok(n) for n, _, tr in outs],
        out_shape=[jax.ShapeDtypeStruct((B, n, S) if tr else (B, S, n), dt) for n, dt, tr in outs],
        compiler_params=_params("parallel", "arbitrary"),
        name="inproj",
    )(x, gain.reshape(1, D), sc1, sh, w_all, cos_t, sin_t)


def _gelu_tanh(x):
    return 0.5 * x * (1.0 + jnp.tanh(math.sqrt(2.0 / math.pi) * (x + 0.044715 * (x * x * x))))


def _compress_kernel(r_ref, w1_ref, w2_ref, pe_ref, o_ref, *, transposed):
    R = r_ref[0, 0].astype(BF16)
    nb, half = R.shape
    w1 = w1_ref[...].astype(BF16)
    top = jnp.dot(R, w1[0:half], preferred_element_type=F32)
    bot = jnp.dot(R, w1[half:2 * half], preferred_element_type=F32)
    pe_term = jnp.dot(pe_ref[...].astype(BF16), w1, preferred_element_type=F32)[0:1]
    bot_next = pltpu.roll(bot, nb - 1, 0)
    hid = _gelu_tanh(top + bot_next + pe_term)
    if transposed:
        out = jnp.dot(w2_ref[...].astype(BF16), hid.T.astype(BF16), preferred_element_type=F32)
    else:
        out = jnp.dot(hid.astype(BF16), w2_ref[...].astype(BF16), preferred_element_type=F32)
    o_ref[0, 0] = out.astype(BF16)


def _compress(rr, w1, w2, pe, transposed):
    B, G, nb, half = rr.shape
    hid = w1.shape[-1]
    oshape = (HEAD_DIM, nb) if transposed else (nb, HEAD_DIM)
    return pl.pallas_call(
        functools.partial(_compress_kernel, transposed=transposed),
        grid=(B, G),
        in_specs=[pl.BlockSpec((1, 1, nb, half), lambda b, g: (b, g, 0, 0)),
                  pl.BlockSpec((2 * half, hid), lambda b, g: (0, 0)),
                  pl.BlockSpec(w2.shape, lambda b, g: (0, 0)),
                  pl.BlockSpec((8, 2 * half), lambda b, g: (0, 0))],
        out_specs=pl.BlockSpec((1, 1) + oshape, lambda b, g: (b, g, 0, 0)),
        out_shape=jax.ShapeDtypeStruct((B, G) + oshape, BF16),
        compiler_params=_params("arbitrary", "arbitrary"),
        name="compress_v" if transposed else "compress_k",
    )(rr, w1, w2, pe)


def _lanes4(a):
    return jnp.concatenate([a, a, a, a], axis=1)


def _nsa_kernel(qt_ref, ksw_ref, vt_ref, kcmp_ref, vcmpt_ref, ovt_ref, glt_ref, o_ref, selb_ref, *, seq_len):
    QB, Dh, HPG = Q_BLOCK, HEAD_DIM, NSA_HPG
    qi = pl.program_id(1)
    s0 = qi * QB
    ns = seq_len // SLC_BLOCK
    ncp = seq_len // CMP_STRIDE
    n_sel = min(SLC_TOPK, ns)
    kt_slc = min(SLC_KT, seq_len)
    t_row = s0 + lax.broadcasted_iota(I32, (1, QB), 1)
    t4 = _lanes4(t_row)
    gates = jax.nn.sigmoid(glt_ref[0])
    blk = lax.broadcasted_iota(I32, (ns, QB), 0)
    cur = lax.shift_right_logical(t_row, 6)
    forced = (blk == 0) | (blk == cur) | (blk == cur - 1)
    future = blk * SLC_BLOCK > t_row
    cmp_last = lax.broadcasted_iota(I32, (ncp, 1), 0) * CMP_STRIDE + (CMP_BLOCK - 1)
    cmask = cmp_last <= t4
    zeros_q = jnp.zeros((Dh, HPG * QB), BF16)
    G = NSA_KV_GROUPS
    NL = G * HPG * QB
    qpads, o_cmps = [], []

    for g in range(G):
        qg = jnp.concatenate(
            [qt_ref[0, (g * HPG + h) * Dh:(g * HPG + h + 1) * Dh, :] for h in range(HPG)], axis=1)
        qpads.append(jnp.concatenate([qg, zeros_q] if g == 0 else [zeros_q, qg], axis=0))
        sc = jnp.where(cmask, jnp.dot(kcmp_ref[0, g], qg, preferred_element_type=F32), NEG_INF)
        m = jnp.max(sc, axis=0, keepdims=True)
        p = jnp.where(cmask, jnp.exp2(sc - m), 0.0)
        l = jnp.sum(p, axis=0, keepdims=True)
        p = p * (1.0 / jnp.maximum(l, 1e-30))
        o_cmps.append(jnp.dot(vcmpt_ref[0, g], p.astype(BF16), preferred_element_type=F32))
        psum = p[:, 0:QB] + p[:, QB:2 * QB] + p[:, 2 * QB:3 * QB] + p[:, 3 * QB:4 * QB]
        p_hi, p_lo = _split2(psum)
        imp = (jnp.dot(ovt_ref[...], p_hi, preferred_element_type=F32)
               + jnp.dot(ovt_ref[...], p_lo, preferred_element_type=F32))
        vals = jnp.where(future, NEG_INF, jnp.where(forced, FORCED, imp))
        sel = jnp.zeros((ns, QB), F32)
        for _ in range(n_sel):
            mx = jnp.max(vals, axis=0, keepdims=True)
            idx = jnp.min(jnp.where(vals == mx, blk, ns), axis=0, keepdims=True)
            pick = blk == idx
            sel = jnp.where(pick, 1.0, sel)
            vals = jnp.where(pick, LOWEST, vals)
        selb_ref[g] = jnp.where(sel > 0.5, 0.0, NEG_INF)

    qall = jnp.concatenate(qpads, axis=1)

    def flash_step(kt, vrow0, k0, width, bias_of_group, carry):
        m_i, l_i, acc = carry
        s = jnp.dot(kt, qall, preferred_element_type=F32)
        s = s + jnp.concatenate([_lanes4(bias_of_group(g)) for g in range(G)], axis=1)
        m_new = jnp.maximum(m_i, jnp.max(s, axis=0, keepdims=True))
        alpha = jnp.exp2(m_i - m_new)
        pj = jnp.exp2(s - m_new)
        l_new = alpha * l_i + jnp.sum(pj, axis=0, keepdims=True)
        pj = pj.astype(BF16)
        half = HPG * QB
        pv = jnp.concatenate(
            [jnp.dot(vt_ref[0, vrow0 + g * Dh:vrow0 + (g + 1) * Dh, pl.ds(k0, width)],
                     pj[:, g * half:(g + 1) * half], preferred_element_type=F32) for g in range(G)], axis=1)
        return m_new, l_new, alpha * acc + pv

    def slc_tile(j, carry):
        k0 = pl.multiple_of(j * kt_slc, kt_slc)
        nblk = kt_slc // SLC_BLOCK
        causal = k0 + lax.broadcasted_iota(I32, (kt_slc, 1), 0) <= t_row

        def bias(g):
            blk_bias = jnp.concatenate(
                [jnp.broadcast_to(selb_ref[g, pl.ds(j * nblk + i, 1), :], (SLC_BLOCK, QB)) for i in range(nblk)],
                axis=0)
            return jnp.where(causal, blk_bias, NEG_INF)

        return flash_step(ksw_ref[0, pl.ds(k0, kt_slc), 0:LANE], 0, k0, kt_slc, bias, carry)

    n_all = seq_len // kt_slc
    unroll = math.gcd(SLC_UNROLL, n_all)

    def slc_body(jj, carry):
        for u in range(unroll):
            carry = slc_tile(jj * unroll + u, carry)
        return carry

    init = (jnp.full((1, NL), NEG_INF, F32), jnp.zeros((1, NL), F32), jnp.zeros((Dh, NL), F32))
    n_tiles = (s0 + QB - 1) // kt_slc + 1
    _, l_s, acc_s = lax.fori_loop(0, (n_tiles + unroll - 1) // unroll, slc_body, init)
    o_slc = acc_s * (1.0 / l_s)

    carry = init
    for jj in range(min(WINDOW // QB + 1, seq_len // QB)):
        tile = qi - jj
        k0 = pl.multiple_of(jnp.maximum(tile, 0) * QB, QB)
        kpos = k0 + lax.broadcasted_iota(I32, (QB, 1), 0)
        wbias = jnp.where((kpos <= t_row) & (kpos > t_row - WINDOW) & (tile >= 0), 0.0, NEG_INF)
        carry = flash_step(ksw_ref[0, pl.ds(k0, QB), LANE:2 * LANE], LANE, k0, QB, lambda g, wb=wbias: wb, carry)
    _, l_w, acc_w = carry
    o_win = acc_w * (1.0 / l_w)

    for g in range(G):
        for h in range(HPG):
            c0 = (g * HPG + h) * 3
            cols = slice((g * HPG + h) * QB, (g * HPG + h + 1) * QB)
            o = (gates[c0:c0 + 1, :] * o_cmps[g][:, h * QB:(h + 1) * QB] + gates[c0 + 1:c0 + 2, :] * o_slc[:, cols]
                 + gates[c0 + 2:c0 + 3, :] * o_win[:, cols])
            o_ref[0, (g * HPG + h) * Dh:(g * HPG + h + 1) * Dh, :] = o


def _nsa(qt, ksw, vt, kcmp, vcmpt, overlap_t, glt):
    B, _, S = qt.shape
    ncp = S // CMP_STRIDE
    ns = S // SLC_BLOCK
    G = NSA_KV_GROUPS
    return pl.pallas_call(
        functools.partial(_nsa_kernel, seq_len=S),
        grid=(B, S // Q_BLOCK),
        in_specs=[pl.BlockSpec((1, NSA_WIDTH, Q_BLOCK), lambda b, i: (b, 0, i)),
                  pl.BlockSpec((1, S, 2 * LANE), lambda b, i: (b, 0, 0)),
                  pl.BlockSpec((1, 2 * LANE, S), lambda b, i: (b, 0, 0)),
                  pl.BlockSpec((1, G, ncp, HEAD_DIM), lambda b, i: (b, 0, 0, 0)),
                  pl.BlockSpec((1, G, HEAD_DIM, ncp), lambda b, i: (b, 0, 0, 0)),
                  pl.BlockSpec((ns, ncp), lambda b, i: (0, 0)),
                  pl.BlockSpec((1, LANE, Q_BLOCK), lambda b, i: (b, 0, i))],
        out_specs=pl.BlockSpec((1, NSA_WIDTH, Q_BLOCK), lambda b, i: (b, 0, i)),
        out_shape=jax.ShapeDtypeStruct((B, NSA_WIDTH, S), F32),
        scratch_shapes=[pltpu.VMEM((G, ns, Q_BLOCK), F32)],
        compiler_params=_params("parallel", "arbitrary"),
        name="nsa",
    )(qt, ksw, vt, kcmp, vcmpt, overlap_t, glt)


def _rwkv_prep_kernel(f_ref, p_ref, mu_ref, w0_ref, w2_ref, a0_ref, a2_ref, g2_ref, kk_ref, ka_ref,
                      bd_ref, r_o, lw_o, k_o, v_o, kk_o, a_o, g_o):
    W = RWKV_WIDTH
    i = pl.program_id(1)
    f = f_ref[0]
    tm = f.shape[0]
    prev_last = jnp.where(i > 0, p_ref[0, 7:8, :], 0.0)
    rolled = pltpu.roll(f, 1, 0)
    row = lax.broadcasted_iota(I32, (tm, 1), 0)
    prev = jnp.where(row == 0, prev_last, rolled)
    f = f + (prev - f) * mu_ref[...]
    r, k, v = f[:, 0:W], f[:, W:2 * W], f[:, 2 * W:3 * W]
    wd = f[:, 3 * W:3 * W + LANE]
    ad = f[:, 3 * W + LANE:3 * W + 2 * LANE]
    gd = f[:, 3 * W + 2 * LANE:3 * W + 3 * LANE]
    lw_o[0] = -DECAY_SCALE * jax.nn.sigmoid(w0_ref[...] + _dot3(jnp.tanh(wd), w2_ref[...]))
    a = jax.nn.sigmoid(a0_ref[...] + _dot3(ad, a2_ref[...]))
    g_o[0] = _dot3(jax.nn.sigmoid(gd), g2_ref[...])
    kk = k * kk_ref[...]
    ss = _dot_lhs2(kk * kk, bd_ref[...])
    kk_o[0] = kk / jnp.maximum(jnp.sqrt(ss), 1e-12)
    k_o[0] = k * (1.0 + (a - 1.0) * ka_ref[...])
    r_o[0] = r
    v_o[0] = v
    a_o[0] = a


def _rwkv_prep(rw, mu, w0, w2, a0, a2, g2, k_k, k_a, bd):
    B, S, C = rw.shape
    W = RWKV_WIDTH
    tm = min(256, S)
    row = lambda n: pl.BlockSpec((1, n), lambda b, i: (0, 0))
    mat = pl.BlockSpec((LANE, W), lambda b, i: (0, 0))
    tok = pl.BlockSpec((1, tm, W), lambda b, i: (b, i, 0))
    return pl.pallas_call(
        _rwkv_prep_kernel,
        grid=(B, S // tm),
        in_specs=[pl.BlockSpec((1, tm, C), lambda b, i: (b, i, 0)),
                  pl.BlockSpec((1, 8, C), lambda b, i: (b, jnp.maximum(i * (tm // 8) - 1, 0), 0)),
                  row(C), row(W), mat, row(W), mat, mat, row(W), row(W),
                  pl.BlockSpec((W, W), lambda b, i: (0, 0))],
        out_specs=[tok] * 7,
        out_shape=[jax.ShapeDtypeStruct((B, S, W), F32)] * 7,
        compiler_params=_params("parallel", "arbitrary"),
        name="rwkv_prep",
    )(rw, rw, mu, w0, w2, a0, a2, g2, k_k, k_a, bd)


def _rwkv_scan_kernel(r_ref, lw_ref, k_ref, v_ref, kk_ref, a_ref, g_ref, lnw_ref, lnb_ref, rk_ref,
                      o_ref, st_ref):
    Dh = HEAD_DIM

    @pl.when(pl.program_id(1) == 0)
    def _():
        st_ref[...] = jnp.zeros_like(st_ref)

    r, lw, k, v, kk, a, gate = (ref[0] for ref in (r_ref, lw_ref, k_ref, v_ref, kk_ref, a_ref, g_ref))
    C = r.shape[0]
    ri = lax.broadcasted_iota(I32, (C, C), 0)
    ci = lax.broadcasted_iota(I32, (C, C), 1)
    incl = ri >= ci
    strict = ri > ci
    tri = incl.astype(BF16)
    l_hi = lw.astype(BF16)
    l_r1 = lw - l_hi.astype(F32)
    l_mid = l_r1.astype(BF16)
    l_lo = (l_r1 - l_mid.astype(F32)).astype(BF16)
    L = (jnp.dot(tri, l_hi, preferred_element_type=F32) + jnp.dot(tri, l_mid, preferred_element_type=F32)
         + jnp.dot(tri, l_lo, preferred_element_type=F32))
    e_l = jnp.exp(L)
    e_inv = jnp.exp(-L)
    e_end = jnp.exp(L[C - 1:C, :] - L)
    b = kk * a
    x1 = jnp.concatenate([kk * jnp.exp(L - lw), r * e_l], axis=0).astype(BF16)
    x2t = jnp.concatenate([k * e_inv, b * e_inv], axis=0).T.astype(BF16)
    zt = jnp.concatenate([k * e_end, -(b * e_end)], axis=0).T.astype(BF16)
    gt = jnp.concatenate([e_l, e_l], axis=0).T
    eye = (ri == ci).astype(F32)
    n_double = max(int(math.log2(C)) - 1, 0)

    heads = range(RWKV_HEADS)
    sls = [slice(h * Dh, (h + 1) * Dh) for h in heads]
    h0s = [st_ref[h] for h in heads]
    vhs = [v[:, sl] for sl in sls]
    amats = [jnp.dot(x1[:, sl], x2t[sl, :], preferred_element_type=F32) for sl in sls]
    p0s = [jnp.dot(x1[:, sl], h0.astype(BF16), preferred_element_type=F32) for sl, h0 in zip(sls, h0s)]
    a_kk = [jnp.where(strict, am[0:C, 0:C], 0.0).astype(BF16) for am in amats]
    a_rkb = [jnp.concatenate([jnp.where(incl, am[C:2 * C, 0:C], 0.0), jnp.where(incl, -am[C:2 * C, C:2 * C], 0.0)],
                             axis=1).astype(BF16) for am in amats]
    nmats = [jnp.where(strict, -am[0:C, C:2 * C], 0.0) for am in amats]
    tinvs = [eye + nm for nm in nmats]
    for _ in range(n_double):
        nmats = [_bdot(nm, nm) for nm in nmats]
        tinvs = [ti + _bdot(ti, nm) for ti, nm in zip(tinvs, nmats)]
    akv = [jnp.dot(ak, vh.astype(BF16), preferred_element_type=F32) for ak, vh in zip(a_kk, vhs)]
    us = [_bdot(ti, p0[0:C] + av) for ti, p0, av in zip(tinvs, p0s, akv)]
    vus = [jnp.concatenate([vh, u], axis=0).astype(BF16) for vh, u in zip(vhs, us)]
    ys = [p0[C:2 * C] + jnp.dot(ar, vu, preferred_element_type=F32) for p0, ar, vu in zip(p0s, a_rkb, vus)]
    for h in heads:
        st_ref[h] = gt[sls[h], C - 1:C] * h0s[h] + jnp.dot(zt[sls[h], :], vus[h], preferred_element_type=F32)
    for h in heads:
        sl, y = sls[h], ys[h]
        mean = jnp.mean(y, axis=-1, keepdims=True)
        yc = y - mean
        var = jnp.mean(yc * yc, axis=-1, keepdims=True)
        yn = yc * lax.rsqrt(var + GN_EPS) * lnw_ref[:, sl] + lnb_ref[:, sl]
        bonus = jnp.sum(r[:, sl] * k[:, sl] * rk_ref[:, sl], axis=-1, keepdims=True) * vhs[h]
        o_ref[0, :, sl] = (yn + bonus) * gate[:, sl]


def _rwkv_scan(r, lw, k, v, kk, a, g, ln_w, ln_b, r_k):
    B, S, W = r.shape
    C = min(RWKV_CHUNK, S)
    tok = pl.BlockSpec((1, C, W), lambda b, i: (b, i, 0))
    row = pl.BlockSpec((1, W), lambda b, i: (0, 0))
    return pl.pallas_call(
        _rwkv_scan_kernel,
        grid=(B, S // C),
        in_specs=[tok] * 7 + [row] * 3,
        out_specs=tok,
        out_shape=jax.ShapeDtypeStruct((B, S, W), F32),
        scratch_shapes=[pltpu.VMEM((RWKV_HEADS, HEAD_DIM, HEAD_DIM), F32)],
        compiler_params=_params("parallel", "arbitrary"),
        name="rwkv_scan",
    )(r, lw, k, v, kk, a, g, ln_w, ln_b, r_k)


def _rms(x, g):
    return x * lax.rsqrt(jnp.mean(x * x, axis=-1, keepdims=True) + RMS_EPS) * g


def _outproj_kernel(on_ref, or_ref, x_ref, gn_ref, wt_ref, wb_ref, gpost_ref, gm_ref, gpre_ref,
                    scf_ref, shf_ref, rw_ref, rb_ref, x1_ref, h2_ref, idx_ref, wgt_ref):
    ont = on_ref[0]
    ont = ont * lax.rsqrt(jnp.mean(ont * ont, axis=0, keepdims=True) + RMS_EPS) * gn_ref[...]
    on = ont.T
    mixed = (jnp.dot(on.astype(BF16), wt_ref[...], preferred_element_type=F32)
             + jnp.dot(or_ref[0].astype(BF16), wb_ref[...], preferred_element_type=F32))
    x1 = x_ref[0] + gm_ref[0] * _rms(mixed, gpost_ref[...])
    x1_ref[0] = x1
    h2 = _rms(x1, gpre_ref[...]) * scf_ref[0] + shf_ref[0]
    h2_ref[0] = h2.astype(BF16)
    logits = _dot3(h2, rw_ref[...]) + rb_ref[...]
    lane = lax.broadcasted_iota(I32, logits.shape, 1)
    vals = logits
    top_v, top_i = [], []
    for _ in range(TOP_K):
        mx = jnp.max(vals, axis=-1, keepdims=True)
        ix = jnp.min(jnp.where(vals == mx, lane, LANE), axis=-1, keepdims=True)
        top_v.append(mx)
        top_i.append(ix)
        vals = jnp.where(lane == ix, LOWEST, vals)
    ex = [jnp.exp(tv - top_v[0]) for tv in top_v]
    inv = 1.0 / (ex[0] + ex[1] + ex[2] + ex[3])
    idx = jnp.zeros(logits.shape, I32)
    wgt = jnp.zeros(logits.shape, F32)
    for kk in range(TOP_K):
        idx = jnp.where(lane == kk, top_i[kk], idx)
        wgt = jnp.where(lane == kk, ex[kk] * inv, wgt)
    idx_ref[0] = idx
    wgt_ref[0] = wgt


def _outproj(o_nsa, o_rwkv, x, gn, w_top, w_bot, g_post, g_m, g_pre, sc1_f, sh_f, router_w, router_b):
    B, S, D = x.shape
    tm = min(256, S)
    W = o_rwkv.shape[-1]
    tok = lambda n: pl.BlockSpec((1, tm, n), lambda b, i: (b, i, 0))
    vec = pl.BlockSpec((1, 1, D), lambda b, i: (b, 0, 0))
    row = lambda n: pl.BlockSpec((1, n), lambda b, i: (0, 0))
    return pl.pallas_call(
        _outproj_kernel,
        grid=(B, S // tm),
        in_specs=[pl.BlockSpec((1, W, tm), lambda b, i: (b, 0, i)), tok(W), tok(D),
                  pl.BlockSpec((W, 1), lambda b, i: (0, 0)),
                  pl.BlockSpec((W, D), lambda b, i: (0, 0)), pl.BlockSpec((W, D), lambda b, i: (0, 0)),
                  row(D), vec, row(D), vec, vec,
                  pl.BlockSpec((D, LANE), lambda b, i: (0, 0)), row(LANE)],
        out_specs=[tok(D), tok(D), tok(LANE), tok(LANE)],
        out_shape=[jax.ShapeDtypeStruct((B, S, D), F32), jax.ShapeDtypeStruct((B, S, D), BF16),
                   jax.ShapeDtypeStruct((B, S, LANE), I32), jax.ShapeDtypeStruct((B, S, LANE), F32)],
        compiler_params=_params("parallel", "arbitrary"),
        name="outproj",
    )(o_nsa, o_rwkv, x, gn, w_top, w_bot, g_post, g_m, g_pre, sc1_f, sh_f, router_w, router_b)


def _expert_kernel(be_ref, nu_ref, x_ref, wgu_ref, bgu_ref, wd_ref, bd_ref, o_ref):
    F = wd_ref.shape[1]

    @pl.when(pl.program_id(0) < nu_ref[0])
    def _():
        gu = jnp.dot(x_ref[...], wgu_ref[0], preferred_element_type=F32) + bgu_ref[0]
        gate = jnp.minimum(gu[:, 0:F], SWIGLU_LIMIT)
        up = jnp.clip(gu[:, F:2 * F], -SWIGLU_LIMIT, SWIGLU_LIMIT)
        glu = gate * jax.nn.sigmoid(gate * SWIGLU_ALPHA)
        o_ref[...] = jnp.dot(((up + 1.0) * glu).astype(BF16), wd_ref[0],
                             preferred_element_type=F32) + bd_ref[0]


def _experts(blk_expert, n_used, x_sorted, w_gu, b_gu, w_d, b_d):
    P, D = x_sorted.shape
    E, _, F2 = w_gu.shape
    F = F2 // 2
    nb = P // MOE_BLOCK
    grid_spec = pltpu.PrefetchScalarGridSpec(
        num_scalar_prefetch=2,
        grid=(nb,),
        in_specs=[pl.BlockSpec((MOE_BLOCK, D), lambda i, be, nu: (i, 0)),
                  pl.BlockSpec((1, D, F2), lambda i, be, nu: (be[i], 0, 0)),
                  pl.BlockSpec((1, 1, F2), lambda i, be, nu: (be[i], 0, 0)),
                  pl.BlockSpec((1, F, D), lambda i, be, nu: (be[i], 0, 0)),
                  pl.BlockSpec((1, 1, D), lambda i, be, nu: (be[i], 0, 0))],
        out_specs=pl.BlockSpec((MOE_BLOCK, D), lambda i, be, nu: (i, 0)),
    )
    return pl.pallas_call(
        _expert_kernel,
        grid_spec=grid_spec,
        out_shape=jax.ShapeDtypeStruct((P, D), F32),
        compiler_params=_params("arbitrary"),
        name="experts",
    )(blk_expert, n_used, x_sorted, w_gu, b_gu.reshape(E, 1, F2), w_d, b_d.reshape(E, 1, D))


def _combine_kernel(y_ref, w_ref, x1_ref, gpost_ref, gf_ref, o_ref):
    w = w_ref[0]
    y = (w[:, 0:1] * y_ref[0] + w[:, 1:2] * y_ref[1] + w[:, 2:3] * y_ref[2] + w[:, 3:4] * y_ref[3])
    o_ref[0] = x1_ref[0] + gf_ref[0] * _rms(y, gpost_ref[...])


def _combine(y4, wgt, x1, g_post, g_f):
    B, S, D = x1.shape
    tm = min(256, S)
    nt = S // tm
    return pl.pallas_call(
        _combine_kernel,
        grid=(B, nt),
        in_specs=[pl.BlockSpec((TOP_K, tm, D), lambda b, i: (0, b * nt + i, 0)),
                  pl.BlockSpec((1, tm, LANE), lambda b, i: (b, i, 0)),
                  pl.BlockSpec((1, tm, D), lambda b, i: (b, i, 0)),
                  pl.BlockSpec((1, D), lambda b, i: (0, 0)),
                  pl.BlockSpec((1, 1, D), lambda b, i: (b, 0, 0))],
        out_specs=pl.BlockSpec((1, tm, D), lambda b, i: (b, i, 0)),
        out_shape=jax.ShapeDtypeStruct((B, S, D), F32),
        compiler_params=_params("parallel", "arbitrary"),
        name="combine",
    )(y4, wgt, x1, g_post, g_f)


def _rot_cols(w):
    d, n = w.shape
    w4 = w.reshape(d, n // HEAD_DIM, 2, HEAD_DIM // 2)
    return jnp.concatenate([-w4[:, :, 1:2], w4[:, :, 0:1]], axis=2).reshape(d, n)


def _pad_cols(w, n):
    return jnp.pad(w, ((0, 0), (0, n - w.shape[1])))


def _pad_rows(w, n):
    return jnp.pad(w, ((0, n - w.shape[0]), (0, 0)))


def _layer(x, ada, l, mix_pre_norm, mix_post_norm, ffn_pre_norm, ffn_post_norm, w_in, cmp_k_pe, cmp_k_w1,
           cmp_k_w2, cmp_v_pe, cmp_v_w1, cmp_v_w2, nsa_out_norm, rwkv_mu, rwkv_w0, rwkv_w2, rwkv_a0, rwkv_a2,
           rwkv_g2, rwkv_k_k, rwkv_k_a, rwkv_r_k, rwkv_ln_w, rwkv_ln_b, w_out, router_w, router_b,
           expert_w_gate_up, expert_b_gate_up, expert_w_down, expert_b_down):
    B, S, D = x.shape
    T = B * S
    G, Dh, W = NSA_KV_GROUPS, HEAD_DIM, RWKV_WIDTH
    sh_m, sc_m, g_m, sh_f, sc_f, g_f = [a.reshape(B, 1, D) for a in jnp.split(ada, 6, axis=-1)]

    wi = w_in[l]
    KV = NSA_KV_WIDTH
    o = NSA_WIDTH
    wq, wkc, wvc, wks, wvs, wkw, wvw = (wi[:, 0:o],) + tuple(wi[:, o + j * KV:o + (j + 1) * KV] for j in range(6))
    wgl = wi[:, o + 6 * KV:o + 6 * KV + 3 * NSA_HEADS]
    wr = wi[:, o + 6 * KV + 3 * NSA_HEADS:]
    w_rope = jnp.concatenate([wq, wkc, wks, wkw], axis=1)
    w_rest = _pad_cols(jnp.concatenate([wvc, wvs, wvw, wgl], axis=1), REST_COLS)
    lora = [(3 * W, RWKV_W_LORA), (3 * W + RWKV_W_LORA, RWKV_A_LORA),
            (3 * W + RWKV_W_LORA + RWKV_A_LORA, RWKV_G_LORA)]
    pad_lora = lambda a: jnp.concatenate([_pad_cols(a[..., s:s + n], LANE) for s, n in lora], axis=-1)
    w_rw = jnp.concatenate([wr[:, 0:3 * W], pad_lora(wr)], axis=1)
    w_all = jnp.concatenate([w_rope, _rot_cols(w_rope), w_rest, w_rw], axis=1).astype(BF16)
    mu_row = rwkv_mu[l].reshape(1, -1)
    mu = jnp.concatenate([mu_row[:, 0:3 * W], pad_lora(mu_row)], axis=1)

    half = Dh // 2
    inv_freq = ROPE_THETA ** (-jnp.arange(half, dtype=F32) / half)
    ang = jnp.arange(S, dtype=F32)[:, None] * inv_freq[None, :]
    n_rope_heads = ROPE_COLS // Dh
    col_scale = jnp.where(jnp.arange(ROPE_COLS) < NSA_WIDTH, Dh ** -0.5 * math.log2(math.e), 1.0).astype(F32)
    cos_t = jnp.tile(jnp.cos(ang), (1, 2 * n_rope_heads)) * col_scale
    sin_t = jnp.tile(jnp.sin(ang), (1, 2 * n_rope_heads)) * col_scale

    qt, kc, ksw, vc, vt, glt, rw = _inproj(x, mix_pre_norm[l], 1.0 + sc_m, sh_m, w_all, cos_t, sin_t)

    nb = S // CMP_STRIDE
    regroup = lambda t: t.reshape(B, nb, CMP_STRIDE, G, Dh).transpose(0, 3, 1, 2, 4).reshape(B, G, nb, CMP_STRIDE * Dh)
    pe8 = lambda pe: jnp.broadcast_to(pe.reshape(1, -1), (8, pe.size))
    kcmp = _compress(regroup(kc), cmp_k_w1[l], cmp_k_w2[l], pe8(cmp_k_pe[l]), transposed=False)
    vcmpt = _compress(regroup(vc), cmp_v_w1[l], cmp_v_w2[l].T, pe8(cmp_v_pe[l]), transposed=True)
    ns = S // SLC_BLOCK
    c0 = jnp.arange(nb)[None, :] * CMP_STRIDE
    b0 = jnp.arange(ns)[:, None] * SLC_BLOCK
    ov = jnp.maximum(jnp.minimum(c0 + CMP_BLOCK, b0 + SLC_BLOCK) - jnp.maximum(c0, b0), 0)
    overlap_t = (ov.astype(F32) / CMP_BLOCK).astype(BF16)
    o_nsa_t = _nsa(qt, ksw, vt, kcmp, vcmpt, overlap_t, glt)

    bd = (jnp.arange(W)[:, None] // Dh == jnp.arange(W)[None, :] // Dh).astype(BF16)
    row = lambda a: a.reshape(1, -1)
    r, lw, k, v, kk, a, g = _rwkv_prep(
        rw, mu, row(rwkv_w0[l]), _pad_rows(rwkv_w2[l], LANE), row(rwkv_a0[l]), _pad_rows(rwkv_a2[l], LANE),
        _pad_rows(rwkv_g2[l], LANE), row(rwkv_k_k[l]), row(rwkv_k_a[l]), bd)
    o_rwkv = _rwkv_scan(r, lw, k, v, kk, a, g, row(rwkv_ln_w[l]), row(rwkv_ln_b[l]), row(rwkv_r_k[l]))

    wo = w_out[l].astype(BF16)
    rb = jnp.concatenate([router_b[l], jnp.full((LANE - N_EXPERTS,), NEG_INF, F32)]).reshape(1, LANE)
    x1, h2, idx, wgt = _outproj(o_nsa_t, o_rwkv, x, nsa_out_norm[l].reshape(-1, 1), wo[0:NSA_WIDTH], wo[NSA_WIDTH:],
                                row(mix_post_norm[l]), g_m, row(ffn_pre_norm[l]), 1.0 + sc_f, sh_f,
                                _pad_cols(router_w[l], LANE), rb)

    n_assign = T * TOP_K
    e_flat = idx.reshape(T, LANE)[:, 0:TOP_K].reshape(-1)
    order = jnp.argsort(e_flat)
    e_sorted = e_flat[order]
    counts = jnp.bincount(e_flat, length=N_EXPERTS)
    starts = jnp.cumsum(counts) - counts
    padded = ((counts + MOE_BLOCK - 1) // MOE_BLOCK) * MOE_BLOCK
    pad_ends = jnp.cumsum(padded)
    pad_starts = pad_ends - padded
    dest = (pad_starts[e_sorted] + (jnp.arange(n_assign) - starts[e_sorted])).astype(I32)
    n_blocks = -(-n_assign // MOE_BLOCK) + N_EXPERTS
    P = n_blocks * MOE_BLOCK
    slot_tok = jnp.full((P,), T, I32).at[dest].set((order // TOP_K).astype(I32))
    pos = jnp.zeros((n_assign,), I32).at[order].set(dest)
    blk_expert = jnp.minimum(
        jnp.searchsorted(pad_ends, jnp.arange(n_blocks) * MOE_BLOCK, side='right'), N_EXPERTS - 1).astype(I32)
    n_used = (pad_ends[-1] // MOE_BLOCK).astype(I32).reshape(1)

    h2_pad = jnp.concatenate([h2.reshape(T, D), jnp.zeros((1, D), BF16)], axis=0)
    x_sorted = h2_pad[slot_tok]
    y_sorted = _experts(blk_expert, n_used, x_sorted, expert_w_gate_up[l].astype(BF16), expert_b_gate_up[l],
                        expert_w_down[l].astype(BF16), expert_b_down[l])
    y4 = y_sorted[pos.reshape(T, TOP_K).T]
    return _combine(y4, wgt, x1, row(ffn_post_norm[l]), g_f)


def kernel(x, c, ada_w, ada_b, mix_pre_norm, mix_post_norm, ffn_pre_norm, ffn_post_norm, w_in, cmp_k_pe, cmp_k_w1, cmp_k_w2, cmp_v_pe, cmp_v_w1, cmp_v_w2, nsa_out_norm, rwkv_mu, rwkv_w0, rwkv_w2, rwkv_a0, rwkv_a2, rwkv_g2, rwkv_k_k, rwkv_k_a, rwkv_r_k, rwkv_ln_w, rwkv_ln_b, w_out, router_w, router_b, expert_w_gate_up, expert_b_gate_up, expert_w_down, expert_b_down):
    for l in range(ada_w.shape[0]):
        ada = _ada(c, ada_w[l], ada_b[l])
        x = _layer(x, ada, l, mix_pre_norm, mix_post_norm, ffn_pre_norm, ffn_post_norm, w_in, cmp_k_pe,
                   cmp_k_w1, cmp_k_w2, cmp_v_pe, cmp_v_w1, cmp_v_w2, nsa_out_norm, rwkv_mu, rwkv_w0, rwkv_w2,
                   rwkv_a0, rwkv_a2, rwkv_g2, rwkv_k_k, rwkv_k_a, rwkv_r_k, rwkv_ln_w, rwkv_ln_b, w_out,
                   router_w, router_b, expert_w_gate_up, expert_b_gate_up, expert_w_down, expert_b_down)
    return x
```

```python
import functools
import math

import jax
import jax.numpy as jnp
from jax import lax
from jax.experimental import pallas as pl
from jax.experimental.pallas import tpu as pltpu

F32 = jnp.float32
BF16 = jnp.bfloat16
I32 = jnp.int32

HEAD_DIM = 64
NSA_HEADS = 8
NSA_KV_GROUPS = 2
NSA_HPG = NSA_HEADS // NSA_KV_GROUPS
NSA_WIDTH = NSA_HEADS * HEAD_DIM
NSA_KV_WIDTH = NSA_KV_GROUPS * HEAD_DIM
CMP_BLOCK = 32
CMP_STRIDE = 16
SLC_BLOCK = 64
SLC_TOPK = 16
WINDOW = 512
Q_BLOCK = 128
RWKV_HEADS = 8
RWKV_WIDTH = RWKV_HEADS * HEAD_DIM
RWKV_W_LORA = 32
RWKV_A_LORA = 32
RWKV_G_LORA = 96
DECAY_SCALE = math.exp(-0.5)
GN_EPS = 64e-5
N_EXPERTS = 32
TOP_K = 4
SWIGLU_LIMIT = 7.0
SWIGLU_ALPHA = 1.702
ROPE_THETA = 10000.0
RMS_EPS = 1e-6
NEG_INF = -1e30
FORCED = 1e9
LOWEST = -3e38

LANE = 128
ROPE_COLS = NSA_WIDTH + 3 * NSA_KV_WIDTH
REST_COLS = 4 * LANE
RW_COLS = 3 * RWKV_WIDTH + 3 * LANE
SLC_KT = 512
SLC_UNROLL = 1
FLASH_LANES = 1024
FLASH_DEPTH = 1
ACC_ROWS = HEAD_DIM + 16
RWKV_CHUNK = 64
MOE_BLOCK = 512
VMEM_LIMIT = 56 * 1024 * 1024


def _bdot(a, b):
    return jnp.dot(a.astype(BF16), b.astype(BF16), preferred_element_type=F32)


def _split2(a):
    hi = a.astype(BF16)
    lo = (a - hi.astype(F32)).astype(BF16)
    return hi, lo


def _dot_lhs2(a, b_bf16):
    hi, lo = _split2(a)
    return (jnp.dot(hi, b_bf16, preferred_element_type=F32)
            + jnp.dot(lo, b_bf16, preferred_element_type=F32))


def _dot3(a, b):
    ah, al = _split2(a)
    bh, bl = _split2(b)
    return (jnp.dot(ah, bh, preferred_element_type=F32)
            + jnp.dot(al, bh, preferred_element_type=F32)
            + jnp.dot(ah, bl, preferred_element_type=F32))


def _params(*sem, flags=None):
    return pltpu.CompilerParams(dimension_semantics=sem, vmem_limit_bytes=VMEM_LIMIT, flags=flags)


def _ada_kernel(c_ref, w_ref, b_ref, o_ref):
    c = c_ref[...]
    act = c * jax.nn.sigmoid(c)
    o_ref[...] = _dot3(act, w_ref[...]) + b_ref[...]


def _ada(c, w, b):
    B, D = c.shape
    N = w.shape[1]
    tn = 1536
    return pl.pallas_call(
        _ada_kernel,
        grid=(N // tn,),
        in_specs=[pl.BlockSpec((B, D), lambda j: (0, 0)),
                  pl.BlockSpec((D, tn), lambda j: (0, j)),
                  pl.BlockSpec((1, tn), lambda j: (0, j))],
        out_specs=pl.BlockSpec((B, tn), lambda j: (0, j)),
        out_shape=jax.ShapeDtypeStruct((B, N), F32),
        compiler_params=_params("arbitrary"),
        name="ada",
    )(c, w, b.reshape(1, N))


def _inproj_kernel(x_ref, g_ref, sc_ref, sh_ref, w_ref, cos_ref, sin_ref,
                   q_ref, kc_ref, ksw_ref, vc_ref, vsw_ref, gl_ref, rw_ref):
    x = x_ref[0]
    ms = jnp.mean(x * x, axis=-1, keepdims=True)
    h = x * lax.rsqrt(ms + RMS_EPS) * g_ref[...]
    h = h * sc_ref[0] + sh_ref[0]
    hb = h.astype(BF16)
    R = ROPE_COLS
    main = jnp.dot(hb, w_ref[:, 0:R], preferred_element_type=F32)
    rot = jnp.dot(hb, w_ref[:, R:2 * R], preferred_element_type=F32)
    roped = main * cos_ref[...] + rot * sin_ref[...]
    q_ref[0] = roped[:, 0:NSA_WIDTH].T.astype(BF16)
    kc_ref[0] = roped[:, NSA_WIDTH:NSA_WIDTH + LANE]
    ksw_ref[0] = roped[:, NSA_WIDTH + LANE:R].astype(BF16)
    rest = jnp.dot(hb, w_ref[:, 2 * R:2 * R + REST_COLS], preferred_element_type=F32)
    vc_ref[0] = rest[:, 0:LANE]
    vsw_ref[0] = rest[:, LANE:3 * LANE].T.astype(BF16)
    gl_ref[0] = rest[:, 3 * LANE:4 * LANE].T
    rw_ref[0] = jnp.dot(hb, w_ref[:, 2 * R + REST_COLS:], preferred_element_type=F32)


def _inproj(x, gain, sc1, sh, w_all, cos_t, sin_t):
    B, S, D = x.shape
    tm = min(256, S)
    NW = w_all.shape[1]
    R = ROPE_COLS
    tok = lambda n: pl.BlockSpec((1, tm, n), lambda b, i: (b, i, 0))
    tok_t = lambda n: pl.BlockSpec((1, n, tm), lambda b, i: (b, 0, i))
    vec = pl.BlockSpec((1, 1, D), lambda b, i: (b, 0, 0))
    outs = [(NSA_WIDTH, BF16, True), (LANE, F32, False), (2 * LANE, BF16, False), (LANE, F32, False),
            (2 * LANE, BF16, True), (LANE, F32, True), (RW_COLS, F32, False)]
    return pl.pallas_call(
        _inproj_kernel,
        grid=(B, S // tm),
        in_specs=[tok(D), pl.BlockSpec((1, D), lambda b, i: (0, 0)), vec, vec,
                  pl.BlockSpec((D, NW), lambda b, i: (0, 0)),
                  pl.BlockSpec((tm, R), lambda b, i: (i, 0)),
                  pl.BlockSpec((tm, R), lambda b, i: (i, 0))],
        out_specs=[tok_t(n) if tr else tok(n) for n, _, tr in outs],
        out_shape=[jax.ShapeDtypeStruct((B, n, S) if tr else (B, S, n), dt) for n, dt, tr in outs],
        compiler_params=_params("parallel", "arbitrary"),
        name="inproj",
    )(x, gain.reshape(1, D), sc1, sh, w_all, cos_t, sin_t)


def _gelu_tanh(x):
    return 0.5 * x * (1.0 + jnp.tanh(math.sqrt(2.0 / math.pi) * (x + 0.044715 * (x * x * x))))


def _compress_kernel(r_ref, w1_ref, w2_ref, pe_ref, o_ref, *, transposed):
    R = r_ref[0, 0].astype(BF16)
    nb, half = R.shape
    w1 = w1_ref[...].astype(BF16)
    top = jnp.dot(R, w1[0:half], preferred_element_type=F32)
    bot = jnp.dot(R, w1[half:2 * half], preferred_element_type=F32)
    pe_term = jnp.dot(pe_ref[...].astype(BF16), w1, preferred_element_type=F32)[0:1]
    bot_next = pltpu.roll(bot, nb - 1, 0)
    hid = _gelu_tanh(top + bot_next + pe_term)
    if transposed:
        out = jnp.dot(w2_ref[...].astype(BF16), hid.T.astype(BF16), preferred_element_type=F32)
    else:
        out = jnp.dot(hid.astype(BF16), w2_ref[...].astype(BF16), preferred_element_type=F32)
    o_ref[0, 0] = out.astype(BF16)


def _compress(rr, w1, w2, pe, transposed):
    B, G, nb, half = rr.shape
    hid = w1.shape[-1]
    oshape = (HEAD_DIM, nb) if transposed else (nb, HEAD_DIM)
    return pl.pallas_call(
        functools.partial(_compress_kernel, transposed=transposed),
        grid=(B, G),
        in_specs=[pl.BlockSpec((1, 1, nb, half), lambda b, g: (b, g, 0, 0)),
                  pl.BlockSpec((2 * half, hid), lambda b, g: (0, 0)),
                  pl.BlockSpec(w2.shape, lambda b, g: (0, 0)),
                  pl.BlockSpec((8, 2 * half), lambda b, g: (0, 0))],
        out_specs=pl.BlockSpec((1, 1) + oshape, lambda b, g: (b, g, 0, 0)),
        out_shape=jax.ShapeDtypeStruct((B, G) + oshape, BF16),
        compiler_params=_params("arbitrary", "arbitrary"),
        name="compress_v" if transposed else "compress_k",
    )(rr, w1, w2, pe)


def _lanes4(a):
    return jnp.concatenate([a, a, a, a], axis=1)


def _nsa_kernel(qt_ref, ksw_ref, vt_ref, kcmp_ref, vcmpt_ref, ovt_ref, glt_ref, o_ref, selb_ref, *, seq_len):
    QB, Dh, HPG = Q_BLOCK, HEAD_DIM, NSA_HPG
    qi = pl.program_id(1)
    s0 = qi * QB
    ns = seq_len // SLC_BLOCK
    ncp = seq_len // CMP_STRIDE
    n_sel = min(SLC_TOPK, ns)
    kt_slc = min(SLC_KT, seq_len)
    t_row = s0 + lax.broadcasted_iota(I32, (1, QB), 1)
    t4 = _lanes4(t_row)
    gates = jax.nn.sigmoid(glt_ref[0])
    blk = lax.broadcasted_iota(I32, (ns, QB), 0)
    cur = lax.shift_right_logical(t_row, 6)
    forced = (blk == 0) | (blk == cur) | (blk == cur - 1)
    future = blk * SLC_BLOCK > t_row
    cmp_last = lax.broadcasted_iota(I32, (ncp, 1), 0) * CMP_STRIDE + (CMP_BLOCK - 1)
    cmask = cmp_last <= t4
    zeros_q = jnp.zeros((Dh, HPG * QB), BF16)
    G = NSA_KV_GROUPS
    NL = G * HPG * QB
    qpads, o_cmps = [], []

    for g in range(G):
        qg = jnp.concatenate(
            [qt_ref[0, (g * HPG + h) * Dh:(g * HPG + h + 1) * Dh, :] for h in range(HPG)], axis=1)
        qpads.append(jnp.concatenate([qg, zeros_q] if g == 0 else [zeros_q, qg], axis=0))
        sc = jnp.where(cmask, jnp.dot(kcmp_ref[0, g], qg, preferred_element_type=F32), NEG_INF)
        m = jnp.max(sc, axis=0, keepdims=True)
        p = jnp.where(cmask, jnp.exp2(sc - m), 0.0)
        l = jnp.sum(p, axis=0, keepdims=True)
        p = p * (1.0 / jnp.maximum(l, 1e-30))
        o_cmps.append(jnp.dot(vcmpt_ref[0, g], p.astype(BF16), preferred_element_type=F32))
        psum = p[:, 0:QB] + p[:, QB:2 * QB] + p[:, 2 * QB:3 * QB] + p[:, 3 * QB:4 * QB]
        p_hi, p_lo = _split2(psum)
        imp = (jnp.dot(ovt_ref[...], p_hi, preferred_element_type=F32)
               + jnp.dot(ovt_ref[...], p_lo, preferred_element_type=F32))
        vals = jnp.where(future, NEG_INF, jnp.where(forced, FORCED, imp))
        sel = jnp.zeros((ns, QB), F32)
        for _ in range(n_sel):
            mx = jnp.max(vals, axis=0, keepdims=True)
            idx = jnp.min(jnp.where(vals == mx, blk, ns), axis=0, keepdims=True)
            pick = blk == idx
            sel = jnp.where(pick, 1.0, sel)
            vals = jnp.where(pick, LOWEST, vals)
        selb_ref[g] = jnp.where(sel > 0.5, 0.0, NEG_INF)

    q_slabs = [jnp.concatenate(qpads, axis=1)[:, sb * FLASH_LANES:(sb + 1) * FLASH_LANES]
               for sb in range(NL // FLASH_LANES)]
    n_slab = len(q_slabs)

    def load_tile(k_lane0, vrow0, k0, width, bias_of_group):
        ones = jnp.ones((ACC_ROWS - Dh, width), BF16)
        kt = ksw_ref[0, pl.ds(k0, width), k_lane0:k_lane0 + LANE]
        vaug = [jnp.concatenate([vt_ref[0, vrow0 + g * Dh:vrow0 + (g + 1) * Dh, pl.ds(k0, width)], ones], axis=0)
                for g in range(G)]
        bias = [bias_of_group(g) for g in range(G)]
        return kt, vaug, bias

    group_lanes = HPG * QB
    groups_per_slab = max(FLASH_LANES // group_lanes, 1)
    lanes_per_part = FLASH_LANES // groups_per_slab

    def flash_tiles(tiles, carry):
        m, acc = list(carry[0]), list(carry[1])
        steps = [(ti, sb) for ti in range(len(tiles)) for sb in range(n_slab)]
        scores = {}
        for i in range(len(steps) + FLASH_DEPTH):
            if i < len(steps):
                ti, sb = steps[i]
                scores[i] = jnp.dot(tiles[ti][0], q_slabs[sb], preferred_element_type=F32)
            j = i - FLASH_DEPTH
            if j >= 0:
                ti, sb = steps[j]
                _, vaug, bias = tiles[ti]
                g0 = sb * FLASH_LANES // group_lanes
                slab_bias = jnp.concatenate(
                    [bias[g0 + p] for p in range(groups_per_slab) for _ in range(lanes_per_part // QB)], axis=1)
                s = scores.pop(j) + slab_bias
                m_new = jnp.maximum(m[sb], jnp.max(s, axis=0, keepdims=True))
                alpha = jnp.exp2(m[sb] - m_new)
                pj = jnp.exp2(s - m_new).astype(BF16)
                pv = [jnp.dot(vaug[g0 + p], pj[:, p * lanes_per_part:(p + 1) * lanes_per_part],
                              preferred_element_type=F32) for p in range(groups_per_slab)]
                acc[sb] = alpha * acc[sb] + (pv[0] if len(pv) == 1 else jnp.concatenate(pv, axis=1))
                m[sb] = m_new
        return tuple(m), tuple(acc)

    def slc_tile(j):
        k0 = pl.multiple_of(j * kt_slc, kt_slc)
        nblk = kt_slc // SLC_BLOCK
        causal = k0 + lax.broadcasted_iota(I32, (kt_slc, 1), 0) <= t_row

        def bias(g):
            blk_bias = jnp.concatenate(
                [jnp.broadcast_to(selb_ref[g, pl.ds(j * nblk + i, 1), :], (SLC_BLOCK, QB)) for i in range(nblk)],
                axis=0)
            return jnp.where(causal, blk_bias, NEG_INF)

        return load_tile(0, 0, k0, kt_slc, bias)

    n_all = seq_len // kt_slc
    unroll = math.gcd(SLC_UNROLL, n_all)

    def slc_body(jj, carry):
        return flash_tiles([slc_tile(jj * unroll + u) for u in range(unroll)], carry)

    init = (tuple(jnp.full((1, FLASH_LANES), NEG_INF, F32) for _ in range(n_slab)),
            tuple(jnp.zeros((ACC_ROWS, FLASH_LANES), F32) for _ in range(n_slab)))
    n_tiles = (s0 + QB - 1) // kt_slc + 1
    _, acc_s = lax.fori_loop(0, (n_tiles + unroll - 1) // unroll, slc_body, init)
    acc_s = jnp.concatenate(acc_s, axis=1)
    o_slc = acc_s[0:Dh] * (1.0 / acc_s[Dh:Dh + 1])

    win_tiles = []
    for jj in range(min(WINDOW // QB + 1, seq_len // QB)):
        tile = qi - jj
        k0 = pl.multiple_of(jnp.maximum(tile, 0) * QB, QB)
        kpos = k0 + lax.broadcasted_iota(I32, (QB, 1), 0)
        wbias = jnp.where((kpos <= t_row) & (kpos > t_row - WINDOW) & (tile >= 0), 0.0, NEG_INF)
        win_tiles.append(load_tile(LANE, LANE, k0, QB, lambda g, wb=wbias: wb))
    _, acc_w = flash_tiles(win_tiles, init)
    acc_w = jnp.concatenate(acc_w, axis=1)
    o_win = acc_w[0:Dh] * (1.0 / acc_w[Dh:Dh + 1])

    for g in range(G):
        for h in range(HPG):
            c0 = (g * HPG + h) * 3
            cols = slice((g * HPG + h) * QB, (g * HPG + h + 1) * QB)
            o = (gates[c0:c0 + 1, :] * o_cmps[g][:, h * QB:(h + 1) * QB] + gates[c0 + 1:c0 + 2, :] * o_slc[:, cols]
                 + gates[c0 + 2:c0 + 3, :] * o_win[:, cols])
            o_ref[0, (g * HPG + h) * Dh:(g * HPG + h + 1) * Dh, :] = o


def _nsa(qt, ksw, vt, kcmp, vcmpt, overlap_t, glt):
    B, _, S = qt.shape
    ncp = S // CMP_STRIDE
    ns = S // SLC_BLOCK
    G = NSA_KV_GROUPS
    return pl.pallas_call(
        functools.partial(_nsa_kernel, seq_len=S),
        grid=(B, S // Q_BLOCK),
        in_specs=[pl.BlockSpec((1, NSA_WIDTH, Q_BLOCK), lambda b, i: (b, 0, i)),
                  pl.BlockSpec((1, S, 2 * LANE), lambda b, i: (b, 0, 0)),
                  pl.BlockSpec((1, 2 * LANE, S), lambda b, i: (b, 0, 0)),
                  pl.BlockSpec((1, G, ncp, HEAD_DIM), lambda b, i: (b, 0, 0, 0)),
                  pl.BlockSpec((1, G, HEAD_DIM, ncp), lambda b, i: (b, 0, 0, 0)),
                  pl.BlockSpec((ns, ncp), lambda b, i: (0, 0)),
                  pl.BlockSpec((1, LANE, Q_BLOCK), lambda b, i: (b, 0, i))],
        out_specs=pl.BlockSpec((1, NSA_WIDTH, Q_BLOCK), lambda b, i: (b, 0, i)),
        out_shape=jax.ShapeDtypeStruct((B, NSA_WIDTH, S), F32),
        scratch_shapes=[pltpu.VMEM((G, ns, Q_BLOCK), F32)],
        compiler_params=_params("parallel", "arbitrary"),
        name="nsa",
    )(qt, ksw, vt, kcmp, vcmpt, overlap_t, glt)


def _rwkv_prep_kernel(f_ref, p_ref, mu_ref, w0_ref, w2_ref, a0_ref, a2_ref, g2_ref, kk_ref, ka_ref,
                      bd_ref, r_o, lw_o, k_o, v_o, kk_o, a_o, g_o):
    W = RWKV_WIDTH
    i = pl.program_id(1)
    f = f_ref[0]
    tm = f.shape[0]
    prev_last = jnp.where(i > 0, p_ref[0, 7:8, :], 0.0)
    rolled = pltpu.roll(f, 1, 0)
    row = lax.broadcasted_iota(I32, (tm, 1), 0)
    prev = jnp.where(row == 0, prev_last, rolled)
    f = f + (prev - f) * mu_ref[...]
    r, k, v = f[:, 0:W], f[:, W:2 * W], f[:, 2 * W:3 * W]
    wd = f[:, 3 * W:3 * W + LANE]
    ad = f[:, 3 * W + LANE:3 * W + 2 * LANE]
    gd = f[:, 3 * W + 2 * LANE:3 * W + 3 * LANE]
    lw_o[0] = -DECAY_SCALE * jax.nn.sigmoid(w0_ref[...] + _dot3(jnp.tanh(wd), w2_ref[...]))
    a = jax.nn.sigmoid(a0_ref[...] + _dot3(ad, a2_ref[...]))
    g_o[0] = _dot3(jax.nn.sigmoid(gd), g2_ref[...])
    kk = k * kk_ref[...]
    ss = _dot_lhs2(kk * kk, bd_ref[...])
    kk_o[0] = kk / jnp.maximum(jnp.sqrt(ss), 1e-12)
    k_o[0] = k * (1.0 + (a - 1.0) * ka_ref[...])
    r_o[0] = r
    v_o[0] = v
    a_o[0] = a


def _rwkv_prep(rw, mu, w0, w2, a0, a2, g2, k_k, k_a, bd):
    B, S, C = rw.shape
    W = RWKV_WIDTH
    tm = min(256, S)
    row = lambda n: pl.BlockSpec((1, n), lambda b, i: (0, 0))
    mat = pl.BlockSpec((LANE, W), lambda b, i: (0, 0))
    tok = pl.BlockSpec((1, tm, W), lambda b, i: (b, i, 0))
    return pl.pallas_call(
        _rwkv_prep_kernel,
        grid=(B, S // tm),
        in_specs=[pl.BlockSpec((1, tm, C), lambda b, i: (b, i, 0)),
                  pl.BlockSpec((1, 8, C), lambda b, i: (b, jnp.maximum(i * (tm // 8) - 1, 0), 0)),
                  row(C), row(W), mat, row(W), mat, mat, row(W), row(W),
                  pl.BlockSpec((W, W), lambda b, i: (0, 0))],
        out_specs=[tok] * 7,
        out_shape=[jax.ShapeDtypeStruct((B, S, W), F32)] * 7,
        compiler_params=_params("parallel", "arbitrary"),
        name="rwkv_prep",
    )(rw, rw, mu, w0, w2, a0, a2, g2, k_k, k_a, bd)


def _rwkv_scan_kernel(r_ref, lw_ref, k_ref, v_ref, kk_ref, a_ref, g_ref, lnw_ref, lnb_ref, rk_ref,
                      o_ref, st_ref):
    Dh = HEAD_DIM

    @pl.when(pl.program_id(1) == 0)
    def _():
        st_ref[...] = jnp.zeros_like(st_ref)

    r, lw, k, v, kk, a, gate = (ref[0] for ref in (r_ref, lw_ref, k_ref, v_ref, kk_ref, a_ref, g_ref))
    C = r.shape[0]
    ri = lax.broadcasted_iota(I32, (C, C), 0)
    ci = lax.broadcasted_iota(I32, (C, C), 1)
    incl = ri >= ci
    strict = ri > ci
    tri = incl.astype(BF16)
    l_hi = lw.astype(BF16)
    l_r1 = lw - l_hi.astype(F32)
    l_mid = l_r1.astype(BF16)
    l_lo = (l_r1 - l_mid.astype(F32)).astype(BF16)
    L = (jnp.dot(tri, l_hi, preferred_element_type=F32) + jnp.dot(tri, l_mid, preferred_element_type=F32)
         + jnp.dot(tri, l_lo, preferred_element_type=F32))
    e_l = jnp.exp(L)
    e_inv = jnp.exp(-L)
    e_end = jnp.exp(L[C - 1:C, :] - L)
    b = kk * a
    x1 = jnp.concatenate([kk * jnp.exp(L - lw), r * e_l], axis=0).astype(BF16)
    x2t = jnp.concatenate([k * e_inv, b * e_inv], axis=0).T.astype(BF16)
    zt = jnp.concatenate([k * e_end, -(b * e_end)], axis=0).T.astype(BF16)
    gt = jnp.concatenate([e_l, e_l], axis=0).T
    eye = (ri == ci).astype(F32)
    n_double = max(int(math.log2(C)) - 1, 0)

    heads = range(RWKV_HEADS)
    sls = [slice(h * Dh, (h + 1) * Dh) for h in heads]
    h0s = [st_ref[h] for h in heads]
    vhs = [v[:, sl] for sl in sls]
    amats = [jnp.dot(x1[:, sl], x2t[sl, :], preferred_element_type=F32) for sl in sls]
    p0s = [jnp.dot(x1[:, sl], h0.astype(BF16), preferred_element_type=F32) for sl, h0 in zip(sls, h0s)]
    a_kk = [jnp.where(strict, am[0:C, 0:C], 0.0).astype(BF16) for am in amats]
    a_rkb = [jnp.concatenate([jnp.where(incl, am[C:2 * C, 0:C], 0.0), jnp.where(incl, -am[C:2 * C, C:2 * C], 0.0)],
                             axis=1).astype(BF16) for am in amats]
    nmats = [jnp.where(strict, -am[0:C, C:2 * C], 0.0) for am in amats]
    tinvs = [eye + nm for nm in nmats]
    for _ in range(n_double):
        nmats = [_bdot(nm, nm) for nm in nmats]
        tinvs = [ti + _bdot(ti, nm) for ti, nm in zip(tinvs, nmats)]
    akv = [jnp.dot(ak, vh.astype(BF16), preferred_element_type=F32) for ak, vh in zip(a_kk, vhs)]
    us = [_bdot(ti, p0[0:C] + av) for ti, p0, av in zip(tinvs, p0s, akv)]
    vus = [jnp.concatenate([vh, u], axis=0).astype(BF16) for vh, u in zip(vhs, us)]
    ys = [p0[C:2 * C] + jnp.dot(ar, vu, preferred_element_type=F32) for p0, ar, vu in zip(p0s, a_rkb, vus)]
    for h in heads:
        st_ref[h] = gt[sls[h], C - 1:C] * h0s[h] + jnp.dot(zt[sls[h], :], vus[h], preferred_element_type=F32)
    for h in heads:
        sl, y = sls[h], ys[h]
        mean = jnp.mean(y, axis=-1, keepdims=True)
        yc = y - mean
        var = jnp.mean(yc * yc, axis=-1, keepdims=True)
        yn = yc * lax.rsqrt(var + GN_EPS) * lnw_ref[:, sl] + lnb_ref[:, sl]
        bonus = jnp.sum(r[:, sl] * k[:, sl] * rk_ref[:, sl], axis=-1, keepdims=True) * vhs[h]
        o_ref[0, :, sl] = (yn + bonus) * gate[:, sl]


def _rwkv_scan(r, lw, k, v, kk, a, g, ln_w, ln_b, r_k):
    B, S, W = r.shape
    C = min(RWKV_CHUNK, S)
    tok = pl.BlockSpec((1, C, W), lambda b, i: (b, i, 0))
    row = pl.BlockSpec((1, W), lambda b, i: (0, 0))
    return pl.pallas_call(
        _rwkv_scan_kernel,
        grid=(B, S // C),
        in_specs=[tok] * 7 + [row] * 3,
        out_specs=tok,
        out_shape=jax.ShapeDtypeStruct((B, S, W), F32),
        scratch_shapes=[pltpu.VMEM((RWKV_HEADS, HEAD_DIM, HEAD_DIM), F32)],
        compiler_params=_params("parallel", "arbitrary"),
        name="rwkv_scan",
    )(r, lw, k, v, kk, a, g, ln_w, ln_b, r_k)


def _rms(x, g):
    return x * lax.rsqrt(jnp.mean(x * x, axis=-1, keepdims=True) + RMS_EPS) * g


def _outproj_kernel(on_ref, or_ref, x_ref, gn_ref, wt_ref, wb_ref, gpost_ref, gm_ref, gpre_ref,
                    scf_ref, shf_ref, rw_ref, rb_ref, x1_ref, h2_ref, idx_ref, wgt_ref):
    ont = on_ref[0]
    ont = ont * lax.rsqrt(jnp.mean(ont * ont, axis=0, keepdims=True) + RMS_EPS) * gn_ref[...]
    on = ont.T
    mixed = (jnp.dot(on.astype(BF16), wt_ref[...], preferred_element_type=F32)
             + jnp.dot(or_ref[0].astype(BF16), wb_ref[...], preferred_element_type=F32))
    x1 = x_ref[0] + gm_ref[0] * _rms(mixed, gpost_ref[...])
    x1_ref[0] = x1
    h2 = _rms(x1, gpre_ref[...]) * scf_ref[0] + shf_ref[0]
    h2_ref[0] = h2.astype(BF16)
    logits = _dot3(h2, rw_ref[...]) + rb_ref[...]
    lane = lax.broadcasted_iota(I32, logits.shape, 1)
    vals = logits
    top_v, top_i = [], []
    for _ in range(TOP_K):
        mx = jnp.max(vals, axis=-1, keepdims=True)
        ix = jnp.min(jnp.where(vals == mx, lane, LANE), axis=-1, keepdims=True)
        top_v.append(mx)
        top_i.append(ix)
        vals = jnp.where(lane == ix, LOWEST, vals)
    ex = [jnp.exp(tv - top_v[0]) for tv in top_v]
    inv = 1.0 / (ex[0] + ex[1] + ex[2] + ex[3])
    idx = jnp.zeros(logits.shape, I32)
    wgt = jnp.zeros(logits.shape, F32)
    for kk in range(TOP_K):
        idx = jnp.where(lane == kk, top_i[kk], idx)
        wgt = jnp.where(lane == kk, ex[kk] * inv, wgt)
    idx_ref[0] = idx
    wgt_ref[0] = wgt


def _outproj(o_nsa, o_rwkv, x, gn, w_top, w_bot, g_post, g_m, g_pre, sc1_f, sh_f, router_w, router_b):
    B, S, D = x.shape
    tm = min(256, S)
    W = o_rwkv.shape[-1]
    tok = lambda n: pl.BlockSpec((1, tm, n), lambda b, i: (b, i, 0))
    vec = pl.BlockSpec((1, 1, D), lambda b, i: (b, 0, 0))
    row = lambda n: pl.BlockSpec((1, n), lambda b, i: (0, 0))
    return pl.pallas_call(
        _outproj_kernel,
        grid=(B, S // tm),
        in_specs=[pl.BlockSpec((1, W, tm), lambda b, i: (b, 0, i)), tok(W), tok(D),
                  pl.BlockSpec((W, 1), lambda b, i: (0, 0)),
                  pl.BlockSpec((W, D), lambda b, i: (0, 0)), pl.BlockSpec((W, D), lambda b, i: (0, 0)),
                  row(D), vec, row(D), vec, vec,
                  pl.BlockSpec((D, LANE), lambda b, i: (0, 0)), row(LANE)],
        out_specs=[tok(D), tok(D), tok(LANE), tok(LANE)],
        out_shape=[jax.ShapeDtypeStruct((B, S, D), F32), jax.ShapeDtypeStruct((B, S, D), BF16),
                   jax.ShapeDtypeStruct((B, S, LANE), I32), jax.ShapeDtypeStruct((B, S, LANE), F32)],
        compiler_params=_params("parallel", "arbitrary"),
        name="outproj",
    )(o_nsa, o_rwkv, x, gn, w_top, w_bot, g_post, g_m, g_pre, sc1_f, sh_f, router_w, router_b)


def _expert_kernel(be_ref, nu_ref, x_ref, wgu_ref, bgu_ref, wd_ref, bd_ref, o_ref, wgu_bf, wd_bf):
    F = wd_ref.shape[1]
    i = pl.program_id(0)
    active = i < nu_ref[0]
    new_expert = (i == 0) | (be_ref[i] != be_ref[jnp.maximum(i - 1, 0)])

    @pl.when(active & new_expert)
    def _():
        wgu_bf[...] = wgu_ref[0].astype(BF16)
        wd_bf[...] = wd_ref[0].astype(BF16)

    @pl.when(active)
    def _():
        gu = jnp.dot(x_ref[...], wgu_bf[...], preferred_element_type=F32) + bgu_ref[0]
        gate = jnp.minimum(gu[:, 0:F], SWIGLU_LIMIT)
        up = jnp.clip(gu[:, F:2 * F], -SWIGLU_LIMIT, SWIGLU_LIMIT)
        glu = gate * jax.nn.sigmoid(gate * SWIGLU_ALPHA)
        out = jnp.dot(((up + 1.0) * glu).astype(BF16), wd_bf[...], preferred_element_type=F32) + bd_ref[0]
        o_ref[...] = out.astype(o_ref.dtype)


def _experts(blk_expert, n_used, x_sorted, w_gu, b_gu, w_d, b_d):
    P, D = x_sorted.shape
    E, _, F2 = w_gu.shape
    F = F2 // 2
    nb = P // MOE_BLOCK
    grid_spec = pltpu.PrefetchScalarGridSpec(
        num_scalar_prefetch=2,
        grid=(nb,),
        in_specs=[pl.BlockSpec((MOE_BLOCK, D), lambda i, be, nu: (i, 0)),
                  pl.BlockSpec((1, D, F2), lambda i, be, nu: (be[i], 0, 0)),
                  pl.BlockSpec((1, 1, F2), lambda i, be, nu: (be[i], 0, 0)),
                  pl.BlockSpec((1, F, D), lambda i, be, nu: (be[i], 0, 0)),
                  pl.BlockSpec((1, 1, D), lambda i, be, nu: (be[i], 0, 0))],
        out_specs=pl.BlockSpec((MOE_BLOCK, D), lambda i, be, nu: (i, 0)),
        scratch_shapes=[pltpu.VMEM((D, F2), BF16), pltpu.VMEM((F, D), BF16)],
    )
    return pl.pallas_call(
        _expert_kernel,
        grid_spec=grid_spec,
        out_shape=jax.ShapeDtypeStruct((P, D), BF16),
        compiler_params=_params("arbitrary"),
        name="experts",
    )(blk_expert, n_used, x_sorted, w_gu, b_gu.reshape(E, 1, F2), w_d, b_d.reshape(E, 1, D))


def _combine_kernel(y_ref, w_ref, x1_ref, gpost_ref, gf_ref, o_ref):
    w = w_ref[0]
    y = (w[:, 0:1] * y_ref[0].astype(F32) + w[:, 1:2] * y_ref[1].astype(F32)
         + w[:, 2:3] * y_ref[2].astype(F32) + w[:, 3:4] * y_ref[3].astype(F32))
    o_ref[0] = x1_ref[0] + gf_ref[0] * _rms(y, gpost_ref[...])


def _combine(y4, wgt, x1, g_post, g_f):
    B, S, D = x1.shape
    tm = min(256, S)
    nt = S // tm
    return pl.pallas_call(
        _combine_kernel,
        grid=(B, nt),
        in_specs=[pl.BlockSpec((TOP_K, tm, D), lambda b, i: (0, b * nt + i, 0)),
                  pl.BlockSpec((1, tm, LANE), lambda b, i: (b, i, 0)),
                  pl.BlockSpec((1, tm, D), lambda b, i: (b, i, 0)),
                  pl.BlockSpec((1, D), lambda b, i: (0, 0)),
                  pl.BlockSpec((1, 1, D), lambda b, i: (b, 0, 0))],
        out_specs=pl.BlockSpec((1, tm, D), lambda b, i: (b, i, 0)),
        out_shape=jax.ShapeDtypeStruct((B, S, D), F32),
        compiler_params=_params("parallel", "arbitrary"),
        name="combine",
    )(y4, wgt, x1, g_post, g_f)


def _rot_cols(w):
    d, n = w.shape
    w4 = w.reshape(d, n // HEAD_DIM, 2, HEAD_DIM // 2)
    return jnp.concatenate([-w4[:, :, 1:2], w4[:, :, 0:1]], axis=2).reshape(d, n)


def _pad_cols(w, n):
    return jnp.pad(w, ((0, 0), (0, n - w.shape[1])))


def _pad_rows(w, n):
    return jnp.pad(w, ((0, n - w.shape[0]), (0, 0)))


def _layer(x, ada, l, mix_pre_norm, mix_post_norm, ffn_pre_norm, ffn_post_norm, w_in, cmp_k_pe, cmp_k_w1,
           cmp_k_w2, cmp_v_pe, cmp_v_w1, cmp_v_w2, nsa_out_norm, rwkv_mu, rwkv_w0, rwkv_w2, rwkv_a0, rwkv_a2,
           rwkv_g2, rwkv_k_k, rwkv_k_a, rwkv_r_k, rwkv_ln_w, rwkv_ln_b, w_out, router_w, router_b,
           expert_w_gate_up, expert_b_gate_up, expert_w_down, expert_b_down):
    B, S, D = x.shape
    T = B * S
    G, Dh, W = NSA_KV_GROUPS, HEAD_DIM, RWKV_WIDTH
    sh_m, sc_m, g_m, sh_f, sc_f, g_f = [a.reshape(B, 1, D) for a in jnp.split(ada, 6, axis=-1)]

    wi = w_in[l]
    KV = NSA_KV_WIDTH
    o = NSA_WIDTH
    wq, wkc, wvc, wks, wvs, wkw, wvw = (wi[:, 0:o],) + tuple(wi[:, o + j * KV:o + (j + 1) * KV] for j in range(6))
    wgl = wi[:, o + 6 * KV:o + 6 * KV + 3 * NSA_HEADS]
    wr = wi[:, o + 6 * KV + 3 * NSA_HEADS:]
    w_rope = jnp.concatenate([wq, wkc, wks, wkw], axis=1)
    w_rest = _pad_cols(jnp.concatenate([wvc, wvs, wvw, wgl], axis=1), REST_COLS)
    lora = [(3 * W, RWKV_W_LORA), (3 * W + RWKV_W_LORA, RWKV_A_LORA),
            (3 * W + RWKV_W_LORA + RWKV_A_LORA, RWKV_G_LORA)]
    pad_lora = lambda a: jnp.concatenate([_pad_cols(a[..., s:s + n], LANE) for s, n in lora], axis=-1)
    w_rw = jnp.concatenate([wr[:, 0:3 * W], pad_lora(wr)], axis=1)
    w_all = jnp.concatenate([w_rope, _rot_cols(w_rope), w_rest, w_rw], axis=1).astype(BF16)
    mu_row = rwkv_mu[l].reshape(1, -1)
    mu = jnp.concatenate([mu_row[:, 0:3 * W], pad_lora(mu_row)], axis=1)

    half = Dh // 2
    inv_freq = ROPE_THETA ** (-jnp.arange(half, dtype=F32) / half)
    ang = jnp.arange(S, dtype=F32)[:, None] * inv_freq[None, :]
    n_rope_heads = ROPE_COLS // Dh
    col_scale = jnp.where(jnp.arange(ROPE_COLS) < NSA_WIDTH, Dh ** -0.5 * math.log2(math.e), 1.0).astype(F32)
    cos_t = jnp.tile(jnp.cos(ang), (1, 2 * n_rope_heads)) * col_scale
    sin_t = jnp.tile(jnp.sin(ang), (1, 2 * n_rope_heads)) * col_scale

    qt, kc, ksw, vc, vt, glt, rw = _inproj(x, mix_pre_norm[l], 1.0 + sc_m, sh_m, w_all, cos_t, sin_t)

    nb = S // CMP_STRIDE
    regroup = lambda t: t.reshape(B, nb, CMP_STRIDE, G, Dh).transpose(0, 3, 1, 2, 4).reshape(B, G, nb, CMP_STRIDE * Dh)
    pe8 = lambda pe: jnp.broadcast_to(pe.reshape(1, -1), (8, pe.size))
    kcmp = _compress(regroup(kc), cmp_k_w1[l], cmp_k_w2[l], pe8(cmp_k_pe[l]), transposed=False)
    vcmpt = _compress(regroup(vc), cmp_v_w1[l], cmp_v_w2[l].T, pe8(cmp_v_pe[l]), transposed=True)
    ns = S // SLC_BLOCK
    c0 = jnp.arange(nb)[None, :] * CMP_STRIDE
    b0 = jnp.arange(ns)[:, None] * SLC_BLOCK
    ov = jnp.maximum(jnp.minimum(c0 + CMP_BLOCK, b0 + SLC_BLOCK) - jnp.maximum(c0, b0), 0)
    overlap_t = (ov.astype(F32) / CMP_BLOCK).astype(BF16)
    o_nsa_t = _nsa(qt, ksw, vt, kcmp, vcmpt, overlap_t, glt)

    bd = (jnp.arange(W)[:, None] // Dh == jnp.arange(W)[None, :] // Dh).astype(BF16)
    row = lambda a: a.reshape(1, -1)
    r, lw, k, v, kk, a, g = _rwkv_prep(
        rw, mu, row(rwkv_w0[l]), _pad_rows(rwkv_w2[l], LANE), row(rwkv_a0[l]), _pad_rows(rwkv_a2[l], LANE),
        _pad_rows(rwkv_g2[l], LANE), row(rwkv_k_k[l]), row(rwkv_k_a[l]), bd)
    o_rwkv = _rwkv_scan(r, lw, k, v, kk, a, g, row(rwkv_ln_w[l]), row(rwkv_ln_b[l]), row(rwkv_r_k[l]))

    wo = w_out[l].astype(BF16)
    rb = jnp.concatenate([router_b[l], jnp.full((LANE - N_EXPERTS,), NEG_INF, F32)]).reshape(1, LANE)
    x1, h2, idx, wgt = _outproj(o_nsa_t, o_rwkv, x, nsa_out_norm[l].reshape(-1, 1), wo[0:NSA_WIDTH], wo[NSA_WIDTH:],
                                row(mix_post_norm[l]), g_m, row(ffn_pre_norm[l]), 1.0 + sc_f, sh_f,
                                _pad_cols(router_w[l], LANE), rb)

    n_assign = T * TOP_K
    idx4 = idx.reshape(T, LANE)[:, 0:TOP_K]
    e_ids = jnp.arange(N_EXPERTS, dtype=I32)
    onehot = (idx4[:, :, None] == e_ids).astype(I32).sum(axis=1)
    csum = jnp.cumsum(onehot, axis=0)
    rank = jnp.take_along_axis(csum - onehot, idx4, axis=1)
    counts = csum[-1]
    starts = jnp.cumsum(counts) - counts
    padded = ((counts + MOE_BLOCK - 1) // MOE_BLOCK) * MOE_BLOCK
    pad_ends = jnp.cumsum(padded)
    pad_starts = pad_ends - padded
    pos = (pad_starts[idx4] + rank).astype(I32)
    n_blocks = -(-n_assign // MOE_BLOCK) + N_EXPERTS
    P = n_blocks * MOE_BLOCK
    blk_expert = jnp.minimum(
        jnp.searchsorted(pad_ends, jnp.arange(n_blocks) * MOE_BLOCK, side='right'), N_EXPERTS - 1).astype(I32)
    n_used = (pad_ends[-1] // MOE_BLOCK).astype(I32).reshape(1)
    order = jnp.argsort(idx4.reshape(-1))
    slot = jnp.arange(P, dtype=I32)
    slot_e = blk_expert[slot // MOE_BLOCK]
    offs = slot - pad_starts[slot_e]
    src = jnp.minimum(starts[slot_e] + offs, n_assign - 1)
    slot_tok = jnp.where(offs < counts[slot_e], order[src] // TOP_K, 0).astype(I32)

    x_sorted = h2.reshape(T, D)[slot_tok]
    y_sorted = _experts(blk_expert, n_used, x_sorted, expert_w_gate_up[l], expert_b_gate_up[l],
                        expert_w_down[l], expert_b_down[l])
    y4 = y_sorted[pos.T]
    return _combine(y4, wgt, x1, row(ffn_post_norm[l]), g_f)


def kernel(x, c, ada_w, ada_b, mix_pre_norm, mix_post_norm, ffn_pre_norm, ffn_post_norm, w_in, cmp_k_pe, cmp_k_w1, cmp_k_w2, cmp_v_pe, cmp_v_w1, cmp_v_w2, nsa_out_norm, rwkv_mu, rwkv_w0, rwkv_w2, rwkv_a0, rwkv_a2, rwkv_g2, rwkv_k_k, rwkv_k_a, rwkv_r_k, rwkv_ln_w, rwkv_ln_b, w_out, router_w, router_b, expert_w_gate_up, expert_b_gate_up, expert_w_down, expert_b_down):
    for l in range(ada_w.shape[0]):
        ada = _ada(c, ada_w[l], ada_b[l])
        x = _layer(x, ada, l, mix_pre_norm, mix_post_norm, ffn_pre_norm, ffn_post_norm, w_in, cmp_k_pe,
                   cmp_k_w1, cmp_k_w2, cmp_v_pe, cmp_v_w1, cmp_v_w2, nsa_out_norm, rwkv_mu, rwkv_w0, rwkv_w2,
                   rwkv_a0, rwkv_a2, rwkv_g2, rwkv_k_k, rwkv_k_a, rwkv_r_k, rwkv_ln_w, rwkv_ln_b, w_out,
                   router_w, router_b, expert_w_gate_up, expert_b_gate_up, expert_w_down, expert_b_down)
    return x
```

```python
import functools
import math

import jax
import jax.numpy as jnp
from jax import lax
from jax.experimental import pallas as pl
from jax.experimental.pallas import tpu as pltpu

F32 = jnp.float32
BF16 = jnp.bfloat16
I32 = jnp.int32

HEAD_DIM = 64
NSA_HEADS = 8
NSA_KV_GROUPS = 2
NSA_HPG = NSA_HEADS // NSA_KV_GROUPS
NSA_WIDTH = NSA_HEADS * HEAD_DIM
NSA_KV_WIDTH = NSA_KV_GROUPS * HEAD_DIM
CMP_BLOCK = 32
CMP_STRIDE = 16
SLC_BLOCK = 64
SLC_TOPK = 16
WINDOW = 512
Q_BLOCK = 128
RWKV_HEADS = 8
RWKV_WIDTH = RWKV_HEADS * HEAD_DIM
RWKV_W_LORA = 32
RWKV_A_LORA = 32
RWKV_G_LORA = 96
DECAY_SCALE = math.exp(-0.5)
GN_EPS = 64e-5
N_EXPERTS = 32
TOP_K = 4
SWIGLU_LIMIT = 7.0
SWIGLU_ALPHA = 1.702
ROPE_THETA = 10000.0
RMS_EPS = 1e-6
NEG_INF = -1e30
FORCED = 1e9
LOWEST = -3e38

LANE = 128
ROPE_COLS = NSA_WIDTH + 3 * NSA_KV_WIDTH
REST_COLS = 4 * LANE
RW_COLS = 3 * RWKV_WIDTH + 3 * LANE
SLC_KT = 512
SLC_UNROLL = 1
FLASH_LANES = 1024
FLASH_DEPTH = 1
ACC_ROWS = HEAD_DIM + 16
RWKV_CHUNK = 128
RWKV_BATCH_ROWS = 1
MOE_BLOCK = 512
VMEM_LIMIT = 56 * 1024 * 1024


def _bdot(a, b):
    return jnp.dot(a.astype(BF16), b.astype(BF16), preferred_element_type=F32)


def _split2(a):
    hi = a.astype(BF16)
    lo = (a - hi.astype(F32)).astype(BF16)
    return hi, lo


def _dot_lhs2(a, b_bf16):
    hi, lo = _split2(a)
    return (jnp.dot(hi, b_bf16, preferred_element_type=F32)
            + jnp.dot(lo, b_bf16, preferred_element_type=F32))


def _dot3(a, b):
    ah, al = _split2(a)
    bh, bl = _split2(b)
    return (jnp.dot(ah, bh, preferred_element_type=F32)
            + jnp.dot(al, bh, preferred_element_type=F32)
            + jnp.dot(ah, bl, preferred_element_type=F32))


def _params(*sem, flags=None):
    return pltpu.CompilerParams(dimension_semantics=sem, vmem_limit_bytes=VMEM_LIMIT, flags=flags)


def _ada_kernel(c_ref, w_ref, b_ref, o_ref):
    c = c_ref[...]
    act = c * jax.nn.sigmoid(c)
    o_ref[...] = _dot3(act, w_ref[...]) + b_ref[...]


def _ada(c, w, b):
    B, D = c.shape
    N = w.shape[1]
    tn = 1536
    return pl.pallas_call(
        _ada_kernel,
        grid=(N // tn,),
        in_specs=[pl.BlockSpec((B, D), lambda j: (0, 0)),
                  pl.BlockSpec((D, tn), lambda j: (0, j)),
                  pl.BlockSpec((1, tn), lambda j: (0, j))],
        out_specs=pl.BlockSpec((B, tn), lambda j: (0, j)),
        out_shape=jax.ShapeDtypeStruct((B, N), F32),
        compiler_params=_params("arbitrary"),
        name="ada",
    )(c, w, b.reshape(1, N))


def _inproj_kernel(x_ref, g_ref, sc_ref, sh_ref, w_ref, cos_ref, sin_ref,
                   q_ref, kc_ref, ksw_ref, vc_ref, vsw_ref, gl_ref, rw_ref):
    x = x_ref[0]
    ms = jnp.mean(x * x, axis=-1, keepdims=True)
    h = x * lax.rsqrt(ms + RMS_EPS) * g_ref[...]
    h = h * sc_ref[0] + sh_ref[0]
    hb = h.astype(BF16)
    R = ROPE_COLS
    main = jnp.dot(hb, w_ref[:, 0:R], preferred_element_type=F32)
    rot = jnp.dot(hb, w_ref[:, R:2 * R], preferred_element_type=F32)
    roped = main * cos_ref[...] + rot * sin_ref[...]
    q_ref[0] = roped[:, 0:NSA_WIDTH].T.astype(BF16)
    kc_ref[0] = roped[:, NSA_WIDTH:NSA_WIDTH + LANE]
    ksw_ref[0] = roped[:, NSA_WIDTH + LANE:R].astype(BF16)
    rest = jnp.dot(hb, w_ref[:, 2 * R:2 * R + REST_COLS], preferred_element_type=F32)
    vc_ref[0] = rest[:, 0:LANE]
    vsw_ref[0] = rest[:, LANE:3 * LANE].T.astype(BF16)
    gl_ref[0] = rest[:, 3 * LANE:4 * LANE].T
    rw_ref[0] = jnp.dot(hb, w_ref[:, 2 * R + REST_COLS:], preferred_element_type=F32)


def _inproj(x, gain, sc1, sh, w_all, cos_t, sin_t):
    B, S, D = x.shape
    tm = min(256, S)
    NW = w_all.shape[1]
    R = ROPE_COLS
    tok = lambda n: pl.BlockSpec((1, tm, n), lambda b, i: (b, i, 0))
    tok_t = lambda n: pl.BlockSpec((1, n, tm), lambda b, i: (b, 0, i))
    vec = pl.BlockSpec((1, 1, D), lambda b, i: (b, 0, 0))
    outs = [(NSA_WIDTH, BF16, True), (LANE, F32, False), (2 * LANE, BF16, False), (LANE, F32, False),
            (2 * LANE, BF16, True), (LANE, F32, True), (RW_COLS, F32, False)]
    return pl.pallas_call(
        _inproj_kernel,
        grid=(B, S // tm),
        in_specs=[tok(D), pl.BlockSpec((1, D), lambda b, i: (0, 0)), vec, vec,
                  pl.BlockSpec((D, NW), lambda b, i: (0, 0)),
                  pl.BlockSpec((tm, R), lambda b, i: (i, 0)),
                  pl.BlockSpec((tm, R), lambda b, i: (i, 0))],
        out_specs=[tok_t(n) if tr else tok(n) for n, _, tr in outs],
        out_shape=[jax.ShapeDtypeStruct((B, n, S) if tr else (B, S, n), dt) for n, dt, tr in outs],
        compiler_params=_params("parallel", "arbitrary"),
        name="inproj",
    )(x, gain.reshape(1, D), sc1, sh, w_all, cos_t, sin_t)


def _gelu_tanh(x):
    return 0.5 * x * (1.0 + jnp.tanh(math.sqrt(2.0 / math.pi) * (x + 0.044715 * (x * x * x))))


def _compress_kernel(r_ref, w1_ref, w2_ref, pe_ref, o_ref, *, transposed):
    R = r_ref[0, 0].astype(BF16)
    nb, half = R.shape
    w1 = w1_ref[...].astype(BF16)
    top = jnp.dot(R, w1[0:half], preferred_element_type=F32)
    bot = jnp.dot(R, w1[half:2 * half], preferred_element_type=F32)
    pe_term = jnp.dot(pe_ref[...].astype(BF16), w1, preferred_element_type=F32)[0:1]
    bot_next = pltpu.roll(bot, nb - 1, 0)
    hid = _gelu_tanh(top + bot_next + pe_term)
    if transposed:
        out = jnp.dot(w2_ref[...].astype(BF16), hid.T.astype(BF16), preferred_element_type=F32)
    else:
        out = jnp.dot(hid.astype(BF16), w2_ref[...].astype(BF16), preferred_element_type=F32)
    o_ref[0, 0] = out.astype(BF16)


def _compress(rr, w1, w2, pe, transposed):
    B, G, nb, half = rr.shape
    hid = w1.shape[-1]
    oshape = (HEAD_DIM, nb) if transposed else (nb, HEAD_DIM)
    return pl.pallas_call(
        functools.partial(_compress_kernel, transposed=transposed),
        grid=(B, G),
        in_specs=[pl.BlockSpec((1, 1, nb, half), lambda b, g: (b, g, 0, 0)),
                  pl.BlockSpec((2 * half, hid), lambda b, g: (0, 0)),
                  pl.BlockSpec(w2.shape, lambda b, g: (0, 0)),
                  pl.BlockSpec((8, 2 * half), lambda b, g: (0, 0))],
        out_specs=pl.BlockSpec((1, 1) + oshape, lambda b, g: (b, g, 0, 0)),
        out_shape=jax.ShapeDtypeStruct((B, G) + oshape, BF16),
        compiler_params=_params("arbitrary", "arbitrary"),
        name="compress_v" if transposed else "compress_k",
    )(rr, w1, w2, pe)


def _lanes4(a):
    return jnp.concatenate([a, a, a, a], axis=1)


def _nsa_kernel(qt_ref, ksw_ref, vt_ref, kcmp_ref, vcmpt_ref, ovt_ref, glt_ref, o_ref, selb_ref, *, seq_len):
    QB, Dh, HPG = Q_BLOCK, HEAD_DIM, NSA_HPG
    qi = pl.program_id(1)
    s0 = qi * QB
    ns = seq_len // SLC_BLOCK
    ncp = seq_len // CMP_STRIDE
    n_sel = min(SLC_TOPK, ns)
    kt_slc = min(SLC_KT, seq_len)
    t_row = s0 + lax.broadcasted_iota(I32, (1, QB), 1)
    t4 = _lanes4(t_row)
    gates = jax.nn.sigmoid(glt_ref[0])
    blk = lax.broadcasted_iota(I32, (ns, QB), 0)
    cur = lax.shift_right_logical(t_row, 6)
    forced = (blk == 0) | (blk == cur) | (blk == cur - 1)
    future = blk * SLC_BLOCK > t_row
    cmp_last = lax.broadcasted_iota(I32, (ncp, 1), 0) * CMP_STRIDE + (CMP_BLOCK - 1)
    cmask = cmp_last <= t4
    zeros_q = jnp.zeros((Dh, HPG * QB), BF16)
    G = NSA_KV_GROUPS
    NL = G * HPG * QB
    qpads, o_cmps, imps = [], [], []

    for g in range(G):
        qg = jnp.concatenate(
            [qt_ref[0, (g * HPG + h) * Dh:(g * HPG + h + 1) * Dh, :] for h in range(HPG)], axis=1)
        qpads.append(jnp.concatenate([qg, zeros_q] if g == 0 else [zeros_q, qg], axis=0))
        sc = jnp.where(cmask, jnp.dot(kcmp_ref[0, g], qg, preferred_element_type=F32), NEG_INF)
        m = jnp.max(sc, axis=0, keepdims=True)
        p = jnp.where(cmask, jnp.exp2(sc - m), 0.0)
        l = jnp.sum(p, axis=0, keepdims=True)
        p = p * (1.0 / jnp.maximum(l, 1e-30))
        o_cmps.append(jnp.dot(vcmpt_ref[0, g], p.astype(BF16), preferred_element_type=F32))
        psum = p[:, 0:QB] + p[:, QB:2 * QB] + p[:, 2 * QB:3 * QB] + p[:, 3 * QB:4 * QB]
        p_hi, p_lo = _split2(psum)
        imp = (jnp.dot(ovt_ref[...], p_hi, preferred_element_type=F32)
               + jnp.dot(ovt_ref[...], p_lo, preferred_element_type=F32))
        imps.append(jnp.where(future, NEG_INF, jnp.where(forced, FORCED, imp)))

    vals = jnp.concatenate(imps, axis=1)
    blk2 = jnp.concatenate([blk] * G, axis=1)
    sel = jnp.zeros((ns, G * QB), F32)
    for _ in range(n_sel):
        mx = jnp.max(vals, axis=0, keepdims=True)
        idx = jnp.min(jnp.where(vals == mx, blk2, ns), axis=0, keepdims=True)
        pick = blk2 == idx
        sel = jnp.where(pick, 1.0, sel)
        vals = jnp.where(pick, LOWEST, vals)
    for g in range(G):
        selb_ref[g] = jnp.where(sel[:, g * QB:(g + 1) * QB] > 0.5, 0.0, NEG_INF)

    q_slabs = [jnp.concatenate(qpads, axis=1)[:, sb * FLASH_LANES:(sb + 1) * FLASH_LANES]
               for sb in range(NL // FLASH_LANES)]
    n_slab = len(q_slabs)

    def load_tile(k_lane0, vrow0, k0, width, bias_of_group):
        ones = jnp.ones((ACC_ROWS - Dh, width), BF16)
        kt = ksw_ref[0, pl.ds(k0, width), k_lane0:k_lane0 + LANE]
        vaug = [jnp.concatenate([vt_ref[0, vrow0 + g * Dh:vrow0 + (g + 1) * Dh, pl.ds(k0, width)], ones], axis=0)
                for g in range(G)]
        bias = [bias_of_group(g) for g in range(G)]
        return kt, vaug, bias

    group_lanes = HPG * QB
    groups_per_slab = max(FLASH_LANES // group_lanes, 1)
    lanes_per_part = FLASH_LANES // groups_per_slab

    def flash_tiles(tiles, carry):
        m, acc = list(carry[0]), list(carry[1])
        steps = [(ti, sb) for ti in range(len(tiles)) for sb in range(n_slab)]
        scores = {}
        for i in range(len(steps) + FLASH_DEPTH):
            if i < len(steps):
                ti, sb = steps[i]
                scores[i] = jnp.dot(tiles[ti][0], q_slabs[sb], preferred_element_type=F32)
            j = i - FLASH_DEPTH
            if j >= 0:
                ti, sb = steps[j]
                _, vaug, bias = tiles[ti]
                g0 = sb * FLASH_LANES // group_lanes
                slab_bias = jnp.concatenate(
                    [bias[g0 + p] for p in range(groups_per_slab) for _ in range(lanes_per_part // QB)], axis=1)
                s = scores.pop(j) + slab_bias
                m_new = jnp.maximum(m[sb], jnp.max(s, axis=0, keepdims=True))
                alpha = jnp.exp2(m[sb] - m_new)
                pj = jnp.exp2(s - m_new).astype(BF16)
                pv = [jnp.dot(vaug[g0 + p], pj[:, p * lanes_per_part:(p + 1) * lanes_per_part],
                              preferred_element_type=F32) for p in range(groups_per_slab)]
                acc[sb] = alpha * acc[sb] + (pv[0] if len(pv) == 1 else jnp.concatenate(pv, axis=1))
                m[sb] = m_new
        return tuple(m), tuple(acc)

    def slc_tile(j):
        k0 = pl.multiple_of(j * kt_slc, kt_slc)
        nblk = kt_slc // SLC_BLOCK
        causal = k0 + lax.broadcasted_iota(I32, (kt_slc, 1), 0) <= t_row

        def bias(g):
            blk_bias = jnp.concatenate(
                [jnp.broadcast_to(selb_ref[g, pl.ds(j * nblk + i, 1), :], (SLC_BLOCK, QB)) for i in range(nblk)],
                axis=0)
            return jnp.where(causal, blk_bias, NEG_INF)

        return load_tile(0, 0, k0, kt_slc, bias)

    n_all = seq_len // kt_slc
    unroll = math.gcd(SLC_UNROLL, n_all)

    def slc_body(jj, carry):
        return flash_tiles([slc_tile(jj * unroll + u) for u in range(unroll)], carry)

    init = (tuple(jnp.full((1, FLASH_LANES), NEG_INF, F32) for _ in range(n_slab)),
            tuple(jnp.zeros((ACC_ROWS, FLASH_LANES), F32) for _ in range(n_slab)))
    n_tiles = (s0 + QB - 1) // kt_slc + 1
    _, acc_s = lax.fori_loop(0, (n_tiles + unroll - 1) // unroll, slc_body, init)
    acc_s = jnp.concatenate(acc_s, axis=1)
    o_slc = acc_s[0:Dh] * (1.0 / acc_s[Dh:Dh + 1])

    win_tiles = []
    for jj in range(min(WINDOW // QB + 1, seq_len // QB)):
        tile = qi - jj
        k0 = pl.multiple_of(jnp.maximum(tile, 0) * QB, QB)
        kpos = k0 + lax.broadcasted_iota(I32, (QB, 1), 0)
        wbias = jnp.where((kpos <= t_row) & (kpos > t_row - WINDOW) & (tile >= 0), 0.0, NEG_INF)
        win_tiles.append(load_tile(LANE, LANE, k0, QB, lambda g, wb=wbias: wb))
    _, acc_w = flash_tiles(win_tiles, init)
    acc_w = jnp.concatenate(acc_w, axis=1)
    o_win = acc_w[0:Dh] * (1.0 / acc_w[Dh:Dh + 1])

    for g in range(G):
        for h in range(HPG):
            c0 = (g * HPG + h) * 3
            cols = slice((g * HPG + h) * QB, (g * HPG + h + 1) * QB)
            o = (gates[c0:c0 + 1, :] * o_cmps[g][:, h * QB:(h + 1) * QB] + gates[c0 + 1:c0 + 2, :] * o_slc[:, cols]
                 + gates[c0 + 2:c0 + 3, :] * o_win[:, cols])
            o_ref[0, (g * HPG + h) * Dh:(g * HPG + h + 1) * Dh, :] = o


def _nsa(qt, ksw, vt, kcmp, vcmpt, overlap_t, glt):
    B, _, S = qt.shape
    ncp = S // CMP_STRIDE
    ns = S // SLC_BLOCK
    G = NSA_KV_GROUPS
    return pl.pallas_call(
        functools.partial(_nsa_kernel, seq_len=S),
        grid=(B, S // Q_BLOCK),
        in_specs=[pl.BlockSpec((1, NSA_WIDTH, Q_BLOCK), lambda b, i: (b, 0, i)),
                  pl.BlockSpec((1, S, 2 * LANE), lambda b, i: (b, 0, 0)),
                  pl.BlockSpec((1, 2 * LANE, S), lambda b, i: (b, 0, 0)),
                  pl.BlockSpec((1, G, ncp, HEAD_DIM), lambda b, i: (b, 0, 0, 0)),
                  pl.BlockSpec((1, G, HEAD_DIM, ncp), lambda b, i: (b, 0, 0, 0)),
                  pl.BlockSpec((ns, ncp), lambda b, i: (0, 0)),
                  pl.BlockSpec((1, LANE, Q_BLOCK), lambda b, i: (b, 0, i))],
        out_specs=pl.BlockSpec((1, NSA_WIDTH, Q_BLOCK), lambda b, i: (b, 0, i)),
        out_shape=jax.ShapeDtypeStruct((B, NSA_WIDTH, S), F32),
        scratch_shapes=[pltpu.VMEM((G, ns, Q_BLOCK), F32)],
        compiler_params=_params("parallel", "arbitrary"),
        name="nsa",
    )(qt, ksw, vt, kcmp, vcmpt, overlap_t, glt)


def _rwkv_prep_kernel(f_ref, p_ref, mu_ref, w0_ref, w2_ref, a0_ref, a2_ref, g2_ref, kk_ref, ka_ref,
                      bd_ref, r_o, lw_o, k_o, v_o, kk_o, a_o, g_o):
    W = RWKV_WIDTH
    i = pl.program_id(1)
    f = f_ref[0]
    tm = f.shape[0]
    prev_last = jnp.where(i > 0, p_ref[0, 7:8, :], 0.0)
    rolled = pltpu.roll(f, 1, 0)
    row = lax.broadcasted_iota(I32, (tm, 1), 0)
    prev = jnp.where(row == 0, prev_last, rolled)
    f = f + (prev - f) * mu_ref[...]
    r, k, v = f[:, 0:W], f[:, W:2 * W], f[:, 2 * W:3 * W]
    wd = f[:, 3 * W:3 * W + LANE]
    ad = f[:, 3 * W + LANE:3 * W + 2 * LANE]
    gd = f[:, 3 * W + 2 * LANE:3 * W + 3 * LANE]
    lw_o[0] = -DECAY_SCALE * jax.nn.sigmoid(w0_ref[...] + _dot3(jnp.tanh(wd), w2_ref[...]))
    a = jax.nn.sigmoid(a0_ref[...] + _dot3(ad, a2_ref[...]))
    g_o[0] = _dot3(jax.nn.sigmoid(gd), g2_ref[...])
    kk = k * kk_ref[...]
    ss = _dot_lhs2(kk * kk, bd_ref[...])
    kk_o[0] = kk / jnp.maximum(jnp.sqrt(ss), 1e-12)
    k_o[0] = k * (1.0 + (a - 1.0) * ka_ref[...])
    r_o[0] = r
    v_o[0] = v
    a_o[0] = a


def _rwkv_prep(rw, mu, w0, w2, a0, a2, g2, k_k, k_a, bd):
    B, S, C = rw.shape
    W = RWKV_WIDTH
    tm = min(256, S)
    row = lambda n: pl.BlockSpec((1, n), lambda b, i: (0, 0))
    mat = pl.BlockSpec((LANE, W), lambda b, i: (0, 0))
    tok = pl.BlockSpec((1, tm, W), lambda b, i: (b, i, 0))
    return pl.pallas_call(
        _rwkv_prep_kernel,
        grid=(B, S // tm),
        in_specs=[pl.BlockSpec((1, tm, C), lambda b, i: (b, i, 0)),
                  pl.BlockSpec((1, 8, C), lambda b, i: (b, jnp.maximum(i * (tm // 8) - 1, 0), 0)),
                  row(C), row(W), mat, row(W), mat, mat, row(W), row(W),
                  pl.BlockSpec((W, W), lambda b, i: (0, 0))],
        out_specs=[tok] * 7,
        out_shape=[jax.ShapeDtypeStruct((B, S, W), F32)] * 7,
        compiler_params=_params("parallel", "arbitrary"),
        name="rwkv_prep",
    )(rw, rw, mu, w0, w2, a0, a2, g2, k_k, k_a, bd)


def _rwkv_scan_kernel(r_ref, lw_ref, k_ref, v_ref, kk_ref, a_ref, g_ref, lnw_ref, lnb_ref, rk_ref,
                      o_ref, st_ref):
    Dh = HEAD_DIM

    @pl.when(pl.program_id(1) == 0)
    def _():
        st_ref[...] = jnp.zeros_like(st_ref)

    NB, C = r_ref.shape[0], r_ref.shape[1]
    ri = lax.broadcasted_iota(I32, (C, C), 0)
    ci = lax.broadcasted_iota(I32, (C, C), 1)
    incl = ri >= ci
    strict = ri > ci
    tri = incl.astype(BF16)
    eye = (ri == ci).astype(F32)
    n_double = max(int(math.log2(C)) - 1, 0)

    def scaled_operands(bb):
        r, lw, k, kk, a = (ref[bb] for ref in (r_ref, lw_ref, k_ref, kk_ref, a_ref))
        l_hi = lw.astype(BF16)
        l_r1 = lw - l_hi.astype(F32)
        l_mid = l_r1.astype(BF16)
        l_lo = (l_r1 - l_mid.astype(F32)).astype(BF16)
        L = (jnp.dot(tri, l_hi, preferred_element_type=F32) + jnp.dot(tri, l_mid, preferred_element_type=F32)
             + jnp.dot(tri, l_lo, preferred_element_type=F32))
        e_l = jnp.exp(L)
        e_inv = jnp.exp(-L)
        e_end = jnp.exp(L[C - 1:C, :] - L)
        b = kk * a
        x1 = jnp.concatenate([kk * jnp.exp(L - lw), r * e_l], axis=0).astype(BF16)
        x2t = jnp.concatenate([k * e_inv, b * e_inv], axis=0).T.astype(BF16)
        zt = jnp.concatenate([k * e_end, -(b * e_end)], axis=0).T.astype(BF16)
        gt = jnp.concatenate([e_l, e_l], axis=0).T
        return x1, x2t, zt, gt

    ops = [scaled_operands(bb) for bb in range(NB)]
    chains = [(bb, h) for bb in range(NB) for h in range(RWKV_HEADS)]
    sls = [slice(h * Dh, (h + 1) * Dh) for _, h in chains]
    h0s = [st_ref[bb, h] for bb, h in chains]
    vhs = [v_ref[bb][:, sl] for (bb, _), sl in zip(chains, sls)]
    x1s = [ops[bb][0][:, sl] for (bb, _), sl in zip(chains, sls)]
    amats = [jnp.dot(x1, ops[bb][1][sl, :], preferred_element_type=F32)
             for x1, (bb, _), sl in zip(x1s, chains, sls)]
    p0s = [jnp.dot(x1, h0.astype(BF16), preferred_element_type=F32) for x1, h0 in zip(x1s, h0s)]
    a_kk = [jnp.where(strict, am[0:C, 0:C], 0.0).astype(BF16) for am in amats]
    a_rkb = [jnp.concatenate([jnp.where(incl, am[C:2 * C, 0:C], 0.0), jnp.where(incl, -am[C:2 * C, C:2 * C], 0.0)],
                             axis=1).astype(BF16) for am in amats]
    nmats = [jnp.where(strict, -am[0:C, C:2 * C], 0.0) for am in amats]
    tinvs = [eye + nm for nm in nmats]
    for _ in range(n_double):
        nmats = [_bdot(nm, nm) for nm in nmats]
        tinvs = [ti + _bdot(ti, nm) for ti, nm in zip(tinvs, nmats)]
    akv = [jnp.dot(ak, vh.astype(BF16), preferred_element_type=F32) for ak, vh in zip(a_kk, vhs)]
    us = [_bdot(ti, p0[0:C] + av) for ti, p0, av in zip(tinvs, p0s, akv)]
    vus = [jnp.concatenate([vh, u], axis=0).astype(BF16) for vh, u in zip(vhs, us)]
    ys = [p0[C:2 * C] + jnp.dot(ar, vu, preferred_element_type=F32) for p0, ar, vu in zip(p0s, a_rkb, vus)]
    for i, (bb, h) in enumerate(chains):
        _, _, zt, gt = ops[bb]
        st_ref[bb, h] = gt[sls[i], C - 1:C] * h0s[i] + jnp.dot(zt[sls[i], :], vus[i], preferred_element_type=F32)
    for i, (bb, h) in enumerate(chains):
        sl, y = sls[i], ys[i]
        mean = jnp.mean(y, axis=-1, keepdims=True)
        yc = y - mean
        var = jnp.mean(yc * yc, axis=-1, keepdims=True)
        yn = yc * lax.rsqrt(var + GN_EPS) * lnw_ref[:, sl] + lnb_ref[:, sl]
        bonus = jnp.sum(r_ref[bb][:, sl] * k_ref[bb][:, sl] * rk_ref[:, sl], axis=-1, keepdims=True) * vhs[i]
        o_ref[bb, :, sl] = (yn + bonus) * g_ref[bb][:, sl]


def _rwkv_scan(r, lw, k, v, kk, a, g, ln_w, ln_b, r_k):
    B, S, W = r.shape
    C = min(RWKV_CHUNK, S)
    nb = math.gcd(RWKV_BATCH_ROWS, B)
    tok = pl.BlockSpec((nb, C, W), lambda b, i: (b, i, 0))
    row = pl.BlockSpec((1, W), lambda b, i: (0, 0))
    return pl.pallas_call(
        _rwkv_scan_kernel,
        grid=(B // nb, S // C),
        in_specs=[tok] * 7 + [row] * 3,
        out_specs=tok,
        out_shape=jax.ShapeDtypeStruct((B, S, W), F32),
        scratch_shapes=[pltpu.VMEM((nb, RWKV_HEADS, HEAD_DIM, HEAD_DIM), F32)],
        compiler_params=_params("parallel", "arbitrary"),
        name="rwkv_scan",
    )(r, lw, k, v, kk, a, g, ln_w, ln_b, r_k)


def _rms(x, g):
    return x * lax.rsqrt(jnp.mean(x * x, axis=-1, keepdims=True) + RMS_EPS) * g


def _outproj_kernel(on_ref, or_ref, x_ref, gn_ref, wt_ref, wb_ref, gpost_ref, gm_ref, gpre_ref,
                    scf_ref, shf_ref, rw_ref, rb_ref, x1_ref, h2_ref, idx_ref, wgt_ref):
    ont = on_ref[0]
    ont = ont * lax.rsqrt(jnp.mean(ont * ont, axis=0, keepdims=True) + RMS_EPS) * gn_ref[...]
    on = ont.T
    mixed = (jnp.dot(on.astype(BF16), wt_ref[...], preferred_element_type=F32)
             + jnp.dot(or_ref[0].astype(BF16), wb_ref[...], preferred_element_type=F32))
    x1 = x_ref[0] + gm_ref[0] * _rms(mixed, gpost_ref[...])
    x1_ref[0] = x1
    h2 = _rms(x1, gpre_ref[...]) * scf_ref[0] + shf_ref[0]
    h2_ref[0] = h2.astype(BF16)
    logits = _dot3(h2, rw_ref[...]) + rb_ref[...]
    lane = lax.broadcasted_iota(I32, logits.shape, 1)
    vals = logits
    top_v, top_i = [], []
    for _ in range(TOP_K):
        mx = jnp.max(vals, axis=-1, keepdims=True)
        ix = jnp.min(jnp.where(vals == mx, lane, LANE), axis=-1, keepdims=True)
        top_v.append(mx)
        top_i.append(ix)
        vals = jnp.where(lane == ix, LOWEST, vals)
    ex = [jnp.exp(tv - top_v[0]) for tv in top_v]
    inv = 1.0 / (ex[0] + ex[1] + ex[2] + ex[3])
    idx = jnp.zeros(logits.shape, I32)
    wgt = jnp.zeros(logits.shape, F32)
    for kk in range(TOP_K):
        idx = jnp.where(lane == kk, top_i[kk], idx)
        wgt = jnp.where(lane == kk, ex[kk] * inv, wgt)
    idx_ref[0] = idx
    wgt_ref[0] = wgt


def _outproj(o_nsa, o_rwkv, x, gn, w_top, w_bot, g_post, g_m, g_pre, sc1_f, sh_f, router_w, router_b):
    B, S, D = x.shape
    tm = min(256, S)
    W = o_rwkv.shape[-1]
    tok = lambda n: pl.BlockSpec((1, tm, n), lambda b, i: (b, i, 0))
    vec = pl.BlockSpec((1, 1, D), lambda b, i: (b, 0, 0))
    row = lambda n: pl.BlockSpec((1, n), lambda b, i: (0, 0))
    return pl.pallas_call(
        _outproj_kernel,
        grid=(B, S // tm),
        in_specs=[pl.BlockSpec((1, W, tm), lambda b, i: (b, 0, i)), tok(W), tok(D),
                  pl.BlockSpec((W, 1), lambda b, i: (0, 0)),
                  pl.BlockSpec((W, D), lambda b, i: (0, 0)), pl.BlockSpec((W, D), lambda b, i: (0, 0)),
                  row(D), vec, row(D), vec, vec,
                  pl.BlockSpec((D, LANE), lambda b, i: (0, 0)), row(LANE)],
        out_specs=[tok(D), tok(D), tok(LANE), tok(LANE)],
        out_shape=[jax.ShapeDtypeStruct((B, S, D), F32), jax.ShapeDtypeStruct((B, S, D), BF16),
                   jax.ShapeDtypeStruct((B, S, LANE), I32), jax.ShapeDtypeStruct((B, S, LANE), F32)],
        compiler_params=_params("parallel", "arbitrary"),
        name="outproj",
    )(o_nsa, o_rwkv, x, gn, w_top, w_bot, g_post, g_m, g_pre, sc1_f, sh_f, router_w, router_b)


def _expert_kernel(be_ref, nu_ref, x_ref, wgu_ref, bgu_ref, wd_ref, bd_ref, o_ref, wgu_bf, wd_bf):
    F = wd_ref.shape[1]
    i = pl.program_id(0)
    active = i < nu_ref[0]
    new_expert = (i == 0) | (be_ref[i] != be_ref[jnp.maximum(i - 1, 0)])

    @pl.when(active & new_expert)
    def _():
        wgu_bf[...] = wgu_ref[0].astype(BF16)
        wd_bf[...] = wd_ref[0].astype(BF16)

    @pl.when(active)
    def _():
        gu = jnp.dot(x_ref[...], wgu_bf[...], preferred_element_type=F32) + bgu_ref[0]
        gate = jnp.minimum(gu[:, 0:F], SWIGLU_LIMIT)
        up = jnp.clip(gu[:, F:2 * F], -SWIGLU_LIMIT, SWIGLU_LIMIT)
        glu = gate * jax.nn.sigmoid(gate * SWIGLU_ALPHA)
        out = jnp.dot(((up + 1.0) * glu).astype(BF16), wd_bf[...], preferred_element_type=F32) + bd_ref[0]
        o_ref[...] = out.astype(o_ref.dtype)


def _experts(blk_expert, n_used, x_sorted, w_gu, b_gu, w_d, b_d):
    P, D = x_sorted.shape
    E, _, F2 = w_gu.shape
    F = F2 // 2
    nb = P // MOE_BLOCK
    grid_spec = pltpu.PrefetchScalarGridSpec(
        num_scalar_prefetch=2,
        grid=(nb,),
        in_specs=[pl.BlockSpec((MOE_BLOCK, D), lambda i, be, nu: (i, 0)),
                  pl.BlockSpec((1, D, F2), lambda i, be, nu: (be[i], 0, 0)),
                  pl.BlockSpec((1, 1, F2), lambda i, be, nu: (be[i], 0, 0)),
                  pl.BlockSpec((1, F, D), lambda i, be, nu: (be[i], 0, 0)),
                  pl.BlockSpec((1, 1, D), lambda i, be, nu: (be[i], 0, 0))],
        out_specs=pl.BlockSpec((MOE_BLOCK, D), lambda i, be, nu: (i, 0)),
        scratch_shapes=[pltpu.VMEM((D, F2), BF16), pltpu.VMEM((F, D), BF16)],
    )
    return pl.pallas_call(
        _expert_kernel,
        grid_spec=grid_spec,
        out_shape=jax.ShapeDtypeStruct((P, D), BF16),
        compiler_params=_params("arbitrary"),
        name="experts",
    )(blk_expert, n_used, x_sorted, w_gu, b_gu.reshape(E, 1, F2), w_d, b_d.reshape(E, 1, D))


def _combine_kernel(y_ref, w_ref, x1_ref, gpost_ref, gf_ref, o_ref):
    w = w_ref[0]
    y = (w[:, 0:1] * y_ref[0].astype(F32) + w[:, 1:2] * y_ref[1].astype(F32)
         + w[:, 2:3] * y_ref[2].astype(F32) + w[:, 3:4] * y_ref[3].astype(F32))
    o_ref[0] = x1_ref[0] + gf_ref[0] * _rms(y, gpost_ref[...])


def _combine(y4, wgt, x1, g_post, g_f):
    B, S, D = x1.shape
    tm = min(256, S)
    nt = S // tm
    return pl.pallas_call(
        _combine_kernel,
        grid=(B, nt),
        in_specs=[pl.BlockSpec((TOP_K, tm, D), lambda b, i: (0, b * nt + i, 0)),
                  pl.BlockSpec((1, tm, LANE), lambda b, i: (b, i, 0)),
                  pl.BlockSpec((1, tm, D), lambda b, i: (b, i, 0)),
                  pl.BlockSpec((1, D), lambda b, i: (0, 0)),
                  pl.BlockSpec((1, 1, D), lambda b, i: (b, 0, 0))],
        out_specs=pl.BlockSpec((1, tm, D), lambda b, i: (b, i, 0)),
        out_shape=jax.ShapeDtypeStruct((B, S, D), F32),
        compiler_params=_params("parallel", "arbitrary"),
        name="combine",
    )(y4, wgt, x1, g_post, g_f)


def _rot_cols(w):
    d, n = w.shape
    w4 = w.reshape(d, n // HEAD_DIM, 2, HEAD_DIM // 2)
    return jnp.concatenate([-w4[:, :, 1:2], w4[:, :, 0:1]], axis=2).reshape(d, n)


def _pad_cols(w, n):
    return jnp.pad(w, ((0, 0), (0, n - w.shape[1])))


def _pad_rows(w, n):
    return jnp.pad(w, ((0, n - w.shape[0]), (0, 0)))


def _layer(x, ada, l, mix_pre_norm, mix_post_norm, ffn_pre_norm, ffn_post_norm, w_in, cmp_k_pe, cmp_k_w1,
           cmp_k_w2, cmp_v_pe, cmp_v_w1, cmp_v_w2, nsa_out_norm, rwkv_mu, rwkv_w0, rwkv_w2, rwkv_a0, rwkv_a2,
           rwkv_g2, rwkv_k_k, rwkv_k_a, rwkv_r_k, rwkv_ln_w, rwkv_ln_b, w_out, router_w, router_b,
           expert_w_gate_up, expert_b_gate_up, expert_w_down, expert_b_down):
    B, S, D = x.shape
    T = B * S
    G, Dh, W = NSA_KV_GROUPS, HEAD_DIM, RWKV_WIDTH
    sh_m, sc_m, g_m, sh_f, sc_f, g_f = [a.reshape(B, 1, D) for a in jnp.split(ada, 6, axis=-1)]

    wi = w_in[l]
    KV = NSA_KV_WIDTH
    o = NSA_WIDTH
    wq, wkc, wvc, wks, wvs, wkw, wvw = (wi[:, 0:o],) + tuple(wi[:, o + j * KV:o + (j + 1) * KV] for j in range(6))
    wgl = wi[:, o + 6 * KV:o + 6 * KV + 3 * NSA_HEADS]
    wr = wi[:, o + 6 * KV + 3 * NSA_HEADS:]
    w_rope = jnp.concatenate([wq, wkc, wks, wkw], axis=1)
    w_rest = _pad_cols(jnp.concatenate([wvc, wvs, wvw, wgl], axis=1), REST_COLS)
    lora = [(3 * W, RWKV_W_LORA), (3 * W + RWKV_W_LORA, RWKV_A_LORA),
            (3 * W + RWKV_W_LORA + RWKV_A_LORA, RWKV_G_LORA)]
    pad_lora = lambda a: jnp.concatenate([_pad_cols(a[..., s:s + n], LANE) for s, n in lora], axis=-1)
    w_rw = jnp.concatenate([wr[:, 0:3 * W], pad_lora(wr)], axis=1)
    w_all = jnp.concatenate([w_rope, _rot_cols(w_rope), w_rest, w_rw], axis=1).astype(BF16)
    mu_row = rwkv_mu[l].reshape(1, -1)
    mu = jnp.concatenate([mu_row[:, 0:3 * W], pad_lora(mu_row)], axis=1)

    half = Dh // 2
    inv_freq = ROPE_THETA ** (-jnp.arange(half, dtype=F32) / half)
    ang = jnp.arange(S, dtype=F32)[:, None] * inv_freq[None, :]
    n_rope_heads = ROPE_COLS // Dh
    col_scale = jnp.where(jnp.arange(ROPE_COLS) < NSA_WIDTH, Dh ** -0.5 * math.log2(math.e), 1.0).astype(F32)
    cos_t = jnp.tile(jnp.cos(ang), (1, 2 * n_rope_heads)) * col_scale
    sin_t = jnp.tile(jnp.sin(ang), (1, 2 * n_rope_heads)) * col_scale

    qt, kc, ksw, vc, vt, glt, rw = _inproj(x, mix_pre_norm[l], 1.0 + sc_m, sh_m, w_all, cos_t, sin_t)

    nb = S // CMP_STRIDE
    regroup = lambda t: t.reshape(B, nb, CMP_STRIDE, G, Dh).transpose(0, 3, 1, 2, 4).reshape(B, G, nb, CMP_STRIDE * Dh)
    pe8 = lambda pe: jnp.broadcast_to(pe.reshape(1, -1), (8, pe.size))
    kcmp = _compress(regroup(kc), cmp_k_w1[l], cmp_k_w2[l], pe8(cmp_k_pe[l]), transposed=False)
    vcmpt = _compress(regroup(vc), cmp_v_w1[l], cmp_v_w2[l].T, pe8(cmp_v_pe[l]), transposed=True)
    ns = S // SLC_BLOCK
    c0 = jnp.arange(nb)[None, :] * CMP_STRIDE
    b0 = jnp.arange(ns)[:, None] * SLC_BLOCK
    ov = jnp.maximum(jnp.minimum(c0 + CMP_BLOCK, b0 + SLC_BLOCK) - jnp.maximum(c0, b0), 0)
    overlap_t = (ov.astype(F32) / CMP_BLOCK).astype(BF16)
    o_nsa_t = _nsa(qt, ksw, vt, kcmp, vcmpt, overlap_t, glt)

    bd = (jnp.arange(W)[:, None] // Dh == jnp.arange(W)[None, :] // Dh).astype(BF16)
    row = lambda a: a.reshape(1, -1)
    r, lw, k, v, kk, a, g = _rwkv_prep(
        rw, mu, row(rwkv_w0[l]), _pad_rows(rwkv_w2[l], LANE), row(rwkv_a0[l]), _pad_rows(rwkv_a2[l], LANE),
        _pad_rows(rwkv_g2[l], LANE), row(rwkv_k_k[l]), row(rwkv_k_a[l]), bd)
    o_rwkv = _rwkv_scan(r, lw, k, v, kk, a, g, row(rwkv_ln_w[l]), row(rwkv_ln_b[l]), row(rwkv_r_k[l]))

    wo = w_out[l].astype(BF16)
    rb = jnp.concatenate([router_b[l], jnp.full((LANE - N_EXPERTS,), NEG_INF, F32)]).reshape(1, LANE)
    x1, h2, idx, wgt = _outproj(o_nsa_t, o_rwkv, x, nsa_out_norm[l].reshape(-1, 1), wo[0:NSA_WIDTH], wo[NSA_WIDTH:],
                                row(mix_post_norm[l]), g_m, row(ffn_pre_norm[l]), 1.0 + sc_f, sh_f,
                                _pad_cols(router_w[l], LANE), rb)

    n_assign = T * TOP_K
    idx4 = idx.reshape(T, LANE)[:, 0:TOP_K]
    e_ids = jnp.arange(N_EXPERTS, dtype=I32)
    hot = idx4[:, :, None] == e_ids
    onehot = hot.astype(I32).sum(axis=1)
    csum = jnp.cumsum(onehot, axis=0)
    counts = csum[-1]
    starts = jnp.cumsum(counts) - counts
    padded = ((counts + MOE_BLOCK - 1) // MOE_BLOCK) * MOE_BLOCK
    pad_ends = jnp.cumsum(padded)
    pad_starts = pad_ends - padded
    pos = jnp.where(hot, (csum - onehot + pad_starts)[:, None, :], 0).sum(axis=-1).astype(I32)
    n_blocks = -(-n_assign // MOE_BLOCK) + N_EXPERTS
    P = n_blocks * MOE_BLOCK
    blk_start = jnp.arange(n_blocks, dtype=I32) * MOE_BLOCK
    blk_expert = jnp.minimum(jnp.searchsorted(pad_ends, blk_start, side='right'), N_EXPERTS - 1).astype(I32)
    n_used = (pad_ends[-1] // MOE_BLOCK).astype(I32).reshape(1)
    order = jnp.argsort(idx4.reshape(-1))
    blk_off = blk_start - pad_starts[blk_expert]
    within = jnp.arange(MOE_BLOCK, dtype=I32)[None, :]
    src = jnp.minimum((starts[blk_expert] + blk_off)[:, None] + within, n_assign - 1)
    valid = within < (counts[blk_expert] - blk_off)[:, None]
    slot_tok = jnp.where(valid, order[src] // TOP_K, 0).astype(I32).reshape(P)

    as_words = lambda a: lax.bitcast_convert_type(a.reshape(a.shape[0], D // 2, 2), jnp.uint32)
    as_rows = lambda a: lax.bitcast_convert_type(a, BF16).reshape(a.shape[:-1] + (D,))
    x_sorted = as_rows(as_words(h2.reshape(T, D))[slot_tok])
    y_sorted = _experts(blk_expert, n_used, x_sorted, expert_w_gate_up[l], expert_b_gate_up[l],
                        expert_w_down[l], expert_b_down[l])
    y4 = as_rows(as_words(y_sorted)[pos.T])
    return _combine(y4, wgt, x1, row(ffn_post_norm[l]), g_f)


def kernel(x, c, ada_w, ada_b, mix_pre_norm, mix_post_norm, ffn_pre_norm, ffn_post_norm, w_in, cmp_k_pe, cmp_k_w1, cmp_k_w2, cmp_v_pe, cmp_v_w1, cmp_v_w2, nsa_out_norm, rwkv_mu, rwkv_w0, rwkv_w2, rwkv_a0, rwkv_a2, rwkv_g2, rwkv_k_k, rwkv_k_a, rwkv_r_k, rwkv_ln_w, rwkv_ln_b, w_out, router_w, router_b, expert_w_gate_up, expert_b_gate_up, expert_w_down, expert_b_down):
    for l in range(ada_w.shape[0]):
        ada = _ada(c, ada_w[l], ada_b[l])
        x = _layer(x, ada, l, mix_pre_norm, mix_post_norm, ffn_pre_norm, ffn_post_norm, w_in, cmp_k_pe,
                   cmp_k_w1, cmp_k_w2, cmp_v_pe, cmp_v_w1, cmp_v_w2, nsa_out_norm, rwkv_mu, rwkv_w0, rwkv_w2,
                   rwkv_a0, rwkv_a2, rwkv_g2, rwkv_k_k, rwkv_k_a, rwkv_r_k, rwkv_ln_w, rwkv_ln_b, w_out,
                   router_w, router_b, expert_w_gate_up, expert_b_gate_up, expert_w_down, expert_b_down)
    return x
```

```python
import functools
import math

import jax
import jax.numpy as jnp
from jax import lax
from jax.experimental import pallas as pl
from jax.experimental.pallas import tpu as pltpu

F32 = jnp.float32
BF16 = jnp.bfloat16
I32 = jnp.int32

HEAD_DIM = 64
NSA_HEADS = 8
NSA_KV_GROUPS = 2
NSA_HPG = NSA_HEADS // NSA_KV_GROUPS
NSA_WIDTH = NSA_HEADS * HEAD_DIM
NSA_KV_WIDTH = NSA_KV_GROUPS * HEAD_DIM
CMP_BLOCK = 32
CMP_STRIDE = 16
SLC_BLOCK = 64
SLC_TOPK = 16
WINDOW = 512
Q_BLOCK = 128
RWKV_HEADS = 8
RWKV_WIDTH = RWKV_HEADS * HEAD_DIM
RWKV_W_LORA = 32
RWKV_A_LORA = 32
RWKV_G_LORA = 96
DECAY_SCALE = math.exp(-0.5)
GN_EPS = 64e-5
N_EXPERTS = 32
TOP_K = 4
SWIGLU_LIMIT = 7.0
SWIGLU_ALPHA = 1.702
ROPE_THETA = 10000.0
RMS_EPS = 1e-6
NEG_INF = -1e30
FORCED = 1e9
LOWEST = -3e38

LANE = 128
ROPE_COLS = NSA_WIDTH + 3 * NSA_KV_WIDTH
REST_COLS = 4 * LANE
RW_COLS = 3 * RWKV_WIDTH + 3 * LANE
SLC_KT = 512
SLC_UNROLL = 1
FLASH_LANES = 1024
FLASH_DEPTH = 1
ACC_ROWS = HEAD_DIM + 16
RWKV_CHUNK = 128
RWKV_BATCH_ROWS = 1
MOE_BLOCK = 512
VMEM_LIMIT = 56 * 1024 * 1024


def _bdot(a, b):
    return jnp.dot(a.astype(BF16), b.astype(BF16), preferred_element_type=F32)


def _split2(a):
    hi = a.astype(BF16)
    lo = (a - hi.astype(F32)).astype(BF16)
    return hi, lo


def _dot_lhs2(a, b_bf16):
    hi, lo = _split2(a)
    return (jnp.dot(hi, b_bf16, preferred_element_type=F32)
            + jnp.dot(lo, b_bf16, preferred_element_type=F32))


def _dot3(a, b):
    ah, al = _split2(a)
    bh, bl = _split2(b)
    return (jnp.dot(ah, bh, preferred_element_type=F32)
            + jnp.dot(al, bh, preferred_element_type=F32)
            + jnp.dot(ah, bl, preferred_element_type=F32))


def _params(*sem, flags=None):
    return pltpu.CompilerParams(dimension_semantics=sem, vmem_limit_bytes=VMEM_LIMIT, flags=flags)


def _ada_kernel(c_ref, w_ref, b_ref, o_ref):
    c = c_ref[...]
    act = c * jax.nn.sigmoid(c)
    o_ref[...] = _dot3(act, w_ref[...]) + b_ref[...]


def _ada(c, w, b):
    B, D = c.shape
    N = w.shape[1]
    tn = 1536
    return pl.pallas_call(
        _ada_kernel,
        grid=(N // tn,),
        in_specs=[pl.BlockSpec((B, D), lambda j: (0, 0)),
                  pl.BlockSpec((D, tn), lambda j: (0, j)),
                  pl.BlockSpec((1, tn), lambda j: (0, j))],
        out_specs=pl.BlockSpec((B, tn), lambda j: (0, j)),
        out_shape=jax.ShapeDtypeStruct((B, N), F32),
        compiler_params=_params("arbitrary"),
        name="ada",
    )(c, w, b.reshape(1, N))


def _inproj_kernel(x_ref, g_ref, sc_ref, sh_ref, w_ref, cos_ref, sin_ref,
                   q_ref, kc_ref, ksw_ref, vc_ref, vsw_ref, gl_ref, rw_ref):
    x = x_ref[0]
    ms = jnp.mean(x * x, axis=-1, keepdims=True)
    h = x * lax.rsqrt(ms + RMS_EPS) * g_ref[...]
    h = h * sc_ref[0] + sh_ref[0]
    hb = h.astype(BF16)
    R = ROPE_COLS
    main = jnp.dot(hb, w_ref[:, 0:R], preferred_element_type=F32)
    rot = jnp.dot(hb, w_ref[:, R:2 * R], preferred_element_type=F32)
    roped = main * cos_ref[...] + rot * sin_ref[...]
    q_ref[0] = roped[:, 0:NSA_WIDTH].T.astype(BF16)
    kc_ref[0] = roped[:, NSA_WIDTH:NSA_WIDTH + LANE]
    ksw_ref[0] = roped[:, NSA_WIDTH + LANE:R].astype(BF16)
    rest = jnp.dot(hb, w_ref[:, 2 * R:2 * R + REST_COLS], preferred_element_type=F32)
    vc_ref[0] = rest[:, 0:LANE]
    vsw_ref[0] = rest[:, LANE:3 * LANE].T.astype(BF16)
    gl_ref[0] = rest[:, 3 * LANE:4 * LANE].T
    rw_ref[0] = jnp.dot(hb, w_ref[:, 2 * R + REST_COLS:], preferred_element_type=F32)


def _inproj(x, gain, sc1, sh, w_all, cos_t, sin_t):
    B, S, D = x.shape
    tm = min(256, S)
    NW = w_all.shape[1]
    R = ROPE_COLS
    tok = lambda n: pl.BlockSpec((1, tm, n), lambda b, i: (b, i, 0))
    tok_t = lambda n: pl.BlockSpec((1, n, tm), lambda b, i: (b, 0, i))
    vec = pl.BlockSpec((1, 1, D), lambda b, i: (b, 0, 0))
    outs = [(NSA_WIDTH, BF16, True), (LANE, F32, False), (2 * LANE, BF16, False), (LANE, F32, False),
            (2 * LANE, BF16, True), (LANE, F32, True), (RW_COLS, F32, False)]
    return pl.pallas_call(
        _inproj_kernel,
        grid=(B, S // tm),
        in_specs=[tok(D), pl.BlockSpec((1, D), lambda b, i: (0, 0)), vec, vec,
                  pl.BlockSpec((D, NW), lambda b, i: (0, 0)),
                  pl.BlockSpec((tm, R), lambda b, i: (i, 0)),
                  pl.BlockSpec((tm, R), lambda b, i: (i, 0))],
        out_specs=[tok_t(n) if tr else tok(n) for n, _, tr in outs],
        out_shape=[jax.ShapeDtypeStruct((B, n, S) if tr else (B, S, n), dt) for n, dt, tr in outs],
        compiler_params=_params("parallel", "arbitrary"),
        name="inproj",
    )(x, gain.reshape(1, D), sc1, sh, w_all, cos_t, sin_t)


def _gelu_tanh(x):
    return 0.5 * x * (1.0 + jnp.tanh(math.sqrt(2.0 / math.pi) * (x + 0.044715 * (x * x * x))))


def _compress_kernel(r_ref, w1_ref, w2_ref, pe_ref, o_ref, *, transposed):
    R = r_ref[0, 0].astype(BF16)
    nb, half = R.shape
    w1 = w1_ref[...].astype(BF16)
    top = jnp.dot(R, w1[0:half], preferred_element_type=F32)
    bot = jnp.dot(R, w1[half:2 * half], preferred_element_type=F32)
    pe_term = jnp.dot(pe_ref[...].astype(BF16), w1, preferred_element_type=F32)[0:1]
    bot_next = pltpu.roll(bot, nb - 1, 0)
    hid = _gelu_tanh(top + bot_next + pe_term)
    if transposed:
        out = jnp.dot(w2_ref[...].astype(BF16), hid.T.astype(BF16), preferred_element_type=F32)
    else:
        out = jnp.dot(hid.astype(BF16), w2_ref[...].astype(BF16), preferred_element_type=F32)
    o_ref[0, 0] = out.astype(BF16)


def _compress(rr, w1, w2, pe, transposed):
    B, G, nb, half = rr.shape
    hid = w1.shape[-1]
    oshape = (HEAD_DIM, nb) if transposed else (nb, HEAD_DIM)
    return pl.pallas_call(
        functools.partial(_compress_kernel, transposed=transposed),
        grid=(B, G),
        in_specs=[pl.BlockSpec((1, 1, nb, half), lambda b, g: (b, g, 0, 0)),
                  pl.BlockSpec((2 * half, hid), lambda b, g: (0, 0)),
                  pl.BlockSpec(w2.shape, lambda b, g: (0, 0)),
                  pl.BlockSpec((8, 2 * half), lambda b, g: (0, 0))],
        out_specs=pl.BlockSpec((1, 1) + oshape, lambda b, g: (b, g, 0, 0)),
        out_shape=jax.ShapeDtypeStruct((B, G) + oshape, BF16),
        compiler_params=_params("arbitrary", "arbitrary"),
        name="compress_v" if transposed else "compress_k",
    )(rr, w1, w2, pe)


def _lanes4(a):
    return jnp.concatenate([a, a, a, a], axis=1)


def _nsa_kernel(qt_ref, ksw_ref, vt_ref, kcmp_ref, vcmpt_ref, ovt_ref, glt_ref, o_ref, selb_ref, *, seq_len):
    QB, Dh, HPG = Q_BLOCK, HEAD_DIM, NSA_HPG
    qi = pl.program_id(1)
    s0 = qi * QB
    ns = seq_len // SLC_BLOCK
    ncp = seq_len // CMP_STRIDE
    n_sel = min(SLC_TOPK, ns)
    kt_slc = min(SLC_KT, seq_len)
    t_row = s0 + lax.broadcasted_iota(I32, (1, QB), 1)
    t4 = _lanes4(t_row)
    gates = jax.nn.sigmoid(glt_ref[0])
    blk = lax.broadcasted_iota(I32, (ns, QB), 0)
    cur = lax.shift_right_logical(t_row, 6)
    forced = (blk == 0) | (blk == cur) | (blk == cur - 1)
    future = blk * SLC_BLOCK > t_row
    cmp_last = lax.broadcasted_iota(I32, (ncp, 1), 0) * CMP_STRIDE + (CMP_BLOCK - 1)
    cmask = cmp_last <= t4
    zeros_q = jnp.zeros((Dh, HPG * QB), BF16)
    G = NSA_KV_GROUPS
    NL = G * HPG * QB
    qpads, o_cmps, imps = [], [], []

    for g in range(G):
        qg = jnp.concatenate(
            [qt_ref[0, (g * HPG + h) * Dh:(g * HPG + h + 1) * Dh, :] for h in range(HPG)], axis=1)
        qpads.append(jnp.concatenate([qg, zeros_q] if g == 0 else [zeros_q, qg], axis=0))
        sc = jnp.where(cmask, jnp.dot(kcmp_ref[0, g], qg, preferred_element_type=F32), NEG_INF)
        m = jnp.max(sc, axis=0, keepdims=True)
        p = jnp.where(cmask, jnp.exp2(sc - m), 0.0)
        l = jnp.sum(p, axis=0, keepdims=True)
        p = p * (1.0 / jnp.maximum(l, 1e-30))
        o_cmps.append(jnp.dot(vcmpt_ref[0, g], p.astype(BF16), preferred_element_type=F32))
        psum = p[:, 0:QB] + p[:, QB:2 * QB] + p[:, 2 * QB:3 * QB] + p[:, 3 * QB:4 * QB]
        p_hi, p_lo = _split2(psum)
        imp = (jnp.dot(ovt_ref[...], p_hi, preferred_element_type=F32)
               + jnp.dot(ovt_ref[...], p_lo, preferred_element_type=F32))
        imps.append(jnp.where(future, NEG_INF, jnp.where(forced, FORCED, imp)))

    vals = jnp.concatenate(imps, axis=1)
    blk2 = jnp.concatenate([blk] * G, axis=1)
    sel = jnp.zeros((ns, G * QB), F32)
    for _ in range(n_sel):
        mx = jnp.max(vals, axis=0, keepdims=True)
        idx = jnp.min(jnp.where(vals == mx, blk2, ns), axis=0, keepdims=True)
        pick = blk2 == idx
        sel = jnp.where(pick, 1.0, sel)
        vals = jnp.where(pick, LOWEST, vals)
    for g in range(G):
        selb_ref[g] = jnp.where(sel[:, g * QB:(g + 1) * QB] > 0.5, 0.0, NEG_INF)

    q_slabs = [jnp.concatenate(qpads, axis=1)[:, sb * FLASH_LANES:(sb + 1) * FLASH_LANES]
               for sb in range(NL // FLASH_LANES)]
    n_slab = len(q_slabs)

    def load_tile(k_lane0, vrow0, k0, width, bias_of_group):
        ones = jnp.ones((ACC_ROWS - Dh, width), BF16)
        kt = ksw_ref[0, pl.ds(k0, width), k_lane0:k_lane0 + LANE]
        vaug = [jnp.concatenate([vt_ref[0, vrow0 + g * Dh:vrow0 + (g + 1) * Dh, pl.ds(k0, width)], ones], axis=0)
                for g in range(G)]
        bias = [bias_of_group(g) for g in range(G)]
        return kt, vaug, bias

    group_lanes = HPG * QB
    groups_per_slab = max(FLASH_LANES // group_lanes, 1)
    lanes_per_part = FLASH_LANES // groups_per_slab

    def flash_tiles(tiles, carry):
        m, acc = list(carry[0]), list(carry[1])
        steps = [(ti, sb) for ti in range(len(tiles)) for sb in range(n_slab)]
        scores = {}
        for i in range(len(steps) + FLASH_DEPTH):
            if i < len(steps):
                ti, sb = steps[i]
                scores[i] = jnp.dot(tiles[ti][0], q_slabs[sb], preferred_element_type=F32)
            j = i - FLASH_DEPTH
            if j >= 0:
                ti, sb = steps[j]
                _, vaug, bias = tiles[ti]
                g0 = sb * FLASH_LANES // group_lanes
                slab_bias = jnp.concatenate(
                    [bias[g0 + p] for p in range(groups_per_slab) for _ in range(lanes_per_part // QB)], axis=1)
                s = scores.pop(j) + slab_bias
                m_new = jnp.maximum(m[sb], jnp.max(s, axis=0, keepdims=True))
                alpha = jnp.exp2(m[sb] - m_new)
                pj = jnp.exp2(s - m_new).astype(BF16)
                pv = [jnp.dot(vaug[g0 + p], pj[:, p * lanes_per_part:(p + 1) * lanes_per_part],
                              preferred_element_type=F32) for p in range(groups_per_slab)]
                acc[sb] = alpha * acc[sb] + (pv[0] if len(pv) == 1 else jnp.concatenate(pv, axis=1))
                m[sb] = m_new
        return tuple(m), tuple(acc)

    def slc_tile(j):
        k0 = pl.multiple_of(j * kt_slc, kt_slc)
        nblk = kt_slc // SLC_BLOCK
        causal = k0 + lax.broadcasted_iota(I32, (kt_slc, 1), 0) <= t_row

        def bias(g):
            blk_bias = jnp.concatenate(
                [jnp.broadcast_to(selb_ref[g, pl.ds(j * nblk + i, 1), :], (SLC_BLOCK, QB)) for i in range(nblk)],
                axis=0)
            return jnp.where(causal, blk_bias, NEG_INF)

        return load_tile(0, 0, k0, kt_slc, bias)

    n_all = seq_len // kt_slc
    unroll = math.gcd(SLC_UNROLL, n_all)

    def slc_body(jj, carry):
        return flash_tiles([slc_tile(jj * unroll + u) for u in range(unroll)], carry)

    init = (tuple(jnp.full((1, FLASH_LANES), NEG_INF, F32) for _ in range(n_slab)),
            tuple(jnp.zeros((ACC_ROWS, FLASH_LANES), F32) for _ in range(n_slab)))
    n_tiles = (s0 + QB - 1) // kt_slc + 1
    _, acc_s = lax.fori_loop(0, (n_tiles + unroll - 1) // unroll, slc_body, init)
    acc_s = jnp.concatenate(acc_s, axis=1)
    o_slc = acc_s[0:Dh] * (1.0 / acc_s[Dh:Dh + 1])

    win_tiles = []
    for jj in range(min(WINDOW // QB + 1, seq_len // QB)):
        tile = qi - jj
        k0 = pl.multiple_of(jnp.maximum(tile, 0) * QB, QB)
        kpos = k0 + lax.broadcasted_iota(I32, (QB, 1), 0)
        wbias = jnp.where((kpos <= t_row) & (kpos > t_row - WINDOW) & (tile >= 0), 0.0, NEG_INF)
        win_tiles.append(load_tile(LANE, LANE, k0, QB, lambda g, wb=wbias: wb))
    _, acc_w = flash_tiles(win_tiles, init)
    acc_w = jnp.concatenate(acc_w, axis=1)
    o_win = acc_w[0:Dh] * (1.0 / acc_w[Dh:Dh + 1])

    for g in range(G):
        for h in range(HPG):
            c0 = (g * HPG + h) * 3
            cols = slice((g * HPG + h) * QB, (g * HPG + h + 1) * QB)
            o = (gates[c0:c0 + 1, :] * o_cmps[g][:, h * QB:(h + 1) * QB] + gates[c0 + 1:c0 + 2, :] * o_slc[:, cols]
                 + gates[c0 + 2:c0 + 3, :] * o_win[:, cols])
            o_ref[0, (g * HPG + h) * Dh:(g * HPG + h + 1) * Dh, :] = o


def _nsa(qt, ksw, vt, kcmp, vcmpt, overlap_t, glt):
    B, _, S = qt.shape
    ncp = S // CMP_STRIDE
    ns = S // SLC_BLOCK
    G = NSA_KV_GROUPS
    return pl.pallas_call(
        functools.partial(_nsa_kernel, seq_len=S),
        grid=(B, S // Q_BLOCK),
        in_specs=[pl.BlockSpec((1, NSA_WIDTH, Q_BLOCK), lambda b, i: (b, 0, i)),
                  pl.BlockSpec((1, S, 2 * LANE), lambda b, i: (b, 0, 0)),
                  pl.BlockSpec((1, 2 * LANE, S), lambda b, i: (b, 0, 0)),
                  pl.BlockSpec((1, G, ncp, HEAD_DIM), lambda b, i: (b, 0, 0, 0)),
                  pl.BlockSpec((1, G, HEAD_DIM, ncp), lambda b, i: (b, 0, 0, 0)),
                  pl.BlockSpec((ns, ncp), lambda b, i: (0, 0)),
                  pl.BlockSpec((1, LANE, Q_BLOCK), lambda b, i: (b, 0, i))],
        out_specs=pl.BlockSpec((1, NSA_WIDTH, Q_BLOCK), lambda b, i: (b, 0, i)),
        out_shape=jax.ShapeDtypeStruct((B, NSA_WIDTH, S), F32),
        scratch_shapes=[pltpu.VMEM((G, ns, Q_BLOCK), F32)],
        compiler_params=_params("parallel", "arbitrary"),
        name="nsa",
    )(qt, ksw, vt, kcmp, vcmpt, overlap_t, glt)


def _rwkv_prep_kernel(f_ref, p_ref, mu_ref, w0_ref, w2_ref, a0_ref, a2_ref, g2_ref, kk_ref, ka_ref,
                      bd_ref, r_o, lw_o, k_o, v_o, kk_o, a_o, g_o):
    W = RWKV_WIDTH
    i = pl.program_id(1)
    f = f_ref[0]
    tm = f.shape[0]
    prev_last = jnp.where(i > 0, p_ref[0, 7:8, :], 0.0)
    rolled = pltpu.roll(f, 1, 0)
    row = lax.broadcasted_iota(I32, (tm, 1), 0)
    prev = jnp.where(row == 0, prev_last, rolled)
    f = f + (prev - f) * mu_ref[...]
    r, k, v = f[:, 0:W], f[:, W:2 * W], f[:, 2 * W:3 * W]
    wd = f[:, 3 * W:3 * W + LANE]
    ad = f[:, 3 * W + LANE:3 * W + 2 * LANE]
    gd = f[:, 3 * W + 2 * LANE:3 * W + 3 * LANE]
    lw_o[0] = -DECAY_SCALE * jax.nn.sigmoid(w0_ref[...] + _dot3(jnp.tanh(wd), w2_ref[...]))
    a = jax.nn.sigmoid(a0_ref[...] + _dot3(ad, a2_ref[...]))
    g_o[0] = _dot3(jax.nn.sigmoid(gd), g2_ref[...])
    kk = k * kk_ref[...]
    ss = _dot_lhs2(kk * kk, bd_ref[...])
    kk_o[0] = kk / jnp.maximum(jnp.sqrt(ss), 1e-12)
    k_o[0] = k * (1.0 + (a - 1.0) * ka_ref[...])
    r_o[0] = r
    v_o[0] = v
    a_o[0] = a


def _rwkv_prep(rw, mu, w0, w2, a0, a2, g2, k_k, k_a, bd):
    B, S, C = rw.shape
    W = RWKV_WIDTH
    tm = min(256, S)
    row = lambda n: pl.BlockSpec((1, n), lambda b, i: (0, 0))
    mat = pl.BlockSpec((LANE, W), lambda b, i: (0, 0))
    tok = pl.BlockSpec((1, tm, W), lambda b, i: (b, i, 0))
    return pl.pallas_call(
        _rwkv_prep_kernel,
        grid=(B, S // tm),
        in_specs=[pl.BlockSpec((1, tm, C), lambda b, i: (b, i, 0)),
                  pl.BlockSpec((1, 8, C), lambda b, i: (b, jnp.maximum(i * (tm // 8) - 1, 0), 0)),
                  row(C), row(W), mat, row(W), mat, mat, row(W), row(W),
                  pl.BlockSpec((W, W), lambda b, i: (0, 0))],
        out_specs=[tok] * 7,
        out_shape=[jax.ShapeDtypeStruct((B, S, W), F32)] * 7,
        compiler_params=_params("parallel", "arbitrary"),
        name="rwkv_prep",
    )(rw, rw, mu, w0, w2, a0, a2, g2, k_k, k_a, bd)


def _rwkv_scan_kernel(r_ref, lw_ref, k_ref, v_ref, kk_ref, a_ref, g_ref, lnw_ref, lnb_ref, rk_ref,
                      o_ref, st_ref):
    Dh = HEAD_DIM

    @pl.when(pl.program_id(1) == 0)
    def _():
        st_ref[...] = jnp.zeros_like(st_ref)

    NB, C = r_ref.shape[0], r_ref.shape[1]
    ri = lax.broadcasted_iota(I32, (C, C), 0)
    ci = lax.broadcasted_iota(I32, (C, C), 1)
    incl = ri >= ci
    strict = ri > ci
    tri = incl.astype(BF16)
    eye = (ri == ci).astype(F32)
    n_double = max(int(math.log2(C)) - 1, 0)

    def scaled_operands(bb):
        r, lw, k, kk, a = (ref[bb] for ref in (r_ref, lw_ref, k_ref, kk_ref, a_ref))
        l_hi = lw.astype(BF16)
        l_r1 = lw - l_hi.astype(F32)
        l_mid = l_r1.astype(BF16)
        l_lo = (l_r1 - l_mid.astype(F32)).astype(BF16)
        L = (jnp.dot(tri, l_hi, preferred_element_type=F32) + jnp.dot(tri, l_mid, preferred_element_type=F32)
             + jnp.dot(tri, l_lo, preferred_element_type=F32))
        e_l = jnp.exp(L)
        e_inv = jnp.exp(-L)
        e_end = jnp.exp(L[C - 1:C, :] - L)
        b = kk * a
        x1 = jnp.concatenate([kk * jnp.exp(L - lw), r * e_l], axis=0).astype(BF16)
        x2t = jnp.concatenate([k * e_inv, b * e_inv], axis=0).T.astype(BF16)
        zt = jnp.concatenate([k * e_end, -(b * e_end)], axis=0).T.astype(BF16)
        gt = jnp.concatenate([e_l, e_l], axis=0).T
        return x1, x2t, zt, gt

    ops = [scaled_operands(bb) for bb in range(NB)]
    chains = [(bb, h) for bb in range(NB) for h in range(RWKV_HEADS)]
    sls = [slice(h * Dh, (h + 1) * Dh) for _, h in chains]
    h0s = [st_ref[bb, h] for bb, h in chains]
    vhs = [v_ref[bb][:, sl] for (bb, _), sl in zip(chains, sls)]
    x1s = [ops[bb][0][:, sl] for (bb, _), sl in zip(chains, sls)]
    amats = [jnp.dot(x1, ops[bb][1][sl, :], preferred_element_type=F32)
             for x1, (bb, _), sl in zip(x1s, chains, sls)]
    p0s = [jnp.dot(x1, h0.astype(BF16), preferred_element_type=F32) for x1, h0 in zip(x1s, h0s)]
    a_kk = [jnp.where(strict, am[0:C, 0:C], 0.0).astype(BF16) for am in amats]
    a_rkb = [jnp.concatenate([jnp.where(incl, am[C:2 * C, 0:C], 0.0), jnp.where(incl, -am[C:2 * C, C:2 * C], 0.0)],
                             axis=1).astype(BF16) for am in amats]
    nmats = [jnp.where(strict, -am[0:C, C:2 * C], 0.0) for am in amats]
    tinvs = [eye + nm for nm in nmats]
    for _ in range(n_double):
        nmats = [_bdot(nm, nm) for nm in nmats]
        tinvs = [ti + _bdot(ti, nm) for ti, nm in zip(tinvs, nmats)]
    akv = [jnp.dot(ak, vh.astype(BF16), preferred_element_type=F32) for ak, vh in zip(a_kk, vhs)]
    us = [_bdot(ti, p0[0:C] + av) for ti, p0, av in zip(tinvs, p0s, akv)]
    vus = [jnp.concatenate([vh, u], axis=0).astype(BF16) for vh, u in zip(vhs, us)]
    ys = [p0[C:2 * C] + jnp.dot(ar, vu, preferred_element_type=F32) for p0, ar, vu in zip(p0s, a_rkb, vus)]
    for i, (bb, h) in enumerate(chains):
        _, _, zt, gt = ops[bb]
        st_ref[bb, h] = gt[sls[i], C - 1:C] * h0s[i] + jnp.dot(zt[sls[i], :], vus[i], preferred_element_type=F32)
    for i, (bb, h) in enumerate(chains):
        sl, y = sls[i], ys[i]
        mean = jnp.mean(y, axis=-1, keepdims=True)
        yc = y - mean
        var = jnp.mean(yc * yc, axis=-1, keepdims=True)
        yn = yc * lax.rsqrt(var + GN_EPS) * lnw_ref[:, sl] + lnb_ref[:, sl]
        bonus = jnp.sum(r_ref[bb][:, sl] * k_ref[bb][:, sl] * rk_ref[:, sl], axis=-1, keepdims=True) * vhs[i]
        o_ref[bb, :, sl] = (yn + bonus) * g_ref[bb][:, sl]


def _rwkv_scan(r, lw, k, v, kk, a, g, ln_w, ln_b, r_k):
    B, S, W = r.shape
    C = min(RWKV_CHUNK, S)
    nb = math.gcd(RWKV_BATCH_ROWS, B)
    tok = pl.BlockSpec((nb, C, W), lambda b, i: (b, i, 0))
    row = pl.BlockSpec((1, W), lambda b, i: (0, 0))
    return pl.pallas_call(
        _rwkv_scan_kernel,
        grid=(B // nb, S // C),
        in_specs=[tok] * 7 + [row] * 3,
        out_specs=tok,
        out_shape=jax.ShapeDtypeStruct((B, S, W), F32),
        scratch_shapes=[pltpu.VMEM((nb, RWKV_HEADS, HEAD_DIM, HEAD_DIM), F32)],
        compiler_params=_params("parallel", "arbitrary"),
        name="rwkv_scan",
    )(r, lw, k, v, kk, a, g, ln_w, ln_b, r_k)


def _rms(x, g):
    return x * lax.rsqrt(jnp.mean(x * x, axis=-1, keepdims=True) + RMS_EPS) * g


def _outproj_kernel(on_ref, or_ref, x_ref, gn_ref, wt_ref, wb_ref, gpost_ref, gm_ref, gpre_ref,
                    scf_ref, shf_ref, rw_ref, rb_ref, x1_ref, h2_ref, idx_ref, wgt_ref):
    ont = on_ref[0]
    ont = ont * lax.rsqrt(jnp.mean(ont * ont, axis=0, keepdims=True) + RMS_EPS) * gn_ref[...]
    on = ont.T
    mixed = (jnp.dot(on.astype(BF16), wt_ref[...], preferred_element_type=F32)
             + jnp.dot(or_ref[0].astype(BF16), wb_ref[...], preferred_element_type=F32))
    x1 = x_ref[0] + gm_ref[0] * _rms(mixed, gpost_ref[...])
    x1_ref[0] = x1
    h2 = _rms(x1, gpre_ref[...]) * scf_ref[0] + shf_ref[0]
    h2_ref[0] = h2.astype(BF16)
    logits = _dot3(h2, rw_ref[...]) + rb_ref[...]
    lane = lax.broadcasted_iota(I32, logits.shape, 1)
    vals = logits
    top_v, top_i = [], []
    for _ in range(TOP_K):
        mx = jnp.max(vals, axis=-1, keepdims=True)
        ix = jnp.min(jnp.where(vals == mx, lane, LANE), axis=-1, keepdims=True)
        top_v.append(mx)
        top_i.append(ix)
        vals = jnp.where(lane == ix, LOWEST, vals)
    ex = [jnp.exp(tv - top_v[0]) for tv in top_v]
    inv = 1.0 / (ex[0] + ex[1] + ex[2] + ex[3])
    idx = jnp.zeros(logits.shape, I32)
    wgt = jnp.zeros(logits.shape, F32)
    for kk in range(TOP_K):
        idx = jnp.where(lane == kk, top_i[kk], idx)
        wgt = jnp.where(lane == kk, ex[kk] * inv, wgt)
    idx_ref[0] = idx
    wgt_ref[0] = wgt


def _outproj(o_nsa, o_rwkv, x, gn, w_top, w_bot, g_post, g_m, g_pre, sc1_f, sh_f, router_w, router_b):
    B, S, D = x.shape
    tm = min(256, S)
    W = o_rwkv.shape[-1]
    tok = lambda n: pl.BlockSpec((1, tm, n), lambda b, i: (b, i, 0))
    vec = pl.BlockSpec((1, 1, D), lambda b, i: (b, 0, 0))
    row = lambda n: pl.BlockSpec((1, n), lambda b, i: (0, 0))
    return pl.pallas_call(
        _outproj_kernel,
        grid=(B, S // tm),
        in_specs=[pl.BlockSpec((1, W, tm), lambda b, i: (b, 0, i)), tok(W), tok(D),
                  pl.BlockSpec((W, 1), lambda b, i: (0, 0)),
                  pl.BlockSpec((W, D), lambda b, i: (0, 0)), pl.BlockSpec((W, D), lambda b, i: (0, 0)),
                  row(D), vec, row(D), vec, vec,
                  pl.BlockSpec((D, LANE), lambda b, i: (0, 0)), row(LANE)],
        out_specs=[tok(D), tok(D), tok(LANE), tok(LANE)],
        out_shape=[jax.ShapeDtypeStruct((B, S, D), F32), jax.ShapeDtypeStruct((B, S, D), BF16),
                   jax.ShapeDtypeStruct((B, S, LANE), I32), jax.ShapeDtypeStruct((B, S, LANE), F32)],
        compiler_params=_params("parallel", "arbitrary"),
        name="outproj",
    )(o_nsa, o_rwkv, x, gn, w_top, w_bot, g_post, g_m, g_pre, sc1_f, sh_f, router_w, router_b)


def _expert_kernel(be_ref, nu_ref, x_ref, wgu_ref, bgu_ref, wd_ref, bd_ref, o_ref, wgu_bf, wd_bf):
    F = wd_ref.shape[1]
    i = pl.program_id(0)
    active = i < nu_ref[0]
    new_expert = (i == 0) | (be_ref[i] != be_ref[jnp.maximum(i - 1, 0)])

    @pl.when(active & new_expert)
    def _():
        wgu_bf[...] = wgu_ref[0].astype(BF16)
        wd_bf[...] = wd_ref[0].astype(BF16)

    @pl.when(active)
    def _():
        gu = jnp.dot(x_ref[...], wgu_bf[...], preferred_element_type=F32) + bgu_ref[0]
        gate = jnp.minimum(gu[:, 0:F], SWIGLU_LIMIT)
        up = jnp.clip(gu[:, F:2 * F], -SWIGLU_LIMIT, SWIGLU_LIMIT)
        glu = gate * jax.nn.sigmoid(gate * SWIGLU_ALPHA)
        out = jnp.dot(((up + 1.0) * glu).astype(BF16), wd_bf[...], preferred_element_type=F32) + bd_ref[0]
        o_ref[...] = out.astype(o_ref.dtype)

    @pl.when(jnp.logical_not(active))
    def _():
        o_ref[...] = jnp.zeros_like(o_ref)


def _experts(blk_expert, n_used, x_sorted, w_gu, b_gu, w_d, b_d):
    P, D = x_sorted.shape
    E, _, F2 = w_gu.shape
    F = F2 // 2
    nb = P // MOE_BLOCK
    grid_spec = pltpu.PrefetchScalarGridSpec(
        num_scalar_prefetch=2,
        grid=(nb,),
        in_specs=[pl.BlockSpec((MOE_BLOCK, D), lambda i, be, nu: (i, 0)),
                  pl.BlockSpec((1, D, F2), lambda i, be, nu: (be[i], 0, 0)),
                  pl.BlockSpec((1, 1, F2), lambda i, be, nu: (be[i], 0, 0)),
                  pl.BlockSpec((1, F, D), lambda i, be, nu: (be[i], 0, 0)),
                  pl.BlockSpec((1, 1, D), lambda i, be, nu: (be[i], 0, 0))],
        out_specs=pl.BlockSpec((MOE_BLOCK, D), lambda i, be, nu: (i, 0)),
        scratch_shapes=[pltpu.VMEM((D, F2), BF16), pltpu.VMEM((F, D), BF16)],
    )
    return pl.pallas_call(
        _expert_kernel,
        grid_spec=grid_spec,
        out_shape=jax.ShapeDtypeStruct((P, D), BF16),
        compiler_params=_params("arbitrary"),
        name="experts",
    )(blk_expert, n_used, x_sorted, w_gu, b_gu.reshape(E, 1, F2), w_d, b_d.reshape(E, 1, D))


def _combine_kernel(y_ref, w_ref, x1_ref, gpost_ref, gf_ref, o_ref):
    w = w_ref[0]
    y = (w[:, 0:1] * y_ref[0].astype(F32) + w[:, 1:2] * y_ref[1].astype(F32)
         + w[:, 2:3] * y_ref[2].astype(F32) + w[:, 3:4] * y_ref[3].astype(F32))
    o_ref[0] = x1_ref[0] + gf_ref[0] * _rms(y, gpost_ref[...])


def _combine(y4, wgt, x1, g_post, g_f):
    B, S, D = x1.shape
    tm = min(256, S)
    nt = S // tm
    return pl.pallas_call(
        _combine_kernel,
        grid=(B, nt),
        in_specs=[pl.BlockSpec((TOP_K, tm, D), lambda b, i: (0, b * nt + i, 0)),
                  pl.BlockSpec((1, tm, LANE), lambda b, i: (b, i, 0)),
                  pl.BlockSpec((1, tm, D), lambda b, i: (b, i, 0)),
                  pl.BlockSpec((1, D), lambda b, i: (0, 0)),
                  pl.BlockSpec((1, 1, D), lambda b, i: (b, 0, 0))],
        out_specs=pl.BlockSpec((1, tm, D), lambda b, i: (b, i, 0)),
        out_shape=jax.ShapeDtypeStruct((B, S, D), F32),
        compiler_params=_params("parallel", "arbitrary"),
        name="combine",
    )(y4, wgt, x1, g_post, g_f)


def _rot_cols(w):
    d, n = w.shape
    w4 = w.reshape(d, n // HEAD_DIM, 2, HEAD_DIM // 2)
    return jnp.concatenate([-w4[:, :, 1:2], w4[:, :, 0:1]], axis=2).reshape(d, n)


def _pad_cols(w, n):
    return jnp.pad(w, ((0, 0), (0, n - w.shape[1])))


def _pad_rows(w, n):
    return jnp.pad(w, ((0, n - w.shape[0]), (0, 0)))


def _layer(x, ada, l, mix_pre_norm, mix_post_norm, ffn_pre_norm, ffn_post_norm, w_in, cmp_k_pe, cmp_k_w1,
           cmp_k_w2, cmp_v_pe, cmp_v_w1, cmp_v_w2, nsa_out_norm, rwkv_mu, rwkv_w0, rwkv_w2, rwkv_a0, rwkv_a2,
           rwkv_g2, rwkv_k_k, rwkv_k_a, rwkv_r_k, rwkv_ln_w, rwkv_ln_b, w_out, router_w, router_b,
           expert_w_gate_up, expert_b_gate_up, expert_w_down, expert_b_down):
    B, S, D = x.shape
    T = B * S
    G, Dh, W = NSA_KV_GROUPS, HEAD_DIM, RWKV_WIDTH
    sh_m, sc_m, g_m, sh_f, sc_f, g_f = [a.reshape(B, 1, D) for a in jnp.split(ada, 6, axis=-1)]

    wi = w_in[l]
    KV = NSA_KV_WIDTH
    o = NSA_WIDTH
    wq, wkc, wvc, wks, wvs, wkw, wvw = (wi[:, 0:o],) + tuple(wi[:, o + j * KV:o + (j + 1) * KV] for j in range(6))
    wgl = wi[:, o + 6 * KV:o + 6 * KV + 3 * NSA_HEADS]
    wr = wi[:, o + 6 * KV + 3 * NSA_HEADS:]
    w_rope = jnp.concatenate([wq, wkc, wks, wkw], axis=1)
    w_rest = _pad_cols(jnp.concatenate([wvc, wvs, wvw, wgl], axis=1), REST_COLS)
    lora = [(3 * W, RWKV_W_LORA), (3 * W + RWKV_W_LORA, RWKV_A_LORA),
            (3 * W + RWKV_W_LORA + RWKV_A_LORA, RWKV_G_LORA)]
    pad_lora = lambda a: jnp.concatenate([_pad_cols(a[..., s:s + n], LANE) for s, n in lora], axis=-1)
    w_rw = jnp.concatenate([wr[:, 0:3 * W], pad_lora(wr)], axis=1)
    w_all = jnp.concatenate([w_rope, _rot_cols(w_rope), w_rest, w_rw], axis=1).astype(BF16)
    mu_row = rwkv_mu[l].reshape(1, -1)
    mu = jnp.concatenate([mu_row[:, 0:3 * W], pad_lora(mu_row)], axis=1)

    half = Dh // 2
    inv_freq = ROPE_THETA ** (-jnp.arange(half, dtype=F32) / half)
    ang = jnp.arange(S, dtype=F32)[:, None] * inv_freq[None, :]
    n_rope_heads = ROPE_COLS // Dh
    col_scale = jnp.where(jnp.arange(ROPE_COLS) < NSA_WIDTH, Dh ** -0.5 * math.log2(math.e), 1.0).astype(F32)
    cos_t = jnp.tile(jnp.cos(ang), (1, 2 * n_rope_heads)) * col_scale
    sin_t = jnp.tile(jnp.sin(ang), (1, 2 * n_rope_heads)) * col_scale

    qt, kc, ksw, vc, vt, glt, rw = _inproj(x, mix_pre_norm[l], 1.0 + sc_m, sh_m, w_all, cos_t, sin_t)

    nb = S // CMP_STRIDE
    regroup = lambda t: t.reshape(B, nb, CMP_STRIDE, G, Dh).transpose(0, 3, 1, 2, 4).reshape(B, G, nb, CMP_STRIDE * Dh)
    pe8 = lambda pe: jnp.broadcast_to(pe.reshape(1, -1), (8, pe.size))
    kcmp = _compress(regroup(kc), cmp_k_w1[l], cmp_k_w2[l], pe8(cmp_k_pe[l]), transposed=False)
    vcmpt = _compress(regroup(vc), cmp_v_w1[l], cmp_v_w2[l].T, pe8(cmp_v_pe[l]), transposed=True)
    ns = S // SLC_BLOCK
    c0 = jnp.arange(nb)[None, :] * CMP_STRIDE
    b0 = jnp.arange(ns)[:, None] * SLC_BLOCK
    ov = jnp.maximum(jnp.minimum(c0 + CMP_BLOCK, b0 + SLC_BLOCK) - jnp.maximum(c0, b0), 0)
    overlap_t = (ov.astype(F32) / CMP_BLOCK).astype(BF16)
    o_nsa_t = _nsa(qt, ksw, vt, kcmp, vcmpt, overlap_t, glt)

    bd = (jnp.arange(W)[:, None] // Dh == jnp.arange(W)[None, :] // Dh).astype(BF16)
    row = lambda a: a.reshape(1, -1)
    r, lw, k, v, kk, a, g = _rwkv_prep(
        rw, mu, row(rwkv_w0[l]), _pad_rows(rwkv_w2[l], LANE), row(rwkv_a0[l]), _pad_rows(rwkv_a2[l], LANE),
        _pad_rows(rwkv_g2[l], LANE), row(rwkv_k_k[l]), row(rwkv_k_a[l]), bd)
    o_rwkv = _rwkv_scan(r, lw, k, v, kk, a, g, row(rwkv_ln_w[l]), row(rwkv_ln_b[l]), row(rwkv_r_k[l]))

    wo = w_out[l].astype(BF16)
    rb = jnp.concatenate([router_b[l], jnp.full((LANE - N_EXPERTS,), NEG_INF, F32)]).reshape(1, LANE)
    x1, h2, idx, wgt = _outproj(o_nsa_t, o_rwkv, x, nsa_out_norm[l].reshape(-1, 1), wo[0:NSA_WIDTH], wo[NSA_WIDTH:],
                                row(mix_post_norm[l]), g_m, row(ffn_pre_norm[l]), 1.0 + sc_f, sh_f,
                                _pad_cols(router_w[l], LANE), rb)

    n_assign = T * TOP_K
    idx4 = idx.reshape(T, LANE)[:, 0:TOP_K]
    e_ids = jnp.arange(N_EXPERTS, dtype=I32)
    hot = idx4[:, :, None] == e_ids
    onehot = hot.astype(I32).sum(axis=1)
    csum = jnp.cumsum(onehot, axis=0)
    counts = csum[-1]
    starts = jnp.cumsum(counts) - counts
    padded = ((counts + MOE_BLOCK - 1) // MOE_BLOCK) * MOE_BLOCK
    pad_ends = jnp.cumsum(padded)
    pad_starts = pad_ends - padded
    pos = jnp.where(hot, (csum - onehot + pad_starts)[:, None, :], 0).sum(axis=-1).astype(I32)
    n_blocks = -(-n_assign // MOE_BLOCK) + N_EXPERTS
    P = n_blocks * MOE_BLOCK
    blk_start = jnp.arange(n_blocks, dtype=I32) * MOE_BLOCK
    blk_expert = jnp.minimum(jnp.searchsorted(pad_ends, blk_start, side='right'), N_EXPERTS - 1).astype(I32)
    n_used = (pad_ends[-1] // MOE_BLOCK).astype(I32).reshape(1)
    order = jnp.argsort(idx4.reshape(-1))
    blk_off = blk_start - pad_starts[blk_expert]
    tok_sorted = jnp.pad((order // TOP_K).astype(I32), (0, MOE_BLOCK))
    run_start = jnp.minimum(starts[blk_expert] + blk_off, n_assign)
    runs = jax.vmap(lambda s: lax.dynamic_slice(tok_sorted, (s,), (MOE_BLOCK,)))(run_start)
    valid = jnp.arange(MOE_BLOCK, dtype=I32)[None, :] < (counts[blk_expert] - blk_off)[:, None]
    slot_tok = jnp.where(valid, runs, 0).reshape(P)

    x_sorted = h2.reshape(T, D)[slot_tok]
    y_sorted = _experts(blk_expert, n_used, x_sorted, expert_w_gate_up[l], expert_b_gate_up[l],
                        expert_w_down[l], expert_b_down[l])
    y4 = y_sorted[pos.T]
    return _combine(y4, wgt, x1, row(ffn_post_norm[l]), g_f)


def kernel(x, c, ada_w, ada_b, mix_pre_norm, mix_post_norm, ffn_pre_norm, ffn_post_norm, w_in, cmp_k_pe, cmp_k_w1, cmp_k_w2, cmp_v_pe, cmp_v_w1, cmp_v_w2, nsa_out_norm, rwkv_mu, rwkv_w0, rwkv_w2, rwkv_a0, rwkv_a2, rwkv_g2, rwkv_k_k, rwkv_k_a, rwkv_r_k, rwkv_ln_w, rwkv_ln_b, w_out, router_w, router_b, expert_w_gate_up, expert_b_gate_up, expert_w_down, expert_b_down):
    for l in range(ada_w.shape[0]):
        ada = _ada(c, ada_w[l], ada_b[l])
        x = _layer(x, ada, l, mix_pre_norm, mix_post_norm, ffn_pre_norm, ffn_post_norm, w_in, cmp_k_pe,
                   cmp_k_w1, cmp_k_w2, cmp_v_pe, cmp_v_w1, cmp_v_w2, nsa_out_norm, rwkv_mu, rwkv_w0, rwkv_w2,
                   rwkv_a0, rwkv_a2, rwkv_g2, rwkv_k_k, rwkv_k_a, rwkv_r_k, rwkv_ln_w, rwkv_ln_b, w_out,
                   router_w, router_b, expert_w_gate_up, expert_b_gate_up, expert_w_down, expert_b_down)
    return x
```

```python
import functools
import math

import jax
import jax.numpy as jnp
from jax import lax
from jax.experimental import pallas as pl
from jax.experimental.pallas import tpu as pltpu

F32 = jnp.float32
BF16 = jnp.bfloat16
I32 = jnp.int32

HEAD_DIM = 64
NSA_HEADS = 8
NSA_KV_GROUPS = 2
NSA_HPG = NSA_HEADS // NSA_KV_GROUPS
NSA_WIDTH = NSA_HEADS * HEAD_DIM
NSA_KV_WIDTH = NSA_KV_GROUPS * HEAD_DIM
CMP_BLOCK = 32
CMP_STRIDE = 16
SLC_BLOCK = 64
SLC_TOPK = 16
WINDOW = 512
Q_BLOCK = 128
RWKV_HEADS = 8
RWKV_WIDTH = RWKV_HEADS * HEAD_DIM
RWKV_W_LORA = 32
RWKV_A_LORA = 32
RWKV_G_LORA = 96
DECAY_SCALE = math.exp(-0.5)
GN_EPS = 64e-5
N_EXPERTS = 32
TOP_K = 4
SWIGLU_LIMIT = 7.0
SWIGLU_ALPHA = 1.702
ROPE_THETA = 10000.0
RMS_EPS = 1e-6
NEG_INF = -1e30
FORCED = 1e9
LOWEST = -3e38

LANE = 128
ROPE_COLS = NSA_WIDTH + 3 * NSA_KV_WIDTH
REST_COLS = 4 * LANE
RW_COLS = 3 * RWKV_WIDTH + 3 * LANE
SLC_KT = 512
SLC_UNROLL = 1
FLASH_LANES = 1024
FLASH_DEPTH = 1
ACC_ROWS = HEAD_DIM + 16
RWKV_CHUNK = 128
RWKV_BATCH_ROWS = 1
MOE_BLOCK = 512
VMEM_LIMIT = 56 * 1024 * 1024


def _bdot(a, b):
    return jnp.dot(a.astype(BF16), b.astype(BF16), preferred_element_type=F32)


def _split2(a):
    hi = a.astype(BF16)
    lo = (a - hi.astype(F32)).astype(BF16)
    return hi, lo


def _dot_lhs2(a, b_bf16):
    hi, lo = _split2(a)
    return (jnp.dot(hi, b_bf16, preferred_element_type=F32)
            + jnp.dot(lo, b_bf16, preferred_element_type=F32))


def _dot3(a, b):
    ah, al = _split2(a)
    bh, bl = _split2(b)
    return (jnp.dot(ah, bh, preferred_element_type=F32)
            + jnp.dot(al, bh, preferred_element_type=F32)
            + jnp.dot(ah, bl, preferred_element_type=F32))


def _params(*sem, flags=None):
    return pltpu.CompilerParams(dimension_semantics=sem, vmem_limit_bytes=VMEM_LIMIT, flags=flags)


def _ada_kernel(c_ref, w_ref, b_ref, o_ref):
    c = c_ref[...]
    act = c * jax.nn.sigmoid(c)
    o_ref[...] = _dot3(act, w_ref[...]) + b_ref[...]


def _ada(c, w, b):
    B, D = c.shape
    N = w.shape[1]
    tn = 1536
    return pl.pallas_call(
        _ada_kernel,
        grid=(N // tn,),
        in_specs=[pl.BlockSpec((B, D), lambda j: (0, 0)),
                  pl.BlockSpec((D, tn), lambda j: (0, j)),
                  pl.BlockSpec((1, tn), lambda j: (0, j))],
        out_specs=pl.BlockSpec((B, tn), lambda j: (0, j)),
        out_shape=jax.ShapeDtypeStruct((B, N), F32),
        compiler_params=_params("arbitrary"),
        name="ada",
    )(c, w, b.reshape(1, N))


def _inproj_kernel(x_ref, g_ref, sc_ref, sh_ref, w_ref, cos_ref, sin_ref,
                   q_ref, kc_ref, ksw_ref, vc_ref, vsw_ref, gl_ref, rw_ref):
    x = x_ref[0]
    ms = jnp.mean(x * x, axis=-1, keepdims=True)
    h = x * lax.rsqrt(ms + RMS_EPS) * g_ref[...]
    h = h * sc_ref[0] + sh_ref[0]
    hb = h.astype(BF16)
    R = ROPE_COLS
    main = jnp.dot(hb, w_ref[:, 0:R], preferred_element_type=F32)
    rot = jnp.dot(hb, w_ref[:, R:2 * R], preferred_element_type=F32)
    roped = main * cos_ref[...] + rot * sin_ref[...]
    q_ref[0] = roped[:, 0:NSA_WIDTH].T.astype(BF16)
    kc_ref[0] = roped[:, NSA_WIDTH:NSA_WIDTH + LANE]
    ksw_ref[0] = roped[:, NSA_WIDTH + LANE:R].astype(BF16)
    rest = jnp.dot(hb, w_ref[:, 2 * R:2 * R + REST_COLS], preferred_element_type=F32)
    vc_ref[0] = rest[:, 0:LANE]
    vsw_ref[0] = rest[:, LANE:3 * LANE].T.astype(BF16)
    gl_ref[0] = rest[:, 3 * LANE:4 * LANE].T
    rw_ref[0] = jnp.dot(hb, w_ref[:, 2 * R + REST_COLS:], preferred_element_type=F32)


def _inproj(x, gain, sc1, sh, w_all, cos_t, sin_t):
    B, S, D = x.shape
    tm = min(256, S)
    NW = w_all.shape[1]
    R = ROPE_COLS
    tok = lambda n: pl.BlockSpec((1, tm, n), lambda b, i: (b, i, 0))
    tok_t = lambda n: pl.BlockSpec((1, n, tm), lambda b, i: (b, 0, i))
    vec = pl.BlockSpec((1, 1, D), lambda b, i: (b, 0, 0))
    outs = [(NSA_WIDTH, BF16, True), (LANE, F32, False), (2 * LANE, BF16, False), (LANE, F32, False),
            (2 * LANE, BF16, True), (LANE, F32, True), (RW_COLS, F32, False)]
    return pl.pallas_call(
        _inproj_kernel,
        grid=(B, S // tm),
        in_specs=[tok(D), pl.BlockSpec((1, D), lambda b, i: (0, 0)), vec, vec,
                  pl.BlockSpec((D, NW), lambda b, i: (0, 0)),
                  pl.BlockSpec((tm, R), lambda b, i: (i, 0)),
                  pl.BlockSpec((tm, R), lambda b, i: (i, 0))],
        out_specs=[tok_t(n) if tr else tok(n) for n, _, tr in outs],
        out_shape=[jax.ShapeDtypeStruct((B, n, S) if tr else (B, S, n), dt) for n, dt, tr in outs],
        compiler_params=_params("parallel", "arbitrary"),
        name="inproj",
    )(x, gain.reshape(1, D), sc1, sh, w_all, cos_t, sin_t)


def _gelu_tanh(x):
    return 0.5 * x * (1.0 + jnp.tanh(math.sqrt(2.0 / math.pi) * (x + 0.044715 * (x * x * x))))


def _compress_kernel(r_ref, w1_ref, w2_ref, pe_ref, o_ref, *, transposed):
    R = r_ref[0, 0].astype(BF16)
    nb, half = R.shape
    w1 = w1_ref[...].astype(BF16)
    top = jnp.dot(R, w1[0:half], preferred_element_type=F32)
    bot = jnp.dot(R, w1[half:2 * half], preferred_element_type=F32)
    pe_term = jnp.dot(pe_ref[...].astype(BF16), w1, preferred_element_type=F32)[0:1]
    bot_next = pltpu.roll(bot, nb - 1, 0)
    hid = _gelu_tanh(top + bot_next + pe_term)
    if transposed:
        out = jnp.dot(w2_ref[...].astype(BF16), hid.T.astype(BF16), preferred_element_type=F32)
    else:
        out = jnp.dot(hid.astype(BF16), w2_ref[...].astype(BF16), preferred_element_type=F32)
    o_ref[0, 0] = out.astype(BF16)


def _compress(rr, w1, w2, pe, transposed):
    B, G, nb, half = rr.shape
    hid = w1.shape[-1]
    oshape = (HEAD_DIM, nb) if transposed else (nb, HEAD_DIM)
    return pl.pallas_call(
        functools.partial(_compress_kernel, transposed=transposed),
        grid=(B, G),
        in_specs=[pl.BlockSpec((1, 1, nb, half), lambda b, g: (b, g, 0, 0)),
                  pl.BlockSpec((2 * half, hid), lambda b, g: (0, 0)),
                  pl.BlockSpec(w2.shape, lambda b, g: (0, 0)),
                  pl.BlockSpec((8, 2 * half), lambda b, g: (0, 0))],
        out_specs=pl.BlockSpec((1, 1) + oshape, lambda b, g: (b, g, 0, 0)),
        out_shape=jax.ShapeDtypeStruct((B, G) + oshape, BF16),
        compiler_params=_params("arbitrary", "arbitrary"),
        name="compress_v" if transposed else "compress_k",
    )(rr, w1, w2, pe)


def _lanes4(a):
    return jnp.concatenate([a, a, a, a], axis=1)


def _nsa_kernel(qt_ref, ksw_ref, vt_ref, kcmp_ref, vcmpt_ref, ovt_ref, glt_ref, o_ref, selb_ref, *, seq_len):
    QB, Dh, HPG = Q_BLOCK, HEAD_DIM, NSA_HPG
    qi = pl.program_id(1)
    s0 = qi * QB
    ns = seq_len // SLC_BLOCK
    ncp = seq_len // CMP_STRIDE
    n_sel = min(SLC_TOPK, ns)
    kt_slc = min(SLC_KT, seq_len)
    t_row = s0 + lax.broadcasted_iota(I32, (1, QB), 1)
    t4 = _lanes4(t_row)
    gates = jax.nn.sigmoid(glt_ref[0])
    blk = lax.broadcasted_iota(I32, (ns, QB), 0)
    cur = lax.shift_right_logical(t_row, 6)
    forced = (blk == 0) | (blk == cur) | (blk == cur - 1)
    future = blk * SLC_BLOCK > t_row
    cmp_last = lax.broadcasted_iota(I32, (ncp, 1), 0) * CMP_STRIDE + (CMP_BLOCK - 1)
    cmask = cmp_last <= t4
    zeros_q = jnp.zeros((Dh, HPG * QB), BF16)
    G = NSA_KV_GROUPS
    NL = G * HPG * QB
    qpads, o_cmps, imps = [], [], []

    for g in range(G):
        qg = jnp.concatenate(
            [qt_ref[0, (g * HPG + h) * Dh:(g * HPG + h + 1) * Dh, :] for h in range(HPG)], axis=1)
        qpads.append(jnp.concatenate([qg, zeros_q] if g == 0 else [zeros_q, qg], axis=0))
        sc = jnp.where(cmask, jnp.dot(kcmp_ref[0, g], qg, preferred_element_type=F32), NEG_INF)
        m = jnp.max(sc, axis=0, keepdims=True)
        p = jnp.where(cmask, jnp.exp2(sc - m), 0.0)
        l = jnp.sum(p, axis=0, keepdims=True)
        p = p * (1.0 / jnp.maximum(l, 1e-30))
        o_cmps.append(jnp.dot(vcmpt_ref[0, g], p.astype(BF16), preferred_element_type=F32))
        psum = p[:, 0:QB] + p[:, QB:2 * QB] + p[:, 2 * QB:3 * QB] + p[:, 3 * QB:4 * QB]
        p_hi, p_lo = _split2(psum)
        imp = (jnp.dot(ovt_ref[...], p_hi, preferred_element_type=F32)
               + jnp.dot(ovt_ref[...], p_lo, preferred_element_type=F32))
        imps.append(jnp.where(future, NEG_INF, jnp.where(forced, FORCED, imp)))

    vals = jnp.concatenate(imps, axis=1)
    blk2 = jnp.concatenate([blk] * G, axis=1)
    sel = jnp.zeros((ns, G * QB), F32)
    for _ in range(n_sel):
        mx = jnp.max(vals, axis=0, keepdims=True)
        idx = jnp.min(jnp.where(vals == mx, blk2, ns), axis=0, keepdims=True)
        pick = blk2 == idx
        sel = jnp.where(pick, 1.0, sel)
        vals = jnp.where(pick, LOWEST, vals)
    for g in range(G):
        selb_ref[g] = jnp.where(sel[:, g * QB:(g + 1) * QB] > 0.5, 0.0, NEG_INF)

    q_slabs = [jnp.concatenate(qpads, axis=1)[:, sb * FLASH_LANES:(sb + 1) * FLASH_LANES]
               for sb in range(NL // FLASH_LANES)]
    n_slab = len(q_slabs)

    def load_tile(k_lane0, vrow0, k0, width, bias_of_group):
        ones = jnp.ones((ACC_ROWS - Dh, width), BF16)
        kt = ksw_ref[0, pl.ds(k0, width), k_lane0:k_lane0 + LANE]
        vaug = [jnp.concatenate([vt_ref[0, vrow0 + g * Dh:vrow0 + (g + 1) * Dh, pl.ds(k0, width)], ones], axis=0)
                for g in range(G)]
        bias = [bias_of_group(g) for g in range(G)]
        return kt, vaug, bias

    group_lanes = HPG * QB
    groups_per_slab = max(FLASH_LANES // group_lanes, 1)
    lanes_per_part = FLASH_LANES // groups_per_slab

    def flash_tiles(tiles, carry):
        m, acc = list(carry[0]), list(carry[1])
        steps = [(ti, sb) for ti in range(len(tiles)) for sb in range(n_slab)]
        scores = {}
        for i in range(len(steps) + FLASH_DEPTH):
            if i < len(steps):
                ti, sb = steps[i]
                scores[i] = jnp.dot(tiles[ti][0], q_slabs[sb], preferred_element_type=F32)
            j = i - FLASH_DEPTH
            if j >= 0:
                ti, sb = steps[j]
                _, vaug, bias = tiles[ti]
                g0 = sb * FLASH_LANES // group_lanes
                slab_bias = jnp.concatenate(
                    [bias[g0 + p] for p in range(groups_per_slab) for _ in range(lanes_per_part // QB)], axis=1)
                s = scores.pop(j) + slab_bias
                m_new = jnp.maximum(m[sb], jnp.max(s, axis=0, keepdims=True))
                alpha = jnp.exp2(m[sb] - m_new)
                pj = jnp.exp2(s - m_new).astype(BF16)
                pv = [jnp.dot(vaug[g0 + p], pj[:, p * lanes_per_part:(p + 1) * lanes_per_part],
                              preferred_element_type=F32) for p in range(groups_per_slab)]
                acc[sb] = alpha * acc[sb] + (pv[0] if len(pv) == 1 else jnp.concatenate(pv, axis=1))
                m[sb] = m_new
        return tuple(m), tuple(acc)

    def slc_tile(j):
        k0 = pl.multiple_of(j * kt_slc, kt_slc)
        nblk = kt_slc // SLC_BLOCK
        causal = k0 + lax.broadcasted_iota(I32, (kt_slc, 1), 0) <= t_row

        def bias(g):
            blk_bias = jnp.concatenate(
                [jnp.broadcast_to(selb_ref[g, pl.ds(j * nblk + i, 1), :], (SLC_BLOCK, QB)) for i in range(nblk)],
                axis=0)
            return jnp.where(causal, blk_bias, NEG_INF)

        return load_tile(0, 0, k0, kt_slc, bias)

    n_all = seq_len // kt_slc
    unroll = math.gcd(SLC_UNROLL, n_all)

    def slc_body(jj, carry):
        return flash_tiles([slc_tile(jj * unroll + u) for u in range(unroll)], carry)

    init = (tuple(jnp.full((1, FLASH_LANES), NEG_INF, F32) for _ in range(n_slab)),
            tuple(jnp.zeros((ACC_ROWS, FLASH_LANES), F32) for _ in range(n_slab)))
    n_tiles = (s0 + QB - 1) // kt_slc + 1
    _, acc_s = lax.fori_loop(0, (n_tiles + unroll - 1) // unroll, slc_body, init)
    acc_s = jnp.concatenate(acc_s, axis=1)
    o_slc = acc_s[0:Dh] * (1.0 / acc_s[Dh:Dh + 1])

    win_keys = min(WINDOW + QB, seq_len)
    k0 = pl.multiple_of(jnp.maximum(qi - WINDOW // QB, 0) * QB, QB)
    kpos = k0 + lax.broadcasted_iota(I32, (win_keys, 1), 0)
    wbias = jnp.where((kpos <= t_row) & (kpos > t_row - WINDOW), 0.0, NEG_INF)
    _, acc_w = flash_tiles([load_tile(LANE, LANE, k0, win_keys, lambda g: wbias)], init)
    acc_w = jnp.concatenate(acc_w, axis=1)
    o_win = acc_w[0:Dh] * (1.0 / acc_w[Dh:Dh + 1])

    for g in range(G):
        for h in range(HPG):
            c0 = (g * HPG + h) * 3
            cols = slice((g * HPG + h) * QB, (g * HPG + h + 1) * QB)
            o = (gates[c0:c0 + 1, :] * o_cmps[g][:, h * QB:(h + 1) * QB] + gates[c0 + 1:c0 + 2, :] * o_slc[:, cols]
                 + gates[c0 + 2:c0 + 3, :] * o_win[:, cols])
            o_ref[0, (g * HPG + h) * Dh:(g * HPG + h + 1) * Dh, :] = o


def _nsa(qt, ksw, vt, kcmp, vcmpt, overlap_t, glt):
    B, _, S = qt.shape
    ncp = S // CMP_STRIDE
    ns = S // SLC_BLOCK
    G = NSA_KV_GROUPS
    return pl.pallas_call(
        functools.partial(_nsa_kernel, seq_len=S),
        grid=(B, S // Q_BLOCK),
        in_specs=[pl.BlockSpec((1, NSA_WIDTH, Q_BLOCK), lambda b, i: (b, 0, i)),
                  pl.BlockSpec((1, S, 2 * LANE), lambda b, i: (b, 0, 0)),
                  pl.BlockSpec((1, 2 * LANE, S), lambda b, i: (b, 0, 0)),
                  pl.BlockSpec((1, G, ncp, HEAD_DIM), lambda b, i: (b, 0, 0, 0)),
                  pl.BlockSpec((1, G, HEAD_DIM, ncp), lambda b, i: (b, 0, 0, 0)),
                  pl.BlockSpec((ns, ncp), lambda b, i: (0, 0)),
                  pl.BlockSpec((1, LANE, Q_BLOCK), lambda b, i: (b, 0, i))],
        out_specs=pl.BlockSpec((1, NSA_WIDTH, Q_BLOCK), lambda b, i: (b, 0, i)),
        out_shape=jax.ShapeDtypeStruct((B, NSA_WIDTH, S), F32),
        scratch_shapes=[pltpu.VMEM((G, ns, Q_BLOCK), F32)],
        compiler_params=_params("parallel", "arbitrary"),
        name="nsa",
    )(qt, ksw, vt, kcmp, vcmpt, overlap_t, glt)


def _rwkv_prep_kernel(f_ref, p_ref, mu_ref, w0_ref, w2_ref, a0_ref, a2_ref, g2_ref, kk_ref, ka_ref,
                      bd_ref, r_o, lw_o, k_o, v_o, kk_o, a_o, g_o):
    W = RWKV_WIDTH
    i = pl.program_id(1)
    f = f_ref[0]
    tm = f.shape[0]
    prev_last = jnp.where(i > 0, p_ref[0, 7:8, :], 0.0)
    rolled = pltpu.roll(f, 1, 0)
    row = lax.broadcasted_iota(I32, (tm, 1), 0)
    prev = jnp.where(row == 0, prev_last, rolled)
    f = f + (prev - f) * mu_ref[...]
    r, k, v = f[:, 0:W], f[:, W:2 * W], f[:, 2 * W:3 * W]
    wd = f[:, 3 * W:3 * W + LANE]
    ad = f[:, 3 * W + LANE:3 * W + 2 * LANE]
    gd = f[:, 3 * W + 2 * LANE:3 * W + 3 * LANE]
    lw_o[0] = -DECAY_SCALE * jax.nn.sigmoid(w0_ref[...] + _dot3(jnp.tanh(wd), w2_ref[...]))
    a = jax.nn.sigmoid(a0_ref[...] + _dot3(ad, a2_ref[...]))
    g_o[0] = _dot3(jax.nn.sigmoid(gd), g2_ref[...])
    kk = k * kk_ref[...]
    ss = _dot_lhs2(kk * kk, bd_ref[...])
    kk_o[0] = kk / jnp.maximum(jnp.sqrt(ss), 1e-12)
    k_o[0] = k * (1.0 + (a - 1.0) * ka_ref[...])
    r_o[0] = r
    v_o[0] = v
    a_o[0] = a


def _rwkv_prep(rw, mu, w0, w2, a0, a2, g2, k_k, k_a, bd):
    B, S, C = rw.shape
    W = RWKV_WIDTH
    tm = min(256, S)
    row = lambda n: pl.BlockSpec((1, n), lambda b, i: (0, 0))
    mat = pl.BlockSpec((LANE, W), lambda b, i: (0, 0))
    tok = pl.BlockSpec((1, tm, W), lambda b, i: (b, i, 0))
    return pl.pallas_call(
        _rwkv_prep_kernel,
        grid=(B, S // tm),
        in_specs=[pl.BlockSpec((1, tm, C), lambda b, i: (b, i, 0)),
                  pl.BlockSpec((1, 8, C), lambda b, i: (b, jnp.maximum(i * (tm // 8) - 1, 0), 0)),
                  row(C), row(W), mat, row(W), mat, mat, row(W), row(W),
                  pl.BlockSpec((W, W), lambda b, i: (0, 0))],
        out_specs=[tok] * 7,
        out_shape=[jax.ShapeDtypeStruct((B, S, W), F32)] * 7,
        compiler_params=_params("parallel", "arbitrary"),
        name="rwkv_prep",
    )(rw, rw, mu, w0, w2, a0, a2, g2, k_k, k_a, bd)


def _rwkv_scan_kernel(r_ref, lw_ref, k_ref, v_ref, kk_ref, a_ref, g_ref, lnw_ref, lnb_ref, rk_ref,
                      o_ref, st_ref):
    Dh = HEAD_DIM

    @pl.when(pl.program_id(1) == 0)
    def _():
        st_ref[...] = jnp.zeros_like(st_ref)

    NB, C = r_ref.shape[0], r_ref.shape[1]
    ri = lax.broadcasted_iota(I32, (C, C), 0)
    ci = lax.broadcasted_iota(I32, (C, C), 1)
    incl = ri >= ci
    strict = ri > ci
    tri = incl.astype(BF16)
    eye = (ri == ci).astype(F32)
    n_double = max(int(math.log2(C)) - 1, 0)

    def scaled_operands(bb):
        r, lw, k, kk, a = (ref[bb] for ref in (r_ref, lw_ref, k_ref, kk_ref, a_ref))
        l_hi = lw.astype(BF16)
        l_r1 = lw - l_hi.astype(F32)
        l_mid = l_r1.astype(BF16)
        l_lo = (l_r1 - l_mid.astype(F32)).astype(BF16)
        L = (jnp.dot(tri, l_hi, preferred_element_type=F32) + jnp.dot(tri, l_mid, preferred_element_type=F32)
             + jnp.dot(tri, l_lo, preferred_element_type=F32))
        e_l = jnp.exp(L)
        e_inv = jnp.exp(-L)
        e_end = jnp.exp(L[C - 1:C, :] - L)
        b = kk * a
        x1 = jnp.concatenate([kk * jnp.exp(L - lw), r * e_l], axis=0).astype(BF16)
        x2t = jnp.concatenate([k * e_inv, b * e_inv], axis=0).T.astype(BF16)
        zt = jnp.concatenate([k * e_end, -(b * e_end)], axis=0).T.astype(BF16)
        gt = jnp.concatenate([e_l, e_l], axis=0).T
        return x1, x2t, zt, gt

    ops = [scaled_operands(bb) for bb in range(NB)]
    chains = [(bb, h) for bb in range(NB) for h in range(RWKV_HEADS)]
    sls = [slice(h * Dh, (h + 1) * Dh) for _, h in chains]
    h0s = [st_ref[bb, h] for bb, h in chains]
    vhs = [v_ref[bb][:, sl] for (bb, _), sl in zip(chains, sls)]
    x1s = [ops[bb][0][:, sl] for (bb, _), sl in zip(chains, sls)]
    amats = [jnp.dot(x1, ops[bb][1][sl, :], preferred_element_type=F32)
             for x1, (bb, _), sl in zip(x1s, chains, sls)]
    p0s = [jnp.dot(x1, h0.astype(BF16), preferred_element_type=F32) for x1, h0 in zip(x1s, h0s)]
    a_kk = [jnp.where(strict, am[0:C, 0:C], 0.0).astype(BF16) for am in amats]
    a_rkb = [jnp.concatenate([jnp.where(incl, am[C:2 * C, 0:C], 0.0), jnp.where(incl, -am[C:2 * C, C:2 * C], 0.0)],
                             axis=1).astype(BF16) for am in amats]
    nmats = [jnp.where(strict, -am[0:C, C:2 * C], 0.0) for am in amats]
    tinvs = [eye + nm for nm in nmats]
    for _ in range(n_double):
        nmats = [_bdot(nm, nm) for nm in nmats]
        tinvs = [ti + _bdot(ti, nm) for ti, nm in zip(tinvs, nmats)]
    akv = [jnp.dot(ak, vh.astype(BF16), preferred_element_type=F32) for ak, vh in zip(a_kk, vhs)]
    us = [_bdot(ti, p0[0:C] + av) for ti, p0, av in zip(tinvs, p0s, akv)]
    vus = [jnp.concatenate([vh, u], axis=0).astype(BF16) for vh, u in zip(vhs, us)]
    ys = [p0[C:2 * C] + jnp.dot(ar, vu, preferred_element_type=F32) for p0, ar, vu in zip(p0s, a_rkb, vus)]
    for i, (bb, h) in enumerate(chains):
        _, _, zt, gt = ops[bb]
        st_ref[bb, h] = gt[sls[i], C - 1:C] * h0s[i] + jnp.dot(zt[sls[i], :], vus[i], preferred_element_type=F32)
    for i, (bb, h) in enumerate(chains):
        sl, y = sls[i], ys[i]
        mean = jnp.mean(y, axis=-1, keepdims=True)
        yc = y - mean
        var = jnp.mean(yc * yc, axis=-1, keepdims=True)
        yn = yc * lax.rsqrt(var + GN_EPS) * lnw_ref[:, sl] + lnb_ref[:, sl]
        bonus = jnp.sum(r_ref[bb][:, sl] * k_ref[bb][:, sl] * rk_ref[:, sl], axis=-1, keepdims=True) * vhs[i]
        o_ref[bb, :, sl] = (yn + bonus) * g_ref[bb][:, sl]


def _rwkv_scan(r, lw, k, v, kk, a, g, ln_w, ln_b, r_k):
    B, S, W = r.shape
    C = min(RWKV_CHUNK, S)
    nb = math.gcd(RWKV_BATCH_ROWS, B)
    tok = pl.BlockSpec((nb, C, W), lambda b, i: (b, i, 0))
    row = pl.BlockSpec((1, W), lambda b, i: (0, 0))
    return pl.pallas_call(
        _rwkv_scan_kernel,
        grid=(B // nb, S // C),
        in_specs=[tok] * 7 + [row] * 3,
        out_specs=tok,
        out_shape=jax.ShapeDtypeStruct((B, S, W), F32),
        scratch_shapes=[pltpu.VMEM((nb, RWKV_HEADS, HEAD_DIM, HEAD_DIM), F32)],
        compiler_params=_params("parallel", "arbitrary"),
        name="rwkv_scan",
    )(r, lw, k, v, kk, a, g, ln_w, ln_b, r_k)


def _rms(x, g):
    return x * lax.rsqrt(jnp.mean(x * x, axis=-1, keepdims=True) + RMS_EPS) * g


def _outproj_kernel(on_ref, or_ref, x_ref, gn_ref, wt_ref, wb_ref, gpost_ref, gm_ref, gpre_ref,
                    scf_ref, shf_ref, rw_ref, rb_ref, x1_ref, h2_ref, idx_ref, wgt_ref):
    ont = on_ref[0]
    ont = ont * lax.rsqrt(jnp.mean(ont * ont, axis=0, keepdims=True) + RMS_EPS) * gn_ref[...]
    on = ont.T
    mixed = (jnp.dot(on.astype(BF16), wt_ref[...], preferred_element_type=F32)
             + jnp.dot(or_ref[0].astype(BF16), wb_ref[...], preferred_element_type=F32))
    x1 = x_ref[0] + gm_ref[0] * _rms(mixed, gpost_ref[...])
    x1_ref[0] = x1
    h2 = _rms(x1, gpre_ref[...]) * scf_ref[0] + shf_ref[0]
    h2_ref[0] = h2.astype(BF16)
    logits = _dot3(h2, rw_ref[...]) + rb_ref[...]
    lane = lax.broadcasted_iota(I32, logits.shape, 1)
    vals = logits
    top_v, top_i = [], []
    for _ in range(TOP_K):
        mx = jnp.max(vals, axis=-1, keepdims=True)
        ix = jnp.min(jnp.where(vals == mx, lane, LANE), axis=-1, keepdims=True)
        top_v.append(mx)
        top_i.append(ix)
        vals = jnp.where(lane == ix, LOWEST, vals)
    ex = [jnp.exp(tv - top_v[0]) for tv in top_v]
    inv = 1.0 / (ex[0] + ex[1] + ex[2] + ex[3])
    idx = jnp.zeros(logits.shape, I32)
    wgt = jnp.zeros(logits.shape, F32)
    for kk in range(TOP_K):
        idx = jnp.where(lane == kk, top_i[kk], idx)
        wgt = jnp.where(lane == kk, ex[kk] * inv, wgt)
    idx_ref[0] = idx
    wgt_ref[0] = wgt


def _outproj(o_nsa, o_rwkv, x, gn, w_top, w_bot, g_post, g_m, g_pre, sc1_f, sh_f, router_w, router_b):
    B, S, D = x.shape
    tm = min(256, S)
    W = o_rwkv.shape[-1]
    tok = lambda n: pl.BlockSpec((1, tm, n), lambda b, i: (b, i, 0))
    vec = pl.BlockSpec((1, 1, D), lambda b, i: (b, 0, 0))
    row = lambda n: pl.BlockSpec((1, n), lambda b, i: (0, 0))
    return pl.pallas_call(
        _outproj_kernel,
        grid=(B, S // tm),
        in_specs=[pl.BlockSpec((1, W, tm), lambda b, i: (b, 0, i)), tok(W), tok(D),
                  pl.BlockSpec((W, 1), lambda b, i: (0, 0)),
                  pl.BlockSpec((W, D), lambda b, i: (0, 0)), pl.BlockSpec((W, D), lambda b, i: (0, 0)),
                  row(D), vec, row(D), vec, vec,
                  pl.BlockSpec((D, LANE), lambda b, i: (0, 0)), row(LANE)],
        out_specs=[tok(D), tok(D), tok(LANE), tok(LANE)],
        out_shape=[jax.ShapeDtypeStruct((B, S, D), F32), jax.ShapeDtypeStruct((B, S, D), BF16),
                   jax.ShapeDtypeStruct((B, S, LANE), I32), jax.ShapeDtypeStruct((B, S, LANE), F32)],
        compiler_params=_params("parallel", "arbitrary"),
        name="outproj",
    )(o_nsa, o_rwkv, x, gn, w_top, w_bot, g_post, g_m, g_pre, sc1_f, sh_f, router_w, router_b)


def _slot_runs_kernel(start_ref, nvalid_ref, tok_ref, o_ref):
    rows = MOE_BLOCK // LANE
    lane = lax.broadcasted_iota(I32, (rows, LANE), 1)
    within = lax.broadcasted_iota(I32, (rows, LANE), 0) * LANE + lane

    def body(b, carry):
        s = start_ref[b]
        off = jnp.bitwise_and(s, LANE - 1)
        win = tok_ref[pl.ds(lax.shift_right_logical(s, 7), rows + 1), :]
        win = pltpu.roll(win, jnp.bitwise_and(LANE - off, LANE - 1), 1)
        run = jnp.where(lane < LANE - off, win[0:rows], win[1:rows + 1])
        o_ref[pl.ds(pl.multiple_of(b * rows, rows), rows), :] = jnp.where(within < nvalid_ref[b], run, 0)
        return carry

    lax.fori_loop(0, start_ref.shape[0], body, 0)


def _slot_runs(run_start, nvalid, tok_sorted):
    nb = run_start.shape[0]
    rows = MOE_BLOCK // LANE
    n_rows = -(-(tok_sorted.shape[0] // LANE + rows + 1) // 8) * 8
    tok2d = jnp.pad(tok_sorted, (0, n_rows * LANE - tok_sorted.shape[0])).reshape(n_rows, LANE)
    grid_spec = pltpu.PrefetchScalarGridSpec(
        num_scalar_prefetch=2,
        grid=(1,),
        in_specs=[pl.BlockSpec((n_rows, LANE), lambda i, s, n: (0, 0))],
        out_specs=pl.BlockSpec((nb * rows, LANE), lambda i, s, n: (0, 0)),
    )
    out = pl.pallas_call(
        _slot_runs_kernel,
        grid_spec=grid_spec,
        out_shape=jax.ShapeDtypeStruct((nb * rows, LANE), I32),
        compiler_params=_params("arbitrary"),
        name="slot_runs",
    )(run_start, nvalid, tok2d)
    return out.reshape(nb * MOE_BLOCK)


def _expert_kernel(be_ref, nu_ref, x_ref, wgu_ref, bgu_ref, wd_ref, bd_ref, o_ref, wgu_bf, wd_bf):
    F = wd_ref.shape[1]
    i = pl.program_id(0)
    active = i < nu_ref[0]
    new_expert = (i == 0) | (be_ref[i] != be_ref[jnp.maximum(i - 1, 0)])

    @pl.when(active & new_expert)
    def _():
        wgu_bf[...] = wgu_ref[0].astype(BF16)
        wd_bf[...] = wd_ref[0].astype(BF16)

    @pl.when(active)
    def _():
        gu = jnp.dot(x_ref[...], wgu_bf[...], preferred_element_type=F32) + bgu_ref[0]
        gate = jnp.minimum(gu[:, 0:F], SWIGLU_LIMIT)
        up = jnp.clip(gu[:, F:2 * F], -SWIGLU_LIMIT, SWIGLU_LIMIT)
        glu = gate * jax.nn.sigmoid(gate * SWIGLU_ALPHA)
        out = jnp.dot(((up + 1.0) * glu).astype(BF16), wd_bf[...], preferred_element_type=F32) + bd_ref[0]
        o_ref[...] = out.astype(o_ref.dtype)

    @pl.when(jnp.logical_not(active))
    def _():
        o_ref[...] = jnp.zeros_like(o_ref)


def _experts(blk_expert, n_used, x_sorted, w_gu, b_gu, w_d, b_d):
    P, D = x_sorted.shape
    E, _, F2 = w_gu.shape
    F = F2 // 2
    nb = P // MOE_BLOCK
    grid_spec = pltpu.PrefetchScalarGridSpec(
        num_scalar_prefetch=2,
        grid=(nb,),
        in_specs=[pl.BlockSpec((MOE_BLOCK, D), lambda i, be, nu: (i, 0)),
                  pl.BlockSpec((1, D, F2), lambda i, be, nu: (be[i], 0, 0)),
                  pl.BlockSpec((1, 1, F2), lambda i, be, nu: (be[i], 0, 0)),
                  pl.BlockSpec((1, F, D), lambda i, be, nu: (be[i], 0, 0)),
                  pl.BlockSpec((1, 1, D), lambda i, be, nu: (be[i], 0, 0))],
        out_specs=pl.BlockSpec((MOE_BLOCK, D), lambda i, be, nu: (i, 0)),
        scratch_shapes=[pltpu.VMEM((D, F2), BF16), pltpu.VMEM((F, D), BF16)],
    )
    return pl.pallas_call(
        _expert_kernel,
        grid_spec=grid_spec,
        out_shape=jax.ShapeDtypeStruct((P, D), F32),
        compiler_params=_params("arbitrary"),
        name="experts",
    )(blk_expert, n_used, x_sorted, w_gu, b_gu.reshape(E, 1, F2), w_d, b_d.reshape(E, 1, D))


def _combine_kernel(y_ref, w_ref, x1_ref, gpost_ref, gf_ref, o_ref):
    w = w_ref[0]
    y = (w[:, 0:1] * y_ref[0].astype(F32) + w[:, 1:2] * y_ref[1].astype(F32)
         + w[:, 2:3] * y_ref[2].astype(F32) + w[:, 3:4] * y_ref[3].astype(F32))
    o_ref[0] = x1_ref[0] + gf_ref[0] * _rms(y, gpost_ref[...])


def _combine(y4, wgt, x1, g_post, g_f):
    B, S, D = x1.shape
    tm = min(256, S)
    nt = S // tm
    return pl.pallas_call(
        _combine_kernel,
        grid=(B, nt),
        in_specs=[pl.BlockSpec((TOP_K, tm, D), lambda b, i: (0, b * nt + i, 0)),
                  pl.BlockSpec((1, tm, LANE), lambda b, i: (b, i, 0)),
                  pl.BlockSpec((1, tm, D), lambda b, i: (b, i, 0)),
                  pl.BlockSpec((1, D), lambda b, i: (0, 0)),
                  pl.BlockSpec((1, 1, D), lambda b, i: (b, 0, 0))],
        out_specs=pl.BlockSpec((1, tm, D), lambda b, i: (b, i, 0)),
        out_shape=jax.ShapeDtypeStruct((B, S, D), F32),
        compiler_params=_params("parallel", "arbitrary"),
        name="combine",
    )(y4, wgt, x1, g_post, g_f)


def _rot_cols(w):
    d, n = w.shape
    w4 = w.reshape(d, n // HEAD_DIM, 2, HEAD_DIM // 2)
    return jnp.concatenate([-w4[:, :, 1:2], w4[:, :, 0:1]], axis=2).reshape(d, n)


def _pad_cols(w, n):
    return jnp.pad(w, ((0, 0), (0, n - w.shape[1])))


def _pad_rows(w, n):
    return jnp.pad(w, ((0, n - w.shape[0]), (0, 0)))


def _layer(x, ada, l, mix_pre_norm, mix_post_norm, ffn_pre_norm, ffn_post_norm, w_in, cmp_k_pe, cmp_k_w1,
           cmp_k_w2, cmp_v_pe, cmp_v_w1, cmp_v_w2, nsa_out_norm, rwkv_mu, rwkv_w0, rwkv_w2, rwkv_a0, rwkv_a2,
           rwkv_g2, rwkv_k_k, rwkv_k_a, rwkv_r_k, rwkv_ln_w, rwkv_ln_b, w_out, router_w, router_b,
           expert_w_gate_up, expert_b_gate_up, expert_w_down, expert_b_down):
    B, S, D = x.shape
    T = B * S
    G, Dh, W = NSA_KV_GROUPS, HEAD_DIM, RWKV_WIDTH
    sh_m, sc_m, g_m, sh_f, sc_f, g_f = [a.reshape(B, 1, D) for a in jnp.split(ada, 6, axis=-1)]

    wi = w_in[l]
    KV = NSA_KV_WIDTH
    o = NSA_WIDTH
    wq, wkc, wvc, wks, wvs, wkw, wvw = (wi[:, 0:o],) + tuple(wi[:, o + j * KV:o + (j + 1) * KV] for j in range(6))
    wgl = wi[:, o + 6 * KV:o + 6 * KV + 3 * NSA_HEADS]
    wr = wi[:, o + 6 * KV + 3 * NSA_HEADS:]
    w_rope = jnp.concatenate([wq, wkc, wks, wkw], axis=1)
    w_rest = _pad_cols(jnp.concatenate([wvc, wvs, wvw, wgl], axis=1), REST_COLS)
    lora = [(3 * W, RWKV_W_LORA), (3 * W + RWKV_W_LORA, RWKV_A_LORA),
            (3 * W + RWKV_W_LORA + RWKV_A_LORA, RWKV_G_LORA)]
    pad_lora = lambda a: jnp.concatenate([_pad_cols(a[..., s:s + n], LANE) for s, n in lora], axis=-1)
    w_rw = jnp.concatenate([wr[:, 0:3 * W], pad_lora(wr)], axis=1)
    w_all = jnp.concatenate([w_rope, _rot_cols(w_rope), w_rest, w_rw], axis=1).astype(BF16)
    mu_row = rwkv_mu[l].reshape(1, -1)
    mu = jnp.concatenate([mu_row[:, 0:3 * W], pad_lora(mu_row)], axis=1)

    half = Dh // 2
    inv_freq = ROPE_THETA ** (-jnp.arange(half, dtype=F32) / half)
    ang = jnp.arange(S, dtype=F32)[:, None] * inv_freq[None, :]
    n_rope_heads = ROPE_COLS // Dh
    col_scale = jnp.where(jnp.arange(ROPE_COLS) < NSA_WIDTH, Dh ** -0.5 * math.log2(math.e), 1.0).astype(F32)
    cos_t = jnp.tile(jnp.cos(ang), (1, 2 * n_rope_heads)) * col_scale
    sin_t = jnp.tile(jnp.sin(ang), (1, 2 * n_rope_heads)) * col_scale

    qt, kc, ksw, vc, vt, glt, rw = _inproj(x, mix_pre_norm[l], 1.0 + sc_m, sh_m, w_all, cos_t, sin_t)

    nb = S // CMP_STRIDE
    regroup = lambda t: t.reshape(B, nb, CMP_STRIDE, G, Dh).transpose(0, 3, 1, 2, 4).reshape(B, G, nb, CMP_STRIDE * Dh)
    pe8 = lambda pe: jnp.broadcast_to(pe.reshape(1, -1), (8, pe.size))
    kcmp = _compress(regroup(kc), cmp_k_w1[l], cmp_k_w2[l], pe8(cmp_k_pe[l]), transposed=False)
    vcmpt = _compress(regroup(vc), cmp_v_w1[l], cmp_v_w2[l].T, pe8(cmp_v_pe[l]), transposed=True)
    ns = S // SLC_BLOCK
    c0 = jnp.arange(nb)[None, :] * CMP_STRIDE
    b0 = jnp.arange(ns)[:, None] * SLC_BLOCK
    ov = jnp.maximum(jnp.minimum(c0 + CMP_BLOCK, b0 + SLC_BLOCK) - jnp.maximum(c0, b0), 0)
    overlap_t = (ov.astype(F32) / CMP_BLOCK).astype(BF16)
    o_nsa_t = _nsa(qt, ksw, vt, kcmp, vcmpt, overlap_t, glt)

    bd = (jnp.arange(W)[:, None] // Dh == jnp.arange(W)[None, :] // Dh).astype(BF16)
    row = lambda a: a.reshape(1, -1)
    r, lw, k, v, kk, a, g = _rwkv_prep(
        rw, mu, row(rwkv_w0[l]), _pad_rows(rwkv_w2[l], LANE), row(rwkv_a0[l]), _pad_rows(rwkv_a2[l], LANE),
        _pad_rows(rwkv_g2[l], LANE), row(rwkv_k_k[l]), row(rwkv_k_a[l]), bd)
    o_rwkv = _rwkv_scan(r, lw, k, v, kk, a, g, row(rwkv_ln_w[l]), row(rwkv_ln_b[l]), row(rwkv_r_k[l]))

    wo = w_out[l].astype(BF16)
    rb = jnp.concatenate([router_b[l], jnp.full((LANE - N_EXPERTS,), NEG_INF, F32)]).reshape(1, LANE)
    x1, h2, idx, wgt = _outproj(o_nsa_t, o_rwkv, x, nsa_out_norm[l].reshape(-1, 1), wo[0:NSA_WIDTH], wo[NSA_WIDTH:],
                                row(mix_post_norm[l]), g_m, row(ffn_pre_norm[l]), 1.0 + sc_f, sh_f,
                                _pad_cols(router_w[l], LANE), rb)

    n_assign = T * TOP_K
    idx4 = idx.reshape(T, LANE)[:, 0:TOP_K]
    e_ids = jnp.arange(N_EXPERTS, dtype=I32)
    hot = idx4[:, :, None] == e_ids
    onehot = hot.astype(I32).sum(axis=1)
    csum = jnp.cumsum(onehot, axis=0)
    counts = csum[-1]
    starts = jnp.cumsum(counts) - counts
    padded = ((counts + MOE_BLOCK - 1) // MOE_BLOCK) * MOE_BLOCK
    pad_ends = jnp.cumsum(padded)
    pad_starts = pad_ends - padded
    pos = jnp.where(hot, (csum - onehot + pad_starts)[:, None, :], 0).sum(axis=-1).astype(I32)
    n_blocks = -(-n_assign // MOE_BLOCK) + N_EXPERTS
    P = n_blocks * MOE_BLOCK
    blk_start = jnp.arange(n_blocks, dtype=I32) * MOE_BLOCK
    blk_expert = jnp.minimum(jnp.searchsorted(pad_ends, blk_start, side='right'), N_EXPERTS - 1).astype(I32)
    n_used = (pad_ends[-1] // MOE_BLOCK).astype(I32).reshape(1)
    order = jnp.argsort(idx4.reshape(-1))
    blk_off = blk_start - pad_starts[blk_expert]
    run_start = jnp.minimum(starts[blk_expert] + blk_off, n_assign).astype(I32)
    nvalid = jnp.clip(counts[blk_expert] - blk_off, 0, MOE_BLOCK).astype(I32)
    slot_tok = _slot_runs(run_start, nvalid, (order // TOP_K).astype(I32))

    x_sorted = h2.reshape(T, D)[slot_tok]
    y_sorted = _experts(blk_expert, n_used, x_sorted, expert_w_gate_up[l], expert_b_gate_up[l],
                        expert_w_down[l], expert_b_down[l])
    y4 = y_sorted[pos.T]
    return _combine(y4, wgt, x1, row(ffn_post_norm[l]), g_f)


def kernel(x, c, ada_w, ada_b, mix_pre_norm, mix_post_norm, ffn_pre_norm, ffn_post_norm, w_in, cmp_k_pe, cmp_k_w1, cmp_k_w2, cmp_v_pe, cmp_v_w1, cmp_v_w2, nsa_out_norm, rwkv_mu, rwkv_w0, rwkv_w2, rwkv_a0, rwkv_a2, rwkv_g2, rwkv_k_k, rwkv_k_a, rwkv_r_k, rwkv_ln_w, rwkv_ln_b, w_out, router_w, router_b, expert_w_gate_up, expert_b_gate_up, expert_w_down, expert_b_down):
    for l in range(ada_w.shape[0]):
        ada = _ada(c, ada_w[l], ada_b[l])
        x = _layer(x, ada, l, mix_pre_norm, mix_post_norm, ffn_pre_norm, ffn_post_norm, w_in, cmp_k_pe,
                   cmp_k_w1, cmp_k_w2, cmp_v_pe, cmp_v_w1, cmp_v_w2, nsa_out_norm, rwkv_mu, rwkv_w0, rwkv_w2,
                   rwkv_a0, rwkv_a2, rwkv_g2, rwkv_k_k, rwkv_k_a, rwkv_r_k, rwkv_ln_w, rwkv_ln_b, w_out,
                   router_w, router_b, expert_w_gate_up, expert_b_gate_up, expert_w_down, expert_b_down)
    return x
```

```python
import functools
import math

import jax
import jax.numpy as jnp
from jax import lax
from jax.experimental import pallas as pl
from jax.experimental.pallas import tpu as pltpu

F32 = jnp.float32
BF16 = jnp.bfloat16
I32 = jnp.int32

HEAD_DIM = 64
NSA_HEADS = 8
NSA_KV_GROUPS = 2
NSA_HPG = NSA_HEADS // NSA_KV_GROUPS
NSA_WIDTH = NSA_HEADS * HEAD_DIM
NSA_KV_WIDTH = NSA_KV_GROUPS * HEAD_DIM
CMP_BLOCK = 32
CMP_STRIDE = 16
SLC_BLOCK = 64
SLC_TOPK = 16
WINDOW = 512
Q_BLOCK = 128
RWKV_HEADS = 8
RWKV_WIDTH = RWKV_HEADS * HEAD_DIM
RWKV_W_LORA = 32
RWKV_A_LORA = 32
RWKV_G_LORA = 96
DECAY_SCALE = math.exp(-0.5)
GN_EPS = 64e-5
N_EXPERTS = 32
TOP_K = 4
SWIGLU_LIMIT = 7.0
SWIGLU_ALPHA = 1.702
ROPE_THETA = 10000.0
RMS_EPS = 1e-6
NEG_INF = -1e30
FORCED = 1e9
LOWEST = -3e38

LANE = 128
ROPE_COLS = NSA_WIDTH + 3 * NSA_KV_WIDTH
REST_COLS = 4 * LANE
RW_COLS = 3 * RWKV_WIDTH + 3 * LANE
SLC_KT = 512
SLC_UNROLL = 1
FLASH_LANES = 1024
FLASH_DEPTH = 1
ACC_ROWS = HEAD_DIM + 16
RWKV_CHUNK = 128
RWKV_BATCH_ROWS = 1
MOE_BLOCK = 512
VMEM_LIMIT = 56 * 1024 * 1024


def _bdot(a, b):
    return jnp.dot(a.astype(BF16), b.astype(BF16), preferred_element_type=F32)


def _split2(a):
    hi = a.astype(BF16)
    lo = (a - hi.astype(F32)).astype(BF16)
    return hi, lo


def _dot_lhs2(a, b_bf16):
    hi, lo = _split2(a)
    return (jnp.dot(hi, b_bf16, preferred_element_type=F32)
            + jnp.dot(lo, b_bf16, preferred_element_type=F32))


def _dot3(a, b):
    ah, al = _split2(a)
    bh, bl = _split2(b)
    return (jnp.dot(ah, bh, preferred_element_type=F32)
            + jnp.dot(al, bh, preferred_element_type=F32)
            + jnp.dot(ah, bl, preferred_element_type=F32))


def _params(*sem, flags=None):
    return pltpu.CompilerParams(dimension_semantics=sem, vmem_limit_bytes=VMEM_LIMIT, flags=flags)


def _ada_kernel(c_ref, w_ref, b_ref, o_ref):
    c = c_ref[...]
    act = c * jax.nn.sigmoid(c)
    o_ref[...] = _dot3(act, w_ref[...]) + b_ref[...]


def _ada(c, w, b):
    B, D = c.shape
    N = w.shape[1]
    tn = 1536
    return pl.pallas_call(
        _ada_kernel,
        grid=(N // tn,),
        in_specs=[pl.BlockSpec((B, D), lambda j: (0, 0)),
                  pl.BlockSpec((D, tn), lambda j: (0, j)),
                  pl.BlockSpec((1, tn), lambda j: (0, j))],
        out_specs=pl.BlockSpec((B, tn), lambda j: (0, j)),
        out_shape=jax.ShapeDtypeStruct((B, N), F32),
        compiler_params=_params("arbitrary"),
        name="ada",
    )(c, w, b.reshape(1, N))


def _inproj_kernel(x_ref, g_ref, sc_ref, sh_ref, w_ref, cos_ref, sin_ref,
                   q_ref, kc_ref, ksw_ref, vc_ref, vsw_ref, gl_ref, rw_ref):
    x = x_ref[0]
    ms = jnp.mean(x * x, axis=-1, keepdims=True)
    h = x * lax.rsqrt(ms + RMS_EPS) * g_ref[...]
    h = h * sc_ref[0] + sh_ref[0]
    hb = h.astype(BF16)
    R = ROPE_COLS
    main = jnp.dot(hb, w_ref[:, 0:R], preferred_element_type=F32)
    rot = jnp.dot(hb, w_ref[:, R:2 * R], preferred_element_type=F32)
    roped = main * cos_ref[...] + rot * sin_ref[...]
    q_ref[0] = roped[:, 0:NSA_WIDTH].T.astype(BF16)
    kc_ref[0] = roped[:, NSA_WIDTH:NSA_WIDTH + LANE]
    ksw_ref[0] = roped[:, NSA_WIDTH + LANE:R].astype(BF16)
    rest = jnp.dot(hb, w_ref[:, 2 * R:2 * R + REST_COLS], preferred_element_type=F32)
    vc_ref[0] = rest[:, 0:LANE]
    vsw_ref[0] = rest[:, LANE:3 * LANE].T.astype(BF16)
    gl_ref[0] = rest[:, 3 * LANE:4 * LANE].T
    rw_ref[0] = jnp.dot(hb, w_ref[:, 2 * R + REST_COLS:], preferred_element_type=F32)


def _inproj(x, gain, sc1, sh, w_all, cos_t, sin_t):
    B, S, D = x.shape
    tm = min(256, S)
    NW = w_all.shape[1]
    R = ROPE_COLS
    tok = lambda n: pl.BlockSpec((1, tm, n), lambda b, i: (b, i, 0))
    tok_t = lambda n: pl.BlockSpec((1, n, tm), lambda b, i: (b, 0, i))
    vec = pl.BlockSpec((1, 1, D), lambda b, i: (b, 0, 0))
    outs = [(NSA_WIDTH, BF16, True), (LANE, F32, False), (2 * LANE, BF16, False), (LANE, F32, False),
            (2 * LANE, BF16, True), (LANE, F32, True), (RW_COLS, F32, False)]
    return pl.pallas_call(
        _inproj_kernel,
        grid=(B, S // tm),
        in_specs=[tok(D), pl.BlockSpec((1, D), lambda b, i: (0, 0)), vec, vec,
                  pl.BlockSpec((D, NW), lambda b, i: (0, 0)),
                  pl.BlockSpec((tm, R), lambda b, i: (i, 0)),
                  pl.BlockSpec((tm, R), lambda b, i: (i, 0))],
        out_specs=[tok_t(n) if tr else tok(n) for n, _, tr in outs],
        out_shape=[jax.ShapeDtypeStruct((B, n, S) if tr else (B, S, n), dt) for n, dt, tr in outs],
        compiler_params=_params("parallel", "arbitrary"),
        name="inproj",
    )(x, gain.reshape(1, D), sc1, sh, w_all, cos_t, sin_t)


def _gelu_tanh(x):
    return 0.5 * x * (1.0 + jnp.tanh(math.sqrt(2.0 / math.pi) * (x + 0.044715 * (x * x * x))))


def _compress_kernel(r_ref, w1g_ref, w1_ref, w2_ref, pe_ref, o_ref, *, transposed):
    R = r_ref[0].astype(BF16)
    nb = R.shape[0]
    top = jnp.dot(R, w1g_ref[0, 0].astype(BF16), preferred_element_type=F32)
    bot = jnp.dot(R, w1g_ref[0, 1].astype(BF16), preferred_element_type=F32)
    pe_term = jnp.dot(pe_ref[...].astype(BF16), w1_ref[...].astype(BF16), preferred_element_type=F32)[0:1]
    bot_next = pltpu.roll(bot, nb - 1, 0)
    hid = _gelu_tanh(top + bot_next + pe_term)
    if transposed:
        out = jnp.dot(w2_ref[...].astype(BF16), hid.T.astype(BF16), preferred_element_type=F32)
    else:
        out = jnp.dot(hid.astype(BF16), w2_ref[...].astype(BF16), preferred_element_type=F32)
    o_ref[0, 0] = out.astype(BF16)


def _compress(tok, w1, w2, pe, transposed):
    B, S, GD = tok.shape
    G, Dh = NSA_KV_GROUPS, HEAD_DIM
    nb = S // CMP_STRIDE
    hid = w1.shape[-1]
    width = CMP_STRIDE * GD
    rr = tok.reshape(B, nb, width)
    w1r = w1.reshape(2, CMP_STRIDE, Dh, hid)
    w1g = jnp.zeros((G, 2, CMP_STRIDE, G, Dh, hid), w1.dtype)
    for g in range(G):
        w1g = w1g.at[g, :, :, g].set(w1r)
    w1g = w1g.reshape(G, 2, width, hid)
    oshape = (Dh, nb) if transposed else (nb, Dh)
    return pl.pallas_call(
        functools.partial(_compress_kernel, transposed=transposed),
        grid=(B, G),
        in_specs=[pl.BlockSpec((1, nb, width), lambda b, g: (b, 0, 0)),
                  pl.BlockSpec((1, 2, width, hid), lambda b, g: (g, 0, 0, 0)),
                  pl.BlockSpec(w1.shape, lambda b, g: (0, 0)),
                  pl.BlockSpec(w2.shape, lambda b, g: (0, 0)),
                  pl.BlockSpec(pe.shape, lambda b, g: (0, 0))],
        out_specs=pl.BlockSpec((1, 1) + oshape, lambda b, g: (b, g, 0, 0)),
        out_shape=jax.ShapeDtypeStruct((B, G) + oshape, BF16),
        compiler_params=_params("arbitrary", "arbitrary"),
        name="compress_v" if transposed else "compress_k",
    )(rr, w1g, w1, w2, pe)


def _lanes4(a):
    return jnp.concatenate([a, a, a, a], axis=1)


def _nsa_kernel(qt_ref, ksw_ref, vt_ref, kcmp_ref, vcmpt_ref, ovt_ref, glt_ref, o_ref, selb_ref, *, seq_len):
    QB, Dh, HPG = Q_BLOCK, HEAD_DIM, NSA_HPG
    qi = pl.program_id(1)
    s0 = qi * QB
    ns = seq_len // SLC_BLOCK
    ncp = seq_len // CMP_STRIDE
    n_sel = min(SLC_TOPK, ns)
    kt_slc = min(SLC_KT, seq_len)
    t_row = s0 + lax.broadcasted_iota(I32, (1, QB), 1)
    t4 = _lanes4(t_row)
    gates = jax.nn.sigmoid(glt_ref[0])
    blk = lax.broadcasted_iota(I32, (ns, QB), 0)
    cur = lax.shift_right_logical(t_row, 6)
    forced = (blk == 0) | (blk == cur) | (blk == cur - 1)
    future = blk * SLC_BLOCK > t_row
    cmp_last = lax.broadcasted_iota(I32, (ncp, 1), 0) * CMP_STRIDE + (CMP_BLOCK - 1)
    cmask = cmp_last <= t4
    zeros_q = jnp.zeros((Dh, HPG * QB), BF16)
    G = NSA_KV_GROUPS
    NL = G * HPG * QB
    qpads, o_cmps, imps = [], [], []

    for g in range(G):
        qg = jnp.concatenate(
            [qt_ref[0, (g * HPG + h) * Dh:(g * HPG + h + 1) * Dh, :] for h in range(HPG)], axis=1)
        qpads.append(jnp.concatenate([qg, zeros_q] if g == 0 else [zeros_q, qg], axis=0))
        sc = jnp.where(cmask, jnp.dot(kcmp_ref[0, g], qg, preferred_element_type=F32), NEG_INF)
        m = jnp.max(sc, axis=0, keepdims=True)
        p = jnp.where(cmask, jnp.exp2(sc - m), 0.0)
        l = jnp.sum(p, axis=0, keepdims=True)
        p = p * (1.0 / jnp.maximum(l, 1e-30))
        o_cmps.append(jnp.dot(vcmpt_ref[0, g], p.astype(BF16), preferred_element_type=F32))
        psum = p[:, 0:QB] + p[:, QB:2 * QB] + p[:, 2 * QB:3 * QB] + p[:, 3 * QB:4 * QB]
        p_hi, p_lo = _split2(psum)
        imp = (jnp.dot(ovt_ref[...], p_hi, preferred_element_type=F32)
               + jnp.dot(ovt_ref[...], p_lo, preferred_element_type=F32))
        imps.append(jnp.where(future, NEG_INF, jnp.where(forced, FORCED, imp)))

    vals = jnp.concatenate(imps, axis=1)
    blk2 = jnp.concatenate([blk] * G, axis=1)
    sel = jnp.zeros((ns, G * QB), F32)
    for _ in range(n_sel):
        mx = jnp.max(vals, axis=0, keepdims=True)
        idx = jnp.min(jnp.where(vals == mx, blk2, ns), axis=0, keepdims=True)
        pick = blk2 == idx
        sel = jnp.where(pick, 1.0, sel)
        vals = jnp.where(pick, LOWEST, vals)
    for g in range(G):
        selb_ref[g] = jnp.where(sel[:, g * QB:(g + 1) * QB] > 0.5, 0.0, NEG_INF)

    q_slabs = [jnp.concatenate(qpads, axis=1)[:, sb * FLASH_LANES:(sb + 1) * FLASH_LANES]
               for sb in range(NL // FLASH_LANES)]
    n_slab = len(q_slabs)

    def load_tile(k_lane0, vrow0, k0, width, bias_of_group):
        ones = jnp.ones((ACC_ROWS - Dh, width), BF16)
        kt = ksw_ref[0, pl.ds(k0, width), k_lane0:k_lane0 + LANE]
        vaug = [jnp.concatenate([vt_ref[0, vrow0 + g * Dh:vrow0 + (g + 1) * Dh, pl.ds(k0, width)], ones], axis=0)
                for g in range(G)]
        bias = [bias_of_group(g) for g in range(G)]
        return kt, vaug, bias

    group_lanes = HPG * QB
    groups_per_slab = max(FLASH_LANES // group_lanes, 1)
    lanes_per_part = FLASH_LANES // groups_per_slab

    def flash_tiles(tiles, carry):
        m, acc = list(carry[0]), list(carry[1])
        steps = [(ti, sb) for ti in range(len(tiles)) for sb in range(n_slab)]
        scores = {}
        for i in range(len(steps) + FLASH_DEPTH):
            if i < len(steps):
                ti, sb = steps[i]
                scores[i] = jnp.dot(tiles[ti][0], q_slabs[sb], preferred_element_type=F32)
            j = i - FLASH_DEPTH
            if j >= 0:
                ti, sb = steps[j]
                _, vaug, bias = tiles[ti]
                g0 = sb * FLASH_LANES // group_lanes
                slab_bias = jnp.concatenate(
                    [bias[g0 + p] for p in range(groups_per_slab) for _ in range(lanes_per_part // QB)], axis=1)
                s = scores.pop(j) + slab_bias
                m_new = jnp.maximum(m[sb], jnp.max(s, axis=0, keepdims=True))
                alpha = jnp.exp2(m[sb] - m_new)
                pj = jnp.exp2(s - m_new).astype(BF16)
                pv = [jnp.dot(vaug[g0 + p], pj[:, p * lanes_per_part:(p + 1) * lanes_per_part],
                              preferred_element_type=F32) for p in range(groups_per_slab)]
                acc[sb] = alpha * acc[sb] + (pv[0] if len(pv) == 1 else jnp.concatenate(pv, axis=1))
                m[sb] = m_new
        return tuple(m), tuple(acc)

    def slc_tile(j):
        k0 = pl.multiple_of(j * kt_slc, kt_slc)
        nblk = kt_slc // SLC_BLOCK
        causal = k0 + lax.broadcasted_iota(I32, (kt_slc, 1), 0) <= t_row

        def bias(g):
            blk_bias = jnp.concatenate(
                [jnp.broadcast_to(selb_ref[g, pl.ds(j * nblk + i, 1), :], (SLC_BLOCK, QB)) for i in range(nblk)],
                axis=0)
            return jnp.where(causal, blk_bias, NEG_INF)

        return load_tile(0, 0, k0, kt_slc, bias)

    n_all = seq_len // kt_slc
    unroll = math.gcd(SLC_UNROLL, n_all)

    def slc_body(jj, carry):
        return flash_tiles([slc_tile(jj * unroll + u) for u in range(unroll)], carry)

    init = (tuple(jnp.full((1, FLASH_LANES), NEG_INF, F32) for _ in range(n_slab)),
            tuple(jnp.zeros((ACC_ROWS, FLASH_LANES), F32) for _ in range(n_slab)))
    n_tiles = (s0 + QB - 1) // kt_slc + 1
    _, acc_s = lax.fori_loop(0, (n_tiles + unroll - 1) // unroll, slc_body, init)
    acc_s = jnp.concatenate(acc_s, axis=1)
    o_slc = acc_s[0:Dh] * (1.0 / acc_s[Dh:Dh + 1])

    win_keys = min(WINDOW + QB, seq_len)
    k0 = pl.multiple_of(jnp.maximum(qi - WINDOW // QB, 0) * QB, QB)
    kpos = k0 + lax.broadcasted_iota(I32, (win_keys, 1), 0)
    wbias = jnp.where((kpos <= t_row) & (kpos > t_row - WINDOW), 0.0, NEG_INF)
    _, acc_w = flash_tiles([load_tile(LANE, LANE, k0, win_keys, lambda g: wbias)], init)
    acc_w = jnp.concatenate(acc_w, axis=1)
    o_win = acc_w[0:Dh] * (1.0 / acc_w[Dh:Dh + 1])

    for g in range(G):
        for h in range(HPG):
            c0 = (g * HPG + h) * 3
            cols = slice((g * HPG + h) * QB, (g * HPG + h + 1) * QB)
            o = (gates[c0:c0 + 1, :] * o_cmps[g][:, h * QB:(h + 1) * QB] + gates[c0 + 1:c0 + 2, :] * o_slc[:, cols]
                 + gates[c0 + 2:c0 + 3, :] * o_win[:, cols])
            o_ref[0, (g * HPG + h) * Dh:(g * HPG + h + 1) * Dh, :] = o


def _nsa(qt, ksw, vt, kcmp, vcmpt, overlap_t, glt):
    B, _, S = qt.shape
    ncp = S // CMP_STRIDE
    ns = S // SLC_BLOCK
    G = NSA_KV_GROUPS
    return pl.pallas_call(
        functools.partial(_nsa_kernel, seq_len=S),
        grid=(B, S // Q_BLOCK),
        in_specs=[pl.BlockSpec((1, NSA_WIDTH, Q_BLOCK), lambda b, i: (b, 0, i)),
                  pl.BlockSpec((1, S, 2 * LANE), lambda b, i: (b, 0, 0)),
                  pl.BlockSpec((1, 2 * LANE, S), lambda b, i: (b, 0, 0)),
                  pl.BlockSpec((1, G, ncp, HEAD_DIM), lambda b, i: (b, 0, 0, 0)),
                  pl.BlockSpec((1, G, HEAD_DIM, ncp), lambda b, i: (b, 0, 0, 0)),
                  pl.BlockSpec((ns, ncp), lambda b, i: (0, 0)),
                  pl.BlockSpec((1, LANE, Q_BLOCK), lambda b, i: (b, 0, i))],
        out_specs=pl.BlockSpec((1, NSA_WIDTH, Q_BLOCK), lambda b, i: (b, 0, i)),
        out_shape=jax.ShapeDtypeStruct((B, NSA_WIDTH, S), F32),
        scratch_shapes=[pltpu.VMEM((G, ns, Q_BLOCK), F32)],
        compiler_params=_params("parallel", "arbitrary"),
        name="nsa",
    )(qt, ksw, vt, kcmp, vcmpt, overlap_t, glt)


def _rwkv_prep_kernel(f_ref, p_ref, mu_ref, w0_ref, w2_ref, a0_ref, a2_ref, g2_ref, kk_ref, ka_ref,
                      bd_ref, r_o, lw_o, k_o, v_o, kk_o, a_o, g_o):
    W = RWKV_WIDTH
    i = pl.program_id(1)
    f = f_ref[0]
    tm = f.shape[0]
    prev_last = jnp.where(i > 0, p_ref[0, 7:8, :], 0.0)
    rolled = pltpu.roll(f, 1, 0)
    row = lax.broadcasted_iota(I32, (tm, 1), 0)
    prev = jnp.where(row == 0, prev_last, rolled)
    f = f + (prev - f) * mu_ref[...]
    r, k, v = f[:, 0:W], f[:, W:2 * W], f[:, 2 * W:3 * W]
    wd = f[:, 3 * W:3 * W + LANE]
    ad = f[:, 3 * W + LANE:3 * W + 2 * LANE]
    gd = f[:, 3 * W + 2 * LANE:3 * W + 3 * LANE]
    lw_o[0] = -DECAY_SCALE * jax.nn.sigmoid(w0_ref[...] + _dot3(jnp.tanh(wd), w2_ref[...]))
    a = jax.nn.sigmoid(a0_ref[...] + _dot3(ad, a2_ref[...]))
    g_o[0] = _dot3(jax.nn.sigmoid(gd), g2_ref[...])
    kk = k * kk_ref[...]
    ss = _dot_lhs2(kk * kk, bd_ref[...])
    kk_o[0] = kk / jnp.maximum(jnp.sqrt(ss), 1e-12)
    k_o[0] = k * (1.0 + (a - 1.0) * ka_ref[...])
    r_o[0] = r
    v_o[0] = v
    a_o[0] = a


def _rwkv_prep(rw, mu, w0, w2, a0, a2, g2, k_k, k_a, bd):
    B, S, C = rw.shape
    W = RWKV_WIDTH
    tm = min(256, S)
    row = lambda n: pl.BlockSpec((1, n), lambda b, i: (0, 0))
    mat = pl.BlockSpec((LANE, W), lambda b, i: (0, 0))
    tok = pl.BlockSpec((1, tm, W), lambda b, i: (b, i, 0))
    return pl.pallas_call(
        _rwkv_prep_kernel,
        grid=(B, S // tm),
        in_specs=[pl.BlockSpec((1, tm, C), lambda b, i: (b, i, 0)),
                  pl.BlockSpec((1, 8, C), lambda b, i: (b, jnp.maximum(i * (tm // 8) - 1, 0), 0)),
                  row(C), row(W), mat, row(W), mat, mat, row(W), row(W),
                  pl.BlockSpec((W, W), lambda b, i: (0, 0))],
        out_specs=[tok] * 7,
        out_shape=[jax.ShapeDtypeStruct((B, S, W), F32)] * 7,
        compiler_params=_params("parallel", "arbitrary"),
        name="rwkv_prep",
    )(rw, rw, mu, w0, w2, a0, a2, g2, k_k, k_a, bd)


def _rwkv_scan_kernel(r_ref, lw_ref, k_ref, v_ref, kk_ref, a_ref, g_ref, lnw_ref, lnb_ref, rk_ref,
                      o_ref, st_ref):
    Dh = HEAD_DIM

    @pl.when(pl.program_id(1) == 0)
    def _():
        st_ref[...] = jnp.zeros_like(st_ref)

    NB, C = r_ref.shape[0], r_ref.shape[1]
    ri = lax.broadcasted_iota(I32, (C, C), 0)
    ci = lax.broadcasted_iota(I32, (C, C), 1)
    incl = ri >= ci
    strict = ri > ci
    tri = incl.astype(BF16)
    eye = (ri == ci).astype(F32)
    n_double = max(int(math.log2(C)) - 1, 0)

    def scaled_operands(bb):
        r, lw, k, kk, a = (ref[bb] for ref in (r_ref, lw_ref, k_ref, kk_ref, a_ref))
        l_hi = lw.astype(BF16)
        l_r1 = lw - l_hi.astype(F32)
        l_mid = l_r1.astype(BF16)
        l_lo = (l_r1 - l_mid.astype(F32)).astype(BF16)
        L = (jnp.dot(tri, l_hi, preferred_element_type=F32) + jnp.dot(tri, l_mid, preferred_element_type=F32)
             + jnp.dot(tri, l_lo, preferred_element_type=F32))
        e_l = jnp.exp(L)
        e_inv = jnp.exp(-L)
        e_end = jnp.exp(L[C - 1:C, :] - L)
        b = kk * a
        x1 = jnp.concatenate([kk * jnp.exp(L - lw), r * e_l], axis=0).astype(BF16)
        x2t = jnp.concatenate([k * e_inv, b * e_inv], axis=0).T.astype(BF16)
        zt = jnp.concatenate([k * e_end, -(b * e_end)], axis=0).T.astype(BF16)
        gt = jnp.concatenate([e_l, e_l], axis=0).T
        return x1, x2t, zt, gt

    ops = [scaled_operands(bb) for bb in range(NB)]
    chains = [(bb, h) for bb in range(NB) for h in range(RWKV_HEADS)]
    sls = [slice(h * Dh, (h + 1) * Dh) for _, h in chains]
    h0s = [st_ref[bb, h] for bb, h in chains]
    vhs = [v_ref[bb][:, sl] for (bb, _), sl in zip(chains, sls)]
    x1s = [ops[bb][0][:, sl] for (bb, _), sl in zip(chains, sls)]
    amats = [jnp.dot(x1, ops[bb][1][sl, :], preferred_element_type=F32)
             for x1, (bb, _), sl in zip(x1s, chains, sls)]
    p0s = [jnp.dot(x1, h0.astype(BF16), preferred_element_type=F32) for x1, h0 in zip(x1s, h0s)]
    a_kk = [jnp.where(strict, am[0:C, 0:C], 0.0).astype(BF16) for am in amats]
    a_rkb = [jnp.concatenate([jnp.where(incl, am[C:2 * C, 0:C], 0.0), jnp.where(incl, -am[C:2 * C, C:2 * C], 0.0)],
                             axis=1).astype(BF16) for am in amats]
    nmats = [jnp.where(strict, -am[0:C, C:2 * C], 0.0) for am in amats]
    tinvs = [eye + nm for nm in nmats]
    for _ in range(n_double):
        nmats = [_bdot(nm, nm) for nm in nmats]
        tinvs = [ti + _bdot(ti, nm) for ti, nm in zip(tinvs, nmats)]
    akv = [jnp.dot(ak, vh.astype(BF16), preferred_element_type=F32) for ak, vh in zip(a_kk, vhs)]
    us = [_bdot(ti, p0[0:C] + av) for ti, p0, av in zip(tinvs, p0s, akv)]
    vus = [jnp.concatenate([vh, u], axis=0).astype(BF16) for vh, u in zip(vhs, us)]
    ys = [p0[C:2 * C] + jnp.dot(ar, vu, preferred_element_type=F32) for p0, ar, vu in zip(p0s, a_rkb, vus)]
    for i, (bb, h) in enumerate(chains):
        _, _, zt, gt = ops[bb]
        st_ref[bb, h] = gt[sls[i], C - 1:C] * h0s[i] + jnp.dot(zt[sls[i], :], vus[i], preferred_element_type=F32)
    for i, (bb, h) in enumerate(chains):
        sl, y = sls[i], ys[i]
        mean = jnp.mean(y, axis=-1, keepdims=True)
        yc = y - mean
        var = jnp.mean(yc * yc, axis=-1, keepdims=True)
        yn = yc * lax.rsqrt(var + GN_EPS) * lnw_ref[:, sl] + lnb_ref[:, sl]
        bonus = jnp.sum(r_ref[bb][:, sl] * k_ref[bb][:, sl] * rk_ref[:, sl], axis=-1, keepdims=True) * vhs[i]
        o_ref[bb, :, sl] = (yn + bonus) * g_ref[bb][:, sl]


def _rwkv_scan(r, lw, k, v, kk, a, g, ln_w, ln_b, r_k):
    B, S, W = r.shape
    C = min(RWKV_CHUNK, S)
    nb = math.gcd(RWKV_BATCH_ROWS, B)
    tok = pl.BlockSpec((nb, C, W), lambda b, i: (b, i, 0))
    row = pl.BlockSpec((1, W), lambda b, i: (0, 0))
    return pl.pallas_call(
        _rwkv_scan_kernel,
        grid=(B // nb, S // C),
        in_specs=[tok] * 7 + [row] * 3,
        out_specs=tok,
        out_shape=jax.ShapeDtypeStruct((B, S, W), F32),
        scratch_shapes=[pltpu.VMEM((nb, RWKV_HEADS, HEAD_DIM, HEAD_DIM), F32)],
        compiler_params=_params("parallel", "arbitrary"),
        name="rwkv_scan",
    )(r, lw, k, v, kk, a, g, ln_w, ln_b, r_k)


def _rms(x, g):
    return x * lax.rsqrt(jnp.mean(x * x, axis=-1, keepdims=True) + RMS_EPS) * g


def _outproj_kernel(on_ref, or_ref, x_ref, gn_ref, wt_ref, wb_ref, gpost_ref, gm_ref, gpre_ref,
                    scf_ref, shf_ref, rw_ref, rb_ref, x1_ref, h2_ref, idx_ref, wgt_ref):
    ont = on_ref[0]
    ont = ont * lax.rsqrt(jnp.mean(ont * ont, axis=0, keepdims=True) + RMS_EPS) * gn_ref[...]
    on = ont.T
    mixed = (jnp.dot(on.astype(BF16), wt_ref[...], preferred_element_type=F32)
             + jnp.dot(or_ref[0].astype(BF16), wb_ref[...], preferred_element_type=F32))
    x1 = x_ref[0] + gm_ref[0] * _rms(mixed, gpost_ref[...])
    x1_ref[0] = x1
    h2 = _rms(x1, gpre_ref[...]) * scf_ref[0] + shf_ref[0]
    h2_ref[0] = h2.astype(BF16)
    logits = _dot3(h2, rw_ref[...]) + rb_ref[...]
    lane = lax.broadcasted_iota(I32, logits.shape, 1)
    vals = logits
    top_v, top_i = [], []
    for _ in range(TOP_K):
        mx = jnp.max(vals, axis=-1, keepdims=True)
        ix = jnp.min(jnp.where(vals == mx, lane, LANE), axis=-1, keepdims=True)
        top_v.append(mx)
        top_i.append(ix)
        vals = jnp.where(lane == ix, LOWEST, vals)
    ex = [jnp.exp(tv - top_v[0]) for tv in top_v]
    inv = 1.0 / (ex[0] + ex[1] + ex[2] + ex[3])
    idx = jnp.zeros(logits.shape, I32)
    wgt = jnp.zeros(logits.shape, F32)
    for kk in range(TOP_K):
        idx = jnp.where(lane == kk, top_i[kk], idx)
        wgt = jnp.where(lane == kk, ex[kk] * inv, wgt)
    idx_ref[0] = idx
    wgt_ref[0] = wgt


def _outproj(o_nsa, o_rwkv, x, gn, w_top, w_bot, g_post, g_m, g_pre, sc1_f, sh_f, router_w, router_b):
    B, S, D = x.shape
    tm = min(256, S)
    W = o_rwkv.shape[-1]
    tok = lambda n: pl.BlockSpec((1, tm, n), lambda b, i: (b, i, 0))
    vec = pl.BlockSpec((1, 1, D), lambda b, i: (b, 0, 0))
    row = lambda n: pl.BlockSpec((1, n), lambda b, i: (0, 0))
    return pl.pallas_call(
        _outproj_kernel,
        grid=(B, S // tm),
        in_specs=[pl.BlockSpec((1, W, tm), lambda b, i: (b, 0, i)), tok(W), tok(D),
                  pl.BlockSpec((W, 1), lambda b, i: (0, 0)),
                  pl.BlockSpec((W, D), lambda b, i: (0, 0)), pl.BlockSpec((W, D), lambda b, i: (0, 0)),
                  row(D), vec, row(D), vec, vec,
                  pl.BlockSpec((D, LANE), lambda b, i: (0, 0)), row(LANE)],
        out_specs=[tok(D), tok(D), tok(LANE), tok(LANE)],
        out_shape=[jax.ShapeDtypeStruct((B, S, D), F32), jax.ShapeDtypeStruct((B, S, D), BF16),
                   jax.ShapeDtypeStruct((B, S, LANE), I32), jax.ShapeDtypeStruct((B, S, LANE), F32)],
        compiler_params=_params("parallel", "arbitrary"),
        name="outproj",
    )(o_nsa, o_rwkv, x, gn, w_top, w_bot, g_post, g_m, g_pre, sc1_f, sh_f, router_w, router_b)


def _slot_runs_kernel(start_ref, nvalid_ref, tok_ref, o_ref):
    rows = MOE_BLOCK // LANE
    lane = lax.broadcasted_iota(I32, (rows, LANE), 1)
    within = lax.broadcasted_iota(I32, (rows, LANE), 0) * LANE + lane

    def body(b, carry):
        s = start_ref[b]
        off = jnp.bitwise_and(s, LANE - 1)
        win = tok_ref[pl.ds(lax.shift_right_logical(s, 7), rows + 1), :]
        win = pltpu.roll(win, jnp.bitwise_and(LANE - off, LANE - 1), 1)
        run = jnp.where(lane < LANE - off, win[0:rows], win[1:rows + 1])
        o_ref[pl.ds(pl.multiple_of(b * rows, rows), rows), :] = jnp.where(within < nvalid_ref[b], run, 0)
        return carry

    lax.fori_loop(0, start_ref.shape[0], body, 0)


def _slot_runs(run_start, nvalid, tok_sorted):
    nb = run_start.shape[0]
    rows = MOE_BLOCK // LANE
    n_rows = -(-(tok_sorted.shape[0] // LANE + rows + 1) // 8) * 8
    tok2d = jnp.pad(tok_sorted, (0, n_rows * LANE - tok_sorted.shape[0])).reshape(n_rows, LANE)
    grid_spec = pltpu.PrefetchScalarGridSpec(
        num_scalar_prefetch=2,
        grid=(1,),
        in_specs=[pl.BlockSpec((n_rows, LANE), lambda i, s, n: (0, 0))],
        out_specs=pl.BlockSpec((nb * rows, LANE), lambda i, s, n: (0, 0)),
    )
    out = pl.pallas_call(
        _slot_runs_kernel,
        grid_spec=grid_spec,
        out_shape=jax.ShapeDtypeStruct((nb * rows, LANE), I32),
        compiler_params=_params("arbitrary"),
        name="slot_runs",
    )(run_start, nvalid, tok2d)
    return out.reshape(nb * MOE_BLOCK)


def _expert_kernel(be_ref, nu_ref, x_ref, wgu_ref, bgu_ref, wd_ref, bd_ref, o_ref, wgu_bf, wd_bf):
    F = wd_ref.shape[1]
    i = pl.program_id(0)
    active = i < nu_ref[0]
    new_expert = (i == 0) | (be_ref[i] != be_ref[jnp.maximum(i - 1, 0)])

    @pl.when(active & new_expert)
    def _():
        wgu_bf[...] = wgu_ref[0].astype(BF16)
        wd_bf[...] = wd_ref[0].astype(BF16)

    @pl.when(active)
    def _():
        gu = jnp.dot(x_ref[...], wgu_bf[...], preferred_element_type=F32) + bgu_ref[0]
        gate = jnp.minimum(gu[:, 0:F], SWIGLU_LIMIT)
        up = jnp.clip(gu[:, F:2 * F], -SWIGLU_LIMIT, SWIGLU_LIMIT)
        glu = gate * jax.nn.sigmoid(gate * SWIGLU_ALPHA)
        out = jnp.dot(((up + 1.0) * glu).astype(BF16), wd_bf[...], preferred_element_type=F32) + bd_ref[0]
        o_ref[...] = out.astype(o_ref.dtype)

    @pl.when(jnp.logical_not(active))
    def _():
        o_ref[...] = jnp.zeros_like(o_ref)


def _experts(blk_expert, n_used, x_sorted, w_gu, b_gu, w_d, b_d):
    P, D = x_sorted.shape
    E, _, F2 = w_gu.shape
    F = F2 // 2
    nb = P // MOE_BLOCK
    grid_spec = pltpu.PrefetchScalarGridSpec(
        num_scalar_prefetch=2,
        grid=(nb,),
        in_specs=[pl.BlockSpec((MOE_BLOCK, D), lambda i, be, nu: (i, 0)),
                  pl.BlockSpec((1, D, F2), lambda i, be, nu: (be[i], 0, 0)),
                  pl.BlockSpec((1, 1, F2), lambda i, be, nu: (be[i], 0, 0)),
                  pl.BlockSpec((1, F, D), lambda i, be, nu: (be[i], 0, 0)),
                  pl.BlockSpec((1, 1, D), lambda i, be, nu: (be[i], 0, 0))],
        out_specs=pl.BlockSpec((MOE_BLOCK, D), lambda i, be, nu: (i, 0)),
        scratch_shapes=[pltpu.VMEM((D, F2), BF16), pltpu.VMEM((F, D), BF16)],
    )
    return pl.pallas_call(
        _expert_kernel,
        grid_spec=grid_spec,
        out_shape=jax.ShapeDtypeStruct((P, D), F32),
        compiler_params=_params("arbitrary"),
        name="experts",
    )(blk_expert, n_used, x_sorted, w_gu, b_gu.reshape(E, 1, F2), w_d, b_d.reshape(E, 1, D))


def _combine_kernel(y_ref, w_ref, x1_ref, gpost_ref, gf_ref, o_ref):
    w = w_ref[0]
    y = (w[:, 0:1] * y_ref[0].astype(F32) + w[:, 1:2] * y_ref[1].astype(F32)
         + w[:, 2:3] * y_ref[2].astype(F32) + w[:, 3:4] * y_ref[3].astype(F32))
    o_ref[0] = x1_ref[0] + gf_ref[0] * _rms(y, gpost_ref[...])


def _combine(y4, wgt, x1, g_post, g_f):
    B, S, D = x1.shape
    tm = min(256, S)
    nt = S // tm
    return pl.pallas_call(
        _combine_kernel,
        grid=(B, nt),
        in_specs=[pl.BlockSpec((TOP_K, tm, D), lambda b, i: (0, b * nt + i, 0)),
                  pl.BlockSpec((1, tm, LANE), lambda b, i: (b, i, 0)),
                  pl.BlockSpec((1, tm, D), lambda b, i: (b, i, 0)),
                  pl.BlockSpec((1, D), lambda b, i: (0, 0)),
                  pl.BlockSpec((1, 1, D), lambda b, i: (b, 0, 0))],
        out_specs=pl.BlockSpec((1, tm, D), lambda b, i: (b, i, 0)),
        out_shape=jax.ShapeDtypeStruct((B, S, D), F32),
        compiler_params=_params("parallel", "arbitrary"),
        name="combine",
    )(y4, wgt, x1, g_post, g_f)


def _rot_cols(w):
    d, n = w.shape
    w4 = w.reshape(d, n // HEAD_DIM, 2, HEAD_DIM // 2)
    return jnp.concatenate([-w4[:, :, 1:2], w4[:, :, 0:1]], axis=2).reshape(d, n)


def _pad_cols(w, n):
    return jnp.pad(w, ((0, 0), (0, n - w.shape[1])))


def _pad_rows(w, n):
    return jnp.pad(w, ((0, n - w.shape[0]), (0, 0)))


def _layer(x, ada, l, mix_pre_norm, mix_post_norm, ffn_pre_norm, ffn_post_norm, w_in, cmp_k_pe, cmp_k_w1,
           cmp_k_w2, cmp_v_pe, cmp_v_w1, cmp_v_w2, nsa_out_norm, rwkv_mu, rwkv_w0, rwkv_w2, rwkv_a0, rwkv_a2,
           rwkv_g2, rwkv_k_k, rwkv_k_a, rwkv_r_k, rwkv_ln_w, rwkv_ln_b, w_out, router_w, router_b,
           expert_w_gate_up, expert_b_gate_up, expert_w_down, expert_b_down):
    B, S, D = x.shape
    T = B * S
    G, Dh, W = NSA_KV_GROUPS, HEAD_DIM, RWKV_WIDTH
    sh_m, sc_m, g_m, sh_f, sc_f, g_f = [a.reshape(B, 1, D) for a in jnp.split(ada, 6, axis=-1)]

    wi = w_in[l]
    KV = NSA_KV_WIDTH
    o = NSA_WIDTH
    wq, wkc, wvc, wks, wvs, wkw, wvw = (wi[:, 0:o],) + tuple(wi[:, o + j * KV:o + (j + 1) * KV] for j in range(6))
    wgl = wi[:, o + 6 * KV:o + 6 * KV + 3 * NSA_HEADS]
    wr = wi[:, o + 6 * KV + 3 * NSA_HEADS:]
    w_rope = jnp.concatenate([wq, wkc, wks, wkw], axis=1)
    w_rest = _pad_cols(jnp.concatenate([wvc, wvs, wvw, wgl], axis=1), REST_COLS)
    lora = [(3 * W, RWKV_W_LORA), (3 * W + RWKV_W_LORA, RWKV_A_LORA),
            (3 * W + RWKV_W_LORA + RWKV_A_LORA, RWKV_G_LORA)]
    pad_lora = lambda a: jnp.concatenate([_pad_cols(a[..., s:s + n], LANE) for s, n in lora], axis=-1)
    w_rw = jnp.concatenate([wr[:, 0:3 * W], pad_lora(wr)], axis=1)
    w_all = jnp.concatenate([w_rope, _rot_cols(w_rope), w_rest, w_rw], axis=1).astype(BF16)
    mu_row = rwkv_mu[l].reshape(1, -1)
    mu = jnp.concatenate([mu_row[:, 0:3 * W], pad_lora(mu_row)], axis=1)

    half = Dh // 2
    inv_freq = ROPE_THETA ** (-jnp.arange(half, dtype=F32) / half)
    ang = jnp.arange(S, dtype=F32)[:, None] * inv_freq[None, :]
    n_rope_heads = ROPE_COLS // Dh
    col_scale = jnp.where(jnp.arange(ROPE_COLS) < NSA_WIDTH, Dh ** -0.5 * math.log2(math.e), 1.0).astype(F32)
    cos_t = jnp.tile(jnp.cos(ang), (1, 2 * n_rope_heads)) * col_scale
    sin_t = jnp.tile(jnp.sin(ang), (1, 2 * n_rope_heads)) * col_scale

    qt, kc, ksw, vc, vt, glt, rw = _inproj(x, mix_pre_norm[l], 1.0 + sc_m, sh_m, w_all, cos_t, sin_t)

    nb = S // CMP_STRIDE
    pe8 = lambda pe: jnp.broadcast_to(pe.reshape(1, -1), (8, pe.size))
    kcmp = _compress(kc, cmp_k_w1[l], cmp_k_w2[l], pe8(cmp_k_pe[l]), transposed=False)
    vcmpt = _compress(vc, cmp_v_w1[l], cmp_v_w2[l].T, pe8(cmp_v_pe[l]), transposed=True)
    ns = S // SLC_BLOCK
    c0 = jnp.arange(nb)[None, :] * CMP_STRIDE
    b0 = jnp.arange(ns)[:, None] * SLC_BLOCK
    ov = jnp.maximum(jnp.minimum(c0 + CMP_BLOCK, b0 + SLC_BLOCK) - jnp.maximum(c0, b0), 0)
    overlap_t = (ov.astype(F32) / CMP_BLOCK).astype(BF16)
    o_nsa_t = _nsa(qt, ksw, vt, kcmp, vcmpt, overlap_t, glt)

    bd = (jnp.arange(W)[:, None] // Dh == jnp.arange(W)[None, :] // Dh).astype(BF16)
    row = lambda a: a.reshape(1, -1)
    r, lw, k, v, kk, a, g = _rwkv_prep(
        rw, mu, row(rwkv_w0[l]), _pad_rows(rwkv_w2[l], LANE), row(rwkv_a0[l]), _pad_rows(rwkv_a2[l], LANE),
        _pad_rows(rwkv_g2[l], LANE), row(rwkv_k_k[l]), row(rwkv_k_a[l]), bd)
    o_rwkv = _rwkv_scan(r, lw, k, v, kk, a, g, row(rwkv_ln_w[l]), row(rwkv_ln_b[l]), row(rwkv_r_k[l]))

    wo = w_out[l].astype(BF16)
    rb = jnp.concatenate([router_b[l], jnp.full((LANE - N_EXPERTS,), NEG_INF, F32)]).reshape(1, LANE)
    x1, h2, idx, wgt = _outproj(o_nsa_t, o_rwkv, x, nsa_out_norm[l].reshape(-1, 1), wo[0:NSA_WIDTH], wo[NSA_WIDTH:],
                                row(mix_post_norm[l]), g_m, row(ffn_pre_norm[l]), 1.0 + sc_f, sh_f,
                                _pad_cols(router_w[l], LANE), rb)

    n_assign = T * TOP_K
    idx4 = idx.reshape(T, LANE)[:, 0:TOP_K]
    e_ids = jnp.arange(N_EXPERTS, dtype=I32)
    hot = idx4[:, :, None] == e_ids
    onehot = hot.astype(I32).sum(axis=1)
    csum = jnp.cumsum(onehot, axis=0)
    counts = csum[-1]
    starts = jnp.cumsum(counts) - counts
    padded = ((counts + MOE_BLOCK - 1) // MOE_BLOCK) * MOE_BLOCK
    pad_ends = jnp.cumsum(padded)
    pad_starts = pad_ends - padded
    pos = jnp.where(hot, (csum - onehot + pad_starts)[:, None, :], 0).sum(axis=-1).astype(I32)
    n_blocks = -(-n_assign // MOE_BLOCK) + N_EXPERTS
    P = n_blocks * MOE_BLOCK
    blk_start = jnp.arange(n_blocks, dtype=I32) * MOE_BLOCK
    blk_expert = jnp.minimum((pad_ends[None, :] <= blk_start[:, None]).sum(axis=1), N_EXPERTS - 1).astype(I32)
    n_used = (pad_ends[-1] // MOE_BLOCK).astype(I32).reshape(1)
    order = jnp.argsort(idx4.reshape(-1))
    blk_off = blk_start - pad_starts[blk_expert]
    run_start = jnp.minimum(starts[blk_expert] + blk_off, n_assign).astype(I32)
    nvalid = jnp.clip(counts[blk_expert] - blk_off, 0, MOE_BLOCK).astype(I32)
    slot_tok = _slot_runs(run_start, nvalid, (order // TOP_K).astype(I32))

    h2_rows = jnp.concatenate([h2.reshape(T, D), jnp.zeros((1, D), BF16)], axis=0)
    x_sorted = h2_rows[slot_tok]
    y_sorted = _experts(blk_expert, n_used, x_sorted, expert_w_gate_up[l], expert_b_gate_up[l],
                        expert_w_down[l], expert_b_down[l])
    y4 = y_sorted[pos.T]
    return _combine(y4, wgt, x1, row(ffn_post_norm[l]), g_f)


def kernel(x, c, ada_w, ada_b, mix_pre_norm, mix_post_norm, ffn_pre_norm, ffn_post_norm, w_in, cmp_k_pe, cmp_k_w1, cmp_k_w2, cmp_v_pe, cmp_v_w1, cmp_v_w2, nsa_out_norm, rwkv_mu, rwkv_w0, rwkv_w2, rwkv_a0, rwkv_a2, rwkv_g2, rwkv_k_k, rwkv_k_a, rwkv_r_k, rwkv_ln_w, rwkv_ln_b, w_out, router_w, router_b, expert_w_gate_up, expert_b_gate_up, expert_w_down, expert_b_down):
    for l in range(ada_w.shape[0]):
        ada = _ada(c, ada_w[l], ada_b[l])
        x = _layer(x, ada, l, mix_pre_norm, mix_post_norm, ffn_pre_norm, ffn_post_norm, w_in, cmp_k_pe,
                   cmp_k_w1, cmp_k_w2, cmp_v_pe, cmp_v_w1, cmp_v_w2, nsa_out_norm, rwkv_mu, rwkv_w0, rwkv_w2,
                   rwkv_a0, rwkv_a2, rwkv_g2, rwkv_k_k, rwkv_k_a, rwkv_r_k, rwkv_ln_w, rwkv_ln_b, w_out,
                   router_w, router_b, expert_w_gate_up, expert_b_gate_up, expert_w_down, expert_b_down)
    return x
```

```python
import functools
import math

import jax
import jax.numpy as jnp
from jax import lax
from jax.experimental import pallas as pl
from jax.experimental.pallas import tpu as pltpu

F32 = jnp.float32
BF16 = jnp.bfloat16
I32 = jnp.int32

HEAD_DIM = 64
NSA_HEADS = 8
NSA_KV_GROUPS = 2
NSA_HPG = NSA_HEADS // NSA_KV_GROUPS
NSA_WIDTH = NSA_HEADS * HEAD_DIM
NSA_KV_WIDTH = NSA_KV_GROUPS * HEAD_DIM
CMP_BLOCK = 32
CMP_STRIDE = 16
SLC_BLOCK = 64
SLC_TOPK = 16
WINDOW = 512
Q_BLOCK = 128
RWKV_HEADS = 8
RWKV_WIDTH = RWKV_HEADS * HEAD_DIM
RWKV_W_LORA = 32
RWKV_A_LORA = 32
RWKV_G_LORA = 96
DECAY_SCALE = math.exp(-0.5)
GN_EPS = 64e-5
N_EXPERTS = 32
TOP_K = 4
SWIGLU_LIMIT = 7.0
SWIGLU_ALPHA = 1.702
ROPE_THETA = 10000.0
RMS_EPS = 1e-6
NEG_INF = -1e30
FORCED = 1e9
LOWEST = -3e38

LANE = 128
ROPE_COLS = NSA_WIDTH + 3 * NSA_KV_WIDTH
REST_COLS = 4 * LANE
RW_COLS = 3 * RWKV_WIDTH + 3 * LANE
SLC_KT = 512
SLC_UNROLL = 1
FLASH_LANES = 1024
FLASH_DEPTH = 1
ACC_ROWS = HEAD_DIM + 16
RWKV_CHUNK = 128
RWKV_BATCH_ROWS = 1
MOE_BLOCK = 512
VMEM_LIMIT = 56 * 1024 * 1024


def _bdot(a, b):
    return jnp.dot(a.astype(BF16), b.astype(BF16), preferred_element_type=F32)


def _split2(a):
    hi = a.astype(BF16)
    lo = (a - hi.astype(F32)).astype(BF16)
    return hi, lo


def _dot_lhs2(a, b_bf16):
    hi, lo = _split2(a)
    return (jnp.dot(hi, b_bf16, preferred_element_type=F32)
            + jnp.dot(lo, b_bf16, preferred_element_type=F32))


def _dot3(a, b):
    ah, al = _split2(a)
    bh, bl = _split2(b)
    return (jnp.dot(ah, bh, preferred_element_type=F32)
            + jnp.dot(al, bh, preferred_element_type=F32)
            + jnp.dot(ah, bl, preferred_element_type=F32))


def _params(*sem, flags=None):
    return pltpu.CompilerParams(dimension_semantics=sem, vmem_limit_bytes=VMEM_LIMIT, flags=flags)


def _ada_kernel(c_ref, w_ref, b_ref, o_ref):
    c = c_ref[...]
    act = c * jax.nn.sigmoid(c)
    o_ref[...] = _dot3(act, w_ref[...]) + b_ref[...]


def _ada(c, w, b):
    B, D = c.shape
    N = w.shape[1]
    tn = 1536
    return pl.pallas_call(
        _ada_kernel,
        grid=(N // tn,),
        in_specs=[pl.BlockSpec((B, D), lambda j: (0, 0)),
                  pl.BlockSpec((D, tn), lambda j: (0, j)),
                  pl.BlockSpec((1, tn), lambda j: (0, j))],
        out_specs=pl.BlockSpec((B, tn), lambda j: (0, j)),
        out_shape=jax.ShapeDtypeStruct((B, N), F32),
        compiler_params=_params("arbitrary"),
        name="ada",
    )(c, w, b.reshape(1, N))


def _inproj_kernel(x_ref, g_ref, sc_ref, sh_ref, w_ref, cos_ref, sin_ref,
                   q_ref, kc_ref, ksw_ref, vc_ref, vsw_ref, gl_ref, rw_ref):
    x = x_ref[0]
    ms = jnp.mean(x * x, axis=-1, keepdims=True)
    h = x * lax.rsqrt(ms + RMS_EPS) * g_ref[...]
    h = h * sc_ref[0] + sh_ref[0]
    hb = h.astype(BF16)
    R = ROPE_COLS
    main = jnp.dot(hb, w_ref[:, 0:R], preferred_element_type=F32)
    rot = jnp.dot(hb, w_ref[:, R:2 * R], preferred_element_type=F32)
    roped = main * cos_ref[...] + rot * sin_ref[...]
    q_ref[0] = roped[:, 0:NSA_WIDTH].T.astype(BF16)
    kc_ref[0] = roped[:, NSA_WIDTH:NSA_WIDTH + LANE]
    ksw_ref[0] = roped[:, NSA_WIDTH + LANE:R].astype(BF16)
    rest = jnp.dot(hb, w_ref[:, 2 * R:2 * R + REST_COLS], preferred_element_type=F32)
    vc_ref[0] = rest[:, 0:LANE]
    vsw_ref[0] = rest[:, LANE:3 * LANE].T.astype(BF16)
    gl_ref[0] = rest[:, 3 * LANE:4 * LANE].T
    rw_ref[0] = jnp.dot(hb, w_ref[:, 2 * R + REST_COLS:], preferred_element_type=F32)


def _inproj(x, gain, sc1, sh, w_all, cos_t, sin_t):
    B, S, D = x.shape
    tm = min(256, S)
    NW = w_all.shape[1]
    R = ROPE_COLS
    tok = lambda n: pl.BlockSpec((1, tm, n), lambda b, i: (b, i, 0))
    tok_t = lambda n: pl.BlockSpec((1, n, tm), lambda b, i: (b, 0, i))
    vec = pl.BlockSpec((1, 1, D), lambda b, i: (b, 0, 0))
    outs = [(NSA_WIDTH, BF16, True), (LANE, F32, False), (2 * LANE, BF16, False), (LANE, F32, False),
            (2 * LANE, BF16, True), (LANE, F32, True), (RW_COLS, F32, False)]
    return pl.pallas_call(
        _inproj_kernel,
        grid=(B, S // tm),
        in_specs=[tok(D), pl.BlockSpec((1, D), lambda b, i: (0, 0)), vec, vec,
                  pl.BlockSpec((D, NW), lambda b, i: (0, 0)),
                  pl.BlockSpec((tm, R), lambda b, i: (i, 0)),
                  pl.BlockSpec((tm, R), lambda b, i: (i, 0))],
        out_specs=[tok_t(n) if tr else tok(n) for n, _, tr in outs],
        out_shape=[jax.ShapeDtypeStruct((B, n, S) if tr else (B, S, n), dt) for n, dt, tr in outs],
        compiler_params=_params("parallel", "arbitrary"),
        name="inproj",
    )(x, gain.reshape(1, D), sc1, sh, w_all, cos_t, sin_t)


def _gelu_tanh(x):
    return 0.5 * x * (1.0 + jnp.tanh(math.sqrt(2.0 / math.pi) * (x + 0.044715 * (x * x * x))))


def _compress_kernel(r_ref, w1g_ref, w1_ref, w2_ref, pe_ref, o_ref, *, transposed):
    R = r_ref[0].astype(BF16)
    nb = R.shape[0]
    top = jnp.dot(R, w1g_ref[0, 0].astype(BF16), preferred_element_type=F32)
    bot = jnp.dot(R, w1g_ref[0, 1].astype(BF16), preferred_element_type=F32)
    pe_term = jnp.dot(pe_ref[...].astype(BF16), w1_ref[...].astype(BF16), preferred_element_type=F32)[0:1]
    bot_next = pltpu.roll(bot, nb - 1, 0)
    hid = _gelu_tanh(top + bot_next + pe_term)
    if transposed:
        out = jnp.dot(w2_ref[...].astype(BF16), hid.T.astype(BF16), preferred_element_type=F32)
    else:
        out = jnp.dot(hid.astype(BF16), w2_ref[...].astype(BF16), preferred_element_type=F32)
    o_ref[0, 0] = out.astype(BF16)


def _compress(tok, w1, w2, pe, transposed):
    B, S, GD = tok.shape
    G, Dh = NSA_KV_GROUPS, HEAD_DIM
    nb = S // CMP_STRIDE
    hid = w1.shape[-1]
    width = CMP_STRIDE * GD
    rr = tok.reshape(B, nb, width)
    w1r = w1.reshape(2, CMP_STRIDE, Dh, hid)
    w1g = jnp.zeros((G, 2, CMP_STRIDE, G, Dh, hid), w1.dtype)
    for g in range(G):
        w1g = w1g.at[g, :, :, g].set(w1r)
    w1g = w1g.reshape(G, 2, width, hid)
    oshape = (Dh, nb) if transposed else (nb, Dh)
    return pl.pallas_call(
        functools.partial(_compress_kernel, transposed=transposed),
        grid=(B, G),
        in_specs=[pl.BlockSpec((1, nb, width), lambda b, g: (b, 0, 0)),
                  pl.BlockSpec((1, 2, width, hid), lambda b, g: (g, 0, 0, 0)),
                  pl.BlockSpec(w1.shape, lambda b, g: (0, 0)),
                  pl.BlockSpec(w2.shape, lambda b, g: (0, 0)),
                  pl.BlockSpec(pe.shape, lambda b, g: (0, 0))],
        out_specs=pl.BlockSpec((1, 1) + oshape, lambda b, g: (b, g, 0, 0)),
        out_shape=jax.ShapeDtypeStruct((B, G) + oshape, BF16),
        compiler_params=_params("arbitrary", "arbitrary"),
        name="compress_v" if transposed else "compress_k",
    )(rr, w1g, w1, w2, pe)


def _lanes4(a):
    return jnp.concatenate([a, a, a, a], axis=1)


def _nsa_kernel(qt_ref, ksw_ref, vt_ref, kcmp_ref, vcmpt_ref, ovt_ref, glt_ref, o_ref, selb_ref, *, seq_len):
    QB, Dh, HPG = Q_BLOCK, HEAD_DIM, NSA_HPG
    qi = pl.program_id(1)
    s0 = qi * QB
    ns = seq_len // SLC_BLOCK
    ncp = seq_len // CMP_STRIDE
    n_sel = min(SLC_TOPK, ns)
    kt_slc = min(SLC_KT, seq_len)
    t_row = s0 + lax.broadcasted_iota(I32, (1, QB), 1)
    t4 = _lanes4(t_row)
    gates = jax.nn.sigmoid(glt_ref[0])
    blk = lax.broadcasted_iota(I32, (ns, QB), 0)
    cur = lax.shift_right_logical(t_row, 6)
    forced = (blk == 0) | (blk == cur) | (blk == cur - 1)
    future = blk * SLC_BLOCK > t_row
    cmp_last = lax.broadcasted_iota(I32, (ncp, 1), 0) * CMP_STRIDE + (CMP_BLOCK - 1)
    cmask = cmp_last <= t4
    zeros_q = jnp.zeros((Dh, HPG * QB), BF16)
    G = NSA_KV_GROUPS
    NL = G * HPG * QB
    qpads, o_cmps, imps = [], [], []

    for g in range(G):
        qg = jnp.concatenate(
            [qt_ref[0, (g * HPG + h) * Dh:(g * HPG + h + 1) * Dh, :] for h in range(HPG)], axis=1)
        qpads.append(jnp.concatenate([qg, zeros_q] if g == 0 else [zeros_q, qg], axis=0))
        sc = jnp.where(cmask, jnp.dot(kcmp_ref[0, g], qg, preferred_element_type=F32), NEG_INF)
        m = jnp.max(sc, axis=0, keepdims=True)
        p = jnp.where(cmask, jnp.exp2(sc - m), 0.0)
        l = jnp.sum(p, axis=0, keepdims=True)
        p = p * (1.0 / jnp.maximum(l, 1e-30))
        o_cmps.append(jnp.dot(vcmpt_ref[0, g], p.astype(BF16), preferred_element_type=F32))
        psum = p[:, 0:QB] + p[:, QB:2 * QB] + p[:, 2 * QB:3 * QB] + p[:, 3 * QB:4 * QB]
        p_hi, p_lo = _split2(psum)
        imp = (jnp.dot(ovt_ref[...], p_hi, preferred_element_type=F32)
               + jnp.dot(ovt_ref[...], p_lo, preferred_element_type=F32))
        imps.append(jnp.where(future, NEG_INF, jnp.where(forced, FORCED, imp)))

    vals = jnp.concatenate(imps, axis=1)
    blk2 = jnp.concatenate([blk] * G, axis=1)
    sel = jnp.zeros((ns, G * QB), F32)
    for _ in range(n_sel):
        mx = jnp.max(vals, axis=0, keepdims=True)
        idx = jnp.min(jnp.where(vals == mx, blk2, ns), axis=0, keepdims=True)
        pick = blk2 == idx
        sel = jnp.where(pick, 1.0, sel)
        vals = jnp.where(pick, LOWEST, vals)
    for g in range(G):
        selb_ref[g] = jnp.where(sel[:, g * QB:(g + 1) * QB] > 0.5, 0.0, NEG_INF)

    q_slabs = [jnp.concatenate(qpads, axis=1)[:, sb * FLASH_LANES:(sb + 1) * FLASH_LANES]
               for sb in range(NL // FLASH_LANES)]
    n_slab = len(q_slabs)

    def load_tile(k_lane0, vrow0, k0, width, bias_of_group):
        ones = jnp.ones((ACC_ROWS - Dh, width), BF16)
        kt = ksw_ref[0, pl.ds(k0, width), k_lane0:k_lane0 + LANE]
        vaug = [jnp.concatenate([vt_ref[0, vrow0 + g * Dh:vrow0 + (g + 1) * Dh, pl.ds(k0, width)], ones], axis=0)
                for g in range(G)]
        bias = [bias_of_group(g) for g in range(G)]
        return kt, vaug, bias

    group_lanes = HPG * QB
    groups_per_slab = max(FLASH_LANES // group_lanes, 1)
    lanes_per_part = FLASH_LANES // groups_per_slab

    def flash_tiles(tiles, carry):
        m, acc = list(carry[0]), list(carry[1])
        steps = [(ti, sb) for ti in range(len(tiles)) for sb in range(n_slab)]
        scores = {}
        for i in range(len(steps) + FLASH_DEPTH):
            if i < len(steps):
                ti, sb = steps[i]
                scores[i] = jnp.dot(tiles[ti][0], q_slabs[sb], preferred_element_type=F32)
            j = i - FLASH_DEPTH
            if j >= 0:
                ti, sb = steps[j]
                _, vaug, bias = tiles[ti]
                g0 = sb * FLASH_LANES // group_lanes
                slab_bias = jnp.concatenate(
                    [bias[g0 + p] for p in range(groups_per_slab) for _ in range(lanes_per_part // QB)], axis=1)
                s = scores.pop(j) + slab_bias
                m_new = jnp.maximum(m[sb], jnp.max(s, axis=0, keepdims=True))
                alpha = jnp.exp2(m[sb] - m_new)
                pj = jnp.exp2(s - m_new).astype(BF16)
                pv = [jnp.dot(vaug[g0 + p], pj[:, p * lanes_per_part:(p + 1) * lanes_per_part],
                              preferred_element_type=F32) for p in range(groups_per_slab)]
                acc[sb] = alpha * acc[sb] + (pv[0] if len(pv) == 1 else jnp.concatenate(pv, axis=1))
                m[sb] = m_new
        return tuple(m), tuple(acc)

    def slc_tile(j):
        k0 = pl.multiple_of(j * kt_slc, kt_slc)
        nblk = kt_slc // SLC_BLOCK
        causal = k0 + lax.broadcasted_iota(I32, (kt_slc, 1), 0) <= t_row

        def bias(g):
            blk_bias = jnp.concatenate(
                [jnp.broadcast_to(selb_ref[g, pl.ds(j * nblk + i, 1), :], (SLC_BLOCK, QB)) for i in range(nblk)],
                axis=0)
            return jnp.where(causal, blk_bias, NEG_INF)

        return load_tile(0, 0, k0, kt_slc, bias)

    n_all = seq_len // kt_slc
    unroll = math.gcd(SLC_UNROLL, n_all)

    def slc_body(jj, carry):
        return flash_tiles([slc_tile(jj * unroll + u) for u in range(unroll)], carry)

    init = (tuple(jnp.full((1, FLASH_LANES), NEG_INF, F32) for _ in range(n_slab)),
            tuple(jnp.zeros((ACC_ROWS, FLASH_LANES), F32) for _ in range(n_slab)))
    n_tiles = (s0 + QB - 1) // kt_slc + 1
    _, acc_s = lax.fori_loop(0, (n_tiles + unroll - 1) // unroll, slc_body, init)
    acc_s = jnp.concatenate(acc_s, axis=1)
    o_slc = acc_s[0:Dh] * (1.0 / acc_s[Dh:Dh + 1])

    win_keys = min(WINDOW + QB, seq_len)
    k0 = pl.multiple_of(jnp.maximum(qi - WINDOW // QB, 0) * QB, QB)
    kpos = k0 + lax.broadcasted_iota(I32, (win_keys, 1), 0)
    wbias = jnp.where((kpos <= t_row) & (kpos > t_row - WINDOW), 0.0, NEG_INF)
    _, acc_w = flash_tiles([load_tile(LANE, LANE, k0, win_keys, lambda g: wbias)], init)
    acc_w = jnp.concatenate(acc_w, axis=1)
    o_win = acc_w[0:Dh] * (1.0 / acc_w[Dh:Dh + 1])

    for g in range(G):
        for h in range(HPG):
            c0 = (g * HPG + h) * 3
            cols = slice((g * HPG + h) * QB, (g * HPG + h + 1) * QB)
            o = (gates[c0:c0 + 1, :] * o_cmps[g][:, h * QB:(h + 1) * QB] + gates[c0 + 1:c0 + 2, :] * o_slc[:, cols]
                 + gates[c0 + 2:c0 + 3, :] * o_win[:, cols])
            o_ref[0, (g * HPG + h) * Dh:(g * HPG + h + 1) * Dh, :] = o


def _nsa(qt, ksw, vt, kcmp, vcmpt, overlap_t, glt):
    B, _, S = qt.shape
    ncp = S // CMP_STRIDE
    ns = S // SLC_BLOCK
    G = NSA_KV_GROUPS
    return pl.pallas_call(
        functools.partial(_nsa_kernel, seq_len=S),
        grid=(B, S // Q_BLOCK),
        in_specs=[pl.BlockSpec((1, NSA_WIDTH, Q_BLOCK), lambda b, i: (b, 0, i)),
                  pl.BlockSpec((1, S, 2 * LANE), lambda b, i: (b, 0, 0)),
                  pl.BlockSpec((1, 2 * LANE, S), lambda b, i: (b, 0, 0)),
                  pl.BlockSpec((1, G, ncp, HEAD_DIM), lambda b, i: (b, 0, 0, 0)),
                  pl.BlockSpec((1, G, HEAD_DIM, ncp), lambda b, i: (b, 0, 0, 0)),
                  pl.BlockSpec((ns, ncp), lambda b, i: (0, 0)),
                  pl.BlockSpec((1, LANE, Q_BLOCK), lambda b, i: (b, 0, i))],
        out_specs=pl.BlockSpec((1, NSA_WIDTH, Q_BLOCK), lambda b, i: (b, 0, i)),
        out_shape=jax.ShapeDtypeStruct((B, NSA_WIDTH, S), F32),
        scratch_shapes=[pltpu.VMEM((G, ns, Q_BLOCK), F32)],
        compiler_params=_params("parallel", "arbitrary"),
        name="nsa",
    )(qt, ksw, vt, kcmp, vcmpt, overlap_t, glt)


def _rwkv_prep_kernel(f_ref, p_ref, mu_ref, w0_ref, w2_ref, a0_ref, a2_ref, g2_ref, kk_ref, ka_ref,
                      bd_ref, r_o, lw_o, k_o, v_o, kk_o, a_o, g_o):
    W = RWKV_WIDTH
    i = pl.program_id(1)
    f = f_ref[0]
    tm = f.shape[0]
    prev_last = jnp.where(i > 0, p_ref[0, 7:8, :], 0.0)
    rolled = pltpu.roll(f, 1, 0)
    row = lax.broadcasted_iota(I32, (tm, 1), 0)
    prev = jnp.where(row == 0, prev_last, rolled)
    f = f + (prev - f) * mu_ref[...]
    r, k, v = f[:, 0:W], f[:, W:2 * W], f[:, 2 * W:3 * W]
    wd = f[:, 3 * W:3 * W + LANE]
    ad = f[:, 3 * W + LANE:3 * W + 2 * LANE]
    gd = f[:, 3 * W + 2 * LANE:3 * W + 3 * LANE]
    lw_o[0] = -DECAY_SCALE * jax.nn.sigmoid(w0_ref[...] + _dot3(jnp.tanh(wd), w2_ref[...]))
    a = jax.nn.sigmoid(a0_ref[...] + _dot3(ad, a2_ref[...]))
    g_o[0] = _dot3(jax.nn.sigmoid(gd), g2_ref[...])
    kk = k * kk_ref[...]
    ss = _dot_lhs2(kk * kk, bd_ref[...])
    kk_o[0] = kk / jnp.maximum(jnp.sqrt(ss), 1e-12)
    k_o[0] = k * (1.0 + (a - 1.0) * ka_ref[...])
    r_o[0] = r
    v_o[0] = v
    a_o[0] = a


def _rwkv_prep(rw, mu, w0, w2, a0, a2, g2, k_k, k_a, bd):
    B, S, C = rw.shape
    W = RWKV_WIDTH
    tm = min(256, S)
    row = lambda n: pl.BlockSpec((1, n), lambda b, i: (0, 0))
    mat = pl.BlockSpec((LANE, W), lambda b, i: (0, 0))
    tok = pl.BlockSpec((1, tm, W), lambda b, i: (b, i, 0))
    return pl.pallas_call(
        _rwkv_prep_kernel,
        grid=(B, S // tm),
        in_specs=[pl.BlockSpec((1, tm, C), lambda b, i: (b, i, 0)),
                  pl.BlockSpec((1, 8, C), lambda b, i: (b, jnp.maximum(i * (tm // 8) - 1, 0), 0)),
                  row(C), row(W), mat, row(W), mat, mat, row(W), row(W),
                  pl.BlockSpec((W, W), lambda b, i: (0, 0))],
        out_specs=[tok] * 7,
        out_shape=[jax.ShapeDtypeStruct((B, S, W), F32)] * 7,
        compiler_params=_params("parallel", "arbitrary"),
        name="rwkv_prep",
    )(rw, rw, mu, w0, w2, a0, a2, g2, k_k, k_a, bd)


def _rwkv_scan_kernel(r_ref, lw_ref, k_ref, v_ref, kk_ref, a_ref, g_ref, lnw_ref, lnb_ref, rk_ref,
                      o_ref, st_ref):
    Dh = HEAD_DIM

    @pl.when(pl.program_id(1) == 0)
    def _():
        st_ref[...] = jnp.zeros_like(st_ref)

    NB, C = r_ref.shape[0], r_ref.shape[1]
    ri = lax.broadcasted_iota(I32, (C, C), 0)
    ci = lax.broadcasted_iota(I32, (C, C), 1)
    incl = ri >= ci
    strict = ri > ci
    tri = incl.astype(BF16)
    eye = (ri == ci).astype(F32)
    n_double = max(int(math.log2(C)) - 1, 0)

    def scaled_operands(bb):
        r, lw, k, kk, a = (ref[bb] for ref in (r_ref, lw_ref, k_ref, kk_ref, a_ref))
        l_hi = lw.astype(BF16)
        l_r1 = lw - l_hi.astype(F32)
        l_mid = l_r1.astype(BF16)
        l_lo = (l_r1 - l_mid.astype(F32)).astype(BF16)
        L = (jnp.dot(tri, l_hi, preferred_element_type=F32) + jnp.dot(tri, l_mid, preferred_element_type=F32)
             + jnp.dot(tri, l_lo, preferred_element_type=F32))
        e_l = jnp.exp(L)
        e_inv = jnp.exp(-L)
        e_end = jnp.exp(L[C - 1:C, :] - L)
        b = kk * a
        x1 = jnp.concatenate([kk * jnp.exp(L - lw), r * e_l], axis=0).astype(BF16)
        x2t = jnp.concatenate([k * e_inv, b * e_inv], axis=0).T.astype(BF16)
        zt = jnp.concatenate([k * e_end, -(b * e_end)], axis=0).T.astype(BF16)
        gt = jnp.concatenate([e_l, e_l], axis=0).T
        return x1, x2t, zt, gt

    ops = [scaled_operands(bb) for bb in range(NB)]
    chains = [(bb, h) for bb in range(NB) for h in range(RWKV_HEADS)]
    sls = [slice(h * Dh, (h + 1) * Dh) for _, h in chains]
    h0s = [st_ref[bb, h] for bb, h in chains]
    vhs = [v_ref[bb][:, sl] for (bb, _), sl in zip(chains, sls)]
    x1s = [ops[bb][0][:, sl] for (bb, _), sl in zip(chains, sls)]
    amats = [jnp.dot(x1, ops[bb][1][sl, :], preferred_element_type=F32)
             for x1, (bb, _), sl in zip(x1s, chains, sls)]
    p0s = [jnp.dot(x1, h0.astype(BF16), preferred_element_type=F32) for x1, h0 in zip(x1s, h0s)]
    a_kk = [jnp.where(strict, am[0:C, 0:C], 0.0).astype(BF16) for am in amats]
    a_rkb = [jnp.concatenate([jnp.where(incl, am[C:2 * C, 0:C], 0.0), jnp.where(incl, -am[C:2 * C, C:2 * C], 0.0)],
                             axis=1).astype(BF16) for am in amats]
    nmats = [jnp.where(strict, -am[0:C, C:2 * C], 0.0) for am in amats]
    tinvs = [eye + nm for nm in nmats]
    for _ in range(n_double):
        nmats = [_bdot(nm, nm) for nm in nmats]
        tinvs = [ti + _bdot(ti, nm) for ti, nm in zip(tinvs, nmats)]
    akv = [jnp.dot(ak, vh.astype(BF16), preferred_element_type=F32) for ak, vh in zip(a_kk, vhs)]
    us = [_bdot(ti, p0[0:C] + av) for ti, p0, av in zip(tinvs, p0s, akv)]
    vus = [jnp.concatenate([vh, u], axis=0).astype(BF16) for vh, u in zip(vhs, us)]
    ys = [p0[C:2 * C] + jnp.dot(ar, vu, preferred_element_type=F32) for p0, ar, vu in zip(p0s, a_rkb, vus)]
    for i, (bb, h) in enumerate(chains):
        _, _, zt, gt = ops[bb]
        st_ref[bb, h] = gt[sls[i], C - 1:C] * h0s[i] + jnp.dot(zt[sls[i], :], vus[i], preferred_element_type=F32)
    for i, (bb, h) in enumerate(chains):
        sl, y = sls[i], ys[i]
        mean = jnp.mean(y, axis=-1, keepdims=True)
        yc = y - mean
        var = jnp.mean(yc * yc, axis=-1, keepdims=True)
        yn = yc * lax.rsqrt(var + GN_EPS) * lnw_ref[:, sl] + lnb_ref[:, sl]
        bonus = jnp.sum(r_ref[bb][:, sl] * k_ref[bb][:, sl] * rk_ref[:, sl], axis=-1, keepdims=True) * vhs[i]
        o_ref[bb, :, sl] = (yn + bonus) * g_ref[bb][:, sl]


def _rwkv_scan(r, lw, k, v, kk, a, g, ln_w, ln_b, r_k):
    B, S, W = r.shape
    C = min(RWKV_CHUNK, S)
    nb = math.gcd(RWKV_BATCH_ROWS, B)
    tok = pl.BlockSpec((nb, C, W), lambda b, i: (b, i, 0))
    row = pl.BlockSpec((1, W), lambda b, i: (0, 0))
    return pl.pallas_call(
        _rwkv_scan_kernel,
        grid=(B // nb, S // C),
        in_specs=[tok] * 7 + [row] * 3,
        out_specs=tok,
        out_shape=jax.ShapeDtypeStruct((B, S, W), F32),
        scratch_shapes=[pltpu.VMEM((nb, RWKV_HEADS, HEAD_DIM, HEAD_DIM), F32)],
        compiler_params=_params("parallel", "arbitrary"),
        name="rwkv_scan",
    )(r, lw, k, v, kk, a, g, ln_w, ln_b, r_k)


def _rms(x, g):
    return x * lax.rsqrt(jnp.mean(x * x, axis=-1, keepdims=True) + RMS_EPS) * g


def _outproj_kernel(on_ref, or_ref, x_ref, gn_ref, wt_ref, wb_ref, gpost_ref, gm_ref, gpre_ref,
                    scf_ref, shf_ref, rw_ref, rb_ref, x1_ref, h2_ref, idx_ref, wgt_ref):
    ont = on_ref[0]
    ont = ont * lax.rsqrt(jnp.mean(ont * ont, axis=0, keepdims=True) + RMS_EPS) * gn_ref[...]
    on = ont.T
    mixed = (jnp.dot(on.astype(BF16), wt_ref[...], preferred_element_type=F32)
             + jnp.dot(or_ref[0].astype(BF16), wb_ref[...], preferred_element_type=F32))
    x1 = x_ref[0] + gm_ref[0] * _rms(mixed, gpost_ref[...])
    x1_ref[0] = x1
    h2 = _rms(x1, gpre_ref[...]) * scf_ref[0] + shf_ref[0]
    h2_ref[0] = h2
    logits = _dot3(h2, rw_ref[...]) + rb_ref[...]
    lane = lax.broadcasted_iota(I32, logits.shape, 1)
    vals = logits
    top_v, top_i = [], []
    for _ in range(TOP_K):
        mx = jnp.max(vals, axis=-1, keepdims=True)
        ix = jnp.min(jnp.where(vals == mx, lane, LANE), axis=-1, keepdims=True)
        top_v.append(mx)
        top_i.append(ix)
        vals = jnp.where(lane == ix, LOWEST, vals)
    ex = [jnp.exp(tv - top_v[0]) for tv in top_v]
    inv = 1.0 / (ex[0] + ex[1] + ex[2] + ex[3])
    idx = jnp.zeros(logits.shape, I32)
    wgt = jnp.zeros(logits.shape, F32)
    for kk in range(TOP_K):
        idx = jnp.where(lane == kk, top_i[kk], idx)
        wgt = jnp.where(lane == kk, ex[kk] * inv, wgt)
    idx_ref[0] = idx
    wgt_ref[0] = wgt


def _outproj(o_nsa, o_rwkv, x, gn, w_top, w_bot, g_post, g_m, g_pre, sc1_f, sh_f, router_w, router_b):
    B, S, D = x.shape
    tm = min(256, S)
    W = o_rwkv.shape[-1]
    tok = lambda n: pl.BlockSpec((1, tm, n), lambda b, i: (b, i, 0))
    vec = pl.BlockSpec((1, 1, D), lambda b, i: (b, 0, 0))
    row = lambda n: pl.BlockSpec((1, n), lambda b, i: (0, 0))
    return pl.pallas_call(
        _outproj_kernel,
        grid=(B, S // tm),
        in_specs=[pl.BlockSpec((1, W, tm), lambda b, i: (b, 0, i)), tok(W), tok(D),
                  pl.BlockSpec((W, 1), lambda b, i: (0, 0)),
                  pl.BlockSpec((W, D), lambda b, i: (0, 0)), pl.BlockSpec((W, D), lambda b, i: (0, 0)),
                  row(D), vec, row(D), vec, vec,
                  pl.BlockSpec((D, LANE), lambda b, i: (0, 0)), row(LANE)],
        out_specs=[tok(D), tok(D), tok(LANE), tok(LANE)],
        out_shape=[jax.ShapeDtypeStruct((B, S, D), F32), jax.ShapeDtypeStruct((B, S, D), F32),
                   jax.ShapeDtypeStruct((B, S, LANE), I32), jax.ShapeDtypeStruct((B, S, LANE), F32)],
        compiler_params=_params("parallel", "arbitrary"),
        name="outproj",
    )(o_nsa, o_rwkv, x, gn, w_top, w_bot, g_post, g_m, g_pre, sc1_f, sh_f, router_w, router_b)


def _slot_runs_kernel(start_ref, nvalid_ref, tok_ref, o_ref):
    rows = MOE_BLOCK // LANE
    lane = lax.broadcasted_iota(I32, (rows, LANE), 1)
    within = lax.broadcasted_iota(I32, (rows, LANE), 0) * LANE + lane

    def body(b, carry):
        s = start_ref[b]
        off = jnp.bitwise_and(s, LANE - 1)
        win = tok_ref[pl.ds(lax.shift_right_logical(s, 7), rows + 1), :]
        win = pltpu.roll(win, jnp.bitwise_and(LANE - off, LANE - 1), 1)
        run = jnp.where(lane < LANE - off, win[0:rows], win[1:rows + 1])
        o_ref[pl.ds(pl.multiple_of(b * rows, rows), rows), :] = jnp.where(within < nvalid_ref[b], run, 0)
        return carry

    lax.fori_loop(0, start_ref.shape[0], body, 0)


def _slot_runs(run_start, nvalid, tok_sorted):
    nb = run_start.shape[0]
    rows = MOE_BLOCK // LANE
    n_rows = -(-(tok_sorted.shape[0] // LANE + rows + 1) // 8) * 8
    tok2d = jnp.pad(tok_sorted, (0, n_rows * LANE - tok_sorted.shape[0])).reshape(n_rows, LANE)
    grid_spec = pltpu.PrefetchScalarGridSpec(
        num_scalar_prefetch=2,
        grid=(1,),
        in_specs=[pl.BlockSpec((n_rows, LANE), lambda i, s, n: (0, 0))],
        out_specs=pl.BlockSpec((nb * rows, LANE), lambda i, s, n: (0, 0)),
    )
    out = pl.pallas_call(
        _slot_runs_kernel,
        grid_spec=grid_spec,
        out_shape=jax.ShapeDtypeStruct((nb * rows, LANE), I32),
        compiler_params=_params("arbitrary"),
        name="slot_runs",
    )(run_start, nvalid, tok2d)
    return out.reshape(nb * MOE_BLOCK)


def _expert_kernel(be_ref, nu_ref, x_ref, wgu_ref, bgu_ref, wd_ref, bd_ref, o_ref, wgu_bf, wd_bf):
    F = wd_ref.shape[1]
    i = pl.program_id(0)
    active = i < nu_ref[0]
    new_expert = (i == 0) | (be_ref[i] != be_ref[jnp.maximum(i - 1, 0)])

    @pl.when(active & new_expert)
    def _():
        wgu_bf[...] = wgu_ref[0].astype(BF16)
        wd_bf[...] = wd_ref[0].astype(BF16)

    @pl.when(active)
    def _():
        gu = jnp.dot(x_ref[...].astype(BF16), wgu_bf[...], preferred_element_type=F32) + bgu_ref[0]
        gate = jnp.minimum(gu[:, 0:F], SWIGLU_LIMIT)
        up = jnp.clip(gu[:, F:2 * F], -SWIGLU_LIMIT, SWIGLU_LIMIT)
        glu = gate * jax.nn.sigmoid(gate * SWIGLU_ALPHA)
        out = jnp.dot(((up + 1.0) * glu).astype(BF16), wd_bf[...], preferred_element_type=F32) + bd_ref[0]
        o_ref[...] = out.astype(o_ref.dtype)

    @pl.when(jnp.logical_not(active))
    def _():
        o_ref[...] = jnp.zeros_like(o_ref)


def _experts(blk_expert, n_used, x_sorted, w_gu, b_gu, w_d, b_d):
    P, D = x_sorted.shape
    E, _, F2 = w_gu.shape
    F = F2 // 2
    nb = P // MOE_BLOCK
    grid_spec = pltpu.PrefetchScalarGridSpec(
        num_scalar_prefetch=2,
        grid=(nb,),
        in_specs=[pl.BlockSpec((MOE_BLOCK, D), lambda i, be, nu: (i, 0)),
                  pl.BlockSpec((1, D, F2), lambda i, be, nu: (be[i], 0, 0)),
                  pl.BlockSpec((1, 1, F2), lambda i, be, nu: (be[i], 0, 0)),
                  pl.BlockSpec((1, F, D), lambda i, be, nu: (be[i], 0, 0)),
                  pl.BlockSpec((1, 1, D), lambda i, be, nu: (be[i], 0, 0))],
        out_specs=pl.BlockSpec((MOE_BLOCK, D), lambda i, be, nu: (i, 0)),
        scratch_shapes=[pltpu.VMEM((D, F2), BF16), pltpu.VMEM((F, D), BF16)],
    )
    return pl.pallas_call(
        _expert_kernel,
        grid_spec=grid_spec,
        out_shape=jax.ShapeDtypeStruct((P, D), F32),
        compiler_params=_params("arbitrary"),
        name="experts",
    )(blk_expert, n_used, x_sorted, w_gu, b_gu.reshape(E, 1, F2), w_d, b_d.reshape(E, 1, D))


def _combine_kernel(y_ref, w_ref, x1_ref, gpost_ref, gf_ref, o_ref):
    w = w_ref[0]
    y = (w[:, 0:1] * y_ref[0].astype(F32) + w[:, 1:2] * y_ref[1].astype(F32)
         + w[:, 2:3] * y_ref[2].astype(F32) + w[:, 3:4] * y_ref[3].astype(F32))
    o_ref[0] = x1_ref[0] + gf_ref[0] * _rms(y, gpost_ref[...])


def _combine(y4, wgt, x1, g_post, g_f):
    B, S, D = x1.shape
    tm = min(256, S)
    nt = S // tm
    return pl.pallas_call(
        _combine_kernel,
        grid=(B, nt),
        in_specs=[pl.BlockSpec((TOP_K, tm, D), lambda b, i: (0, b * nt + i, 0)),
                  pl.BlockSpec((1, tm, LANE), lambda b, i: (b, i, 0)),
                  pl.BlockSpec((1, tm, D), lambda b, i: (b, i, 0)),
                  pl.BlockSpec((1, D), lambda b, i: (0, 0)),
                  pl.BlockSpec((1, 1, D), lambda b, i: (b, 0, 0))],
        out_specs=pl.BlockSpec((1, tm, D), lambda b, i: (b, i, 0)),
        out_shape=jax.ShapeDtypeStruct((B, S, D), F32),
        compiler_params=_params("parallel", "arbitrary"),
        name="combine",
    )(y4, wgt, x1, g_post, g_f)


def _rot_cols(w):
    d, n = w.shape
    w4 = w.reshape(d, n // HEAD_DIM, 2, HEAD_DIM // 2)
    return jnp.concatenate([-w4[:, :, 1:2], w4[:, :, 0:1]], axis=2).reshape(d, n)


def _pad_cols(w, n):
    return jnp.pad(w, ((0, 0), (0, n - w.shape[1])))


def _pad_rows(w, n):
    return jnp.pad(w, ((0, n - w.shape[0]), (0, 0)))


def _layer(x, ada, l, mix_pre_norm, mix_post_norm, ffn_pre_norm, ffn_post_norm, w_in, cmp_k_pe, cmp_k_w1,
           cmp_k_w2, cmp_v_pe, cmp_v_w1, cmp_v_w2, nsa_out_norm, rwkv_mu, rwkv_w0, rwkv_w2, rwkv_a0, rwkv_a2,
           rwkv_g2, rwkv_k_k, rwkv_k_a, rwkv_r_k, rwkv_ln_w, rwkv_ln_b, w_out, router_w, router_b,
           expert_w_gate_up, expert_b_gate_up, expert_w_down, expert_b_down):
    B, S, D = x.shape
    T = B * S
    G, Dh, W = NSA_KV_GROUPS, HEAD_DIM, RWKV_WIDTH
    sh_m, sc_m, g_m, sh_f, sc_f, g_f = [a.reshape(B, 1, D) for a in jnp.split(ada, 6, axis=-1)]

    wi = w_in[l]
    KV = NSA_KV_WIDTH
    o = NSA_WIDTH
    wq, wkc, wvc, wks, wvs, wkw, wvw = (wi[:, 0:o],) + tuple(wi[:, o + j * KV:o + (j + 1) * KV] for j in range(6))
    wgl = wi[:, o + 6 * KV:o + 6 * KV + 3 * NSA_HEADS]
    wr = wi[:, o + 6 * KV + 3 * NSA_HEADS:]
    w_rope = jnp.concatenate([wq, wkc, wks, wkw], axis=1)
    w_rest = _pad_cols(jnp.concatenate([wvc, wvs, wvw, wgl], axis=1), REST_COLS)
    lora = [(3 * W, RWKV_W_LORA), (3 * W + RWKV_W_LORA, RWKV_A_LORA),
            (3 * W + RWKV_W_LORA + RWKV_A_LORA, RWKV_G_LORA)]
    pad_lora = lambda a: jnp.concatenate([_pad_cols(a[..., s:s + n], LANE) for s, n in lora], axis=-1)
    w_rw = jnp.concatenate([wr[:, 0:3 * W], pad_lora(wr)], axis=1)
    w_all = jnp.concatenate([w_rope, _rot_cols(w_rope), w_rest, w_rw], axis=1).astype(BF16)
    mu_row = rwkv_mu[l].reshape(1, -1)
    mu = jnp.concatenate([mu_row[:, 0:3 * W], pad_lora(mu_row)], axis=1)

    half = Dh // 2
    inv_freq = ROPE_THETA ** (-jnp.arange(half, dtype=F32) / half)
    ang = jnp.arange(S, dtype=F32)[:, None] * inv_freq[None, :]
    n_rope_heads = ROPE_COLS // Dh
    col_scale = jnp.where(jnp.arange(ROPE_COLS) < NSA_WIDTH, Dh ** -0.5 * math.log2(math.e), 1.0).astype(F32)
    cos_t = jnp.tile(jnp.cos(ang), (1, 2 * n_rope_heads)) * col_scale
    sin_t = jnp.tile(jnp.sin(ang), (1, 2 * n_rope_heads)) * col_scale

    qt, kc, ksw, vc, vt, glt, rw = _inproj(x, mix_pre_norm[l], 1.0 + sc_m, sh_m, w_all, cos_t, sin_t)

    nb = S // CMP_STRIDE
    pe8 = lambda pe: jnp.broadcast_to(pe.reshape(1, -1), (8, pe.size))
    kcmp = _compress(kc, cmp_k_w1[l], cmp_k_w2[l], pe8(cmp_k_pe[l]), transposed=False)
    vcmpt = _compress(vc, cmp_v_w1[l], cmp_v_w2[l].T, pe8(cmp_v_pe[l]), transposed=True)
    ns = S // SLC_BLOCK
    c0 = jnp.arange(nb)[None, :] * CMP_STRIDE
    b0 = jnp.arange(ns)[:, None] * SLC_BLOCK
    ov = jnp.maximum(jnp.minimum(c0 + CMP_BLOCK, b0 + SLC_BLOCK) - jnp.maximum(c0, b0), 0)
    overlap_t = (ov.astype(F32) / CMP_BLOCK).astype(BF16)
    o_nsa_t = _nsa(qt, ksw, vt, kcmp, vcmpt, overlap_t, glt)

    bd = (jnp.arange(W)[:, None] // Dh == jnp.arange(W)[None, :] // Dh).astype(BF16)
    row = lambda a: a.reshape(1, -1)
    r, lw, k, v, kk, a, g = _rwkv_prep(
        rw, mu, row(rwkv_w0[l]), _pad_rows(rwkv_w2[l], LANE), row(rwkv_a0[l]), _pad_rows(rwkv_a2[l], LANE),
        _pad_rows(rwkv_g2[l], LANE), row(rwkv_k_k[l]), row(rwkv_k_a[l]), bd)
    o_rwkv = _rwkv_scan(r, lw, k, v, kk, a, g, row(rwkv_ln_w[l]), row(rwkv_ln_b[l]), row(rwkv_r_k[l]))

    wo = w_out[l].astype(BF16)
    rb = jnp.concatenate([router_b[l], jnp.full((LANE - N_EXPERTS,), NEG_INF, F32)]).reshape(1, LANE)
    x1, h2, idx, wgt = _outproj(o_nsa_t, o_rwkv, x, nsa_out_norm[l].reshape(-1, 1), wo[0:NSA_WIDTH], wo[NSA_WIDTH:],
                                row(mix_post_norm[l]), g_m, row(ffn_pre_norm[l]), 1.0 + sc_f, sh_f,
                                _pad_cols(router_w[l], LANE), rb)

    n_assign = T * TOP_K
    idx4 = idx.reshape(T, LANE)[:, 0:TOP_K]
    e_ids = jnp.arange(N_EXPERTS, dtype=I32)
    hot = idx4[:, :, None] == e_ids
    onehot = hot.astype(I32).sum(axis=1)
    csum = jnp.cumsum(onehot, axis=0)
    counts = csum[-1]
    starts = jnp.cumsum(counts) - counts
    padded = ((counts + MOE_BLOCK - 1) // MOE_BLOCK) * MOE_BLOCK
    pad_ends = jnp.cumsum(padded)
    pad_starts = pad_ends - padded
    pos = jnp.where(hot, (csum - onehot + pad_starts)[:, None, :], 0).sum(axis=-1).astype(I32)
    n_blocks = -(-n_assign // MOE_BLOCK) + N_EXPERTS
    P = n_blocks * MOE_BLOCK
    blk_start = jnp.arange(n_blocks, dtype=I32) * MOE_BLOCK
    blk_expert = jnp.minimum((pad_ends[None, :] <= blk_start[:, None]).sum(axis=1), N_EXPERTS - 1).astype(I32)
    n_used = (pad_ends[-1] // MOE_BLOCK).astype(I32).reshape(1)
    order = jnp.argsort(idx4.reshape(-1))
    blk_off = blk_start - pad_starts[blk_expert]
    run_start = jnp.minimum(starts[blk_expert] + blk_off, n_assign).astype(I32)
    nvalid = jnp.clip(counts[blk_expert] - blk_off, 0, MOE_BLOCK).astype(I32)
    slot_tok = _slot_runs(run_start, nvalid, (order // TOP_K).astype(I32))

    x_sorted = h2.reshape(T, D)[slot_tok]
    y_sorted = _experts(blk_expert, n_used, x_sorted, expert_w_gate_up[l], expert_b_gate_up[l],
                        expert_w_down[l], expert_b_down[l])
    y4 = y_sorted[pos.T]
    return _combine(y4, wgt, x1, row(ffn_post_norm[l]), g_f)


def kernel(x, c, ada_w, ada_b, mix_pre_norm, mix_post_norm, ffn_pre_norm, ffn_post_norm, w_in, cmp_k_pe, cmp_k_w1, cmp_k_w2, cmp_v_pe, cmp_v_w1, cmp_v_w2, nsa_out_norm, rwkv_mu, rwkv_w0, rwkv_w2, rwkv_a0, rwkv_a2, rwkv_g2, rwkv_k_k, rwkv_k_a, rwkv_r_k, rwkv_ln_w, rwkv_ln_b, w_out, router_w, router_b, expert_w_gate_up, expert_b_gate_up, expert_w_down, expert_b_down):
    for l in range(ada_w.shape[0]):
        ada = _ada(c, ada_w[l], ada_b[l])
        x = _layer(x, ada, l, mix_pre_norm, mix_post_norm, ffn_pre_norm, ffn_post_norm, w_in, cmp_k_pe,
                   cmp_k_w1, cmp_k_w2, cmp_v_pe, cmp_v_w1, cmp_v_w2, nsa_out_norm, rwkv_mu, rwkv_w0, rwkv_w2,
                   rwkv_a0, rwkv_a2, rwkv_g2, rwkv_k_k, rwkv_k_a, rwkv_r_k, rwkv_ln_w, rwkv_ln_b, w_out,
                   router_w, router_b, expert_w_gate_up, expert_b_gate_up, expert_w_down, expert_b_down)
    return x
```

```python
import functools
import math

import jax
import jax.numpy as jnp
from jax import lax
from jax.experimental import pallas as pl
from jax.experimental.pallas import tpu as pltpu

F32 = jnp.float32
BF16 = jnp.bfloat16
I32 = jnp.int32

HEAD_DIM = 64
NSA_HEADS = 8
NSA_KV_GROUPS = 2
NSA_HPG = NSA_HEADS // NSA_KV_GROUPS
NSA_WIDTH = NSA_HEADS * HEAD_DIM
NSA_KV_WIDTH = NSA_KV_GROUPS * HEAD_DIM
CMP_BLOCK = 32
CMP_STRIDE = 16
SLC_BLOCK = 64
SLC_TOPK = 16
WINDOW = 512
Q_BLOCK = 256
RWKV_HEADS = 8
RWKV_WIDTH = RWKV_HEADS * HEAD_DIM
RWKV_W_LORA = 32
RWKV_A_LORA = 32
RWKV_G_LORA = 96
DECAY_SCALE = math.exp(-0.5)
GN_EPS = 64e-5
N_EXPERTS = 32
TOP_K = 4
SWIGLU_LIMIT = 7.0
SWIGLU_ALPHA = 1.702
ROPE_THETA = 10000.0
RMS_EPS = 1e-6
NEG_INF = -1e30
FORCED = 1e9
LOWEST = -3e38

LANE = 128
ROPE_COLS = NSA_WIDTH + 3 * NSA_KV_WIDTH
REST_COLS = 4 * LANE
RW_COLS = 3 * RWKV_WIDTH + 3 * LANE
SLC_KT = 512
SLC_UNROLL = 1
FLASH_LANES = 1024
FLASH_DEPTH = 1
CMP_TIERS = 4
ACC_ROWS = HEAD_DIM + 16
RWKV_CHUNK = 128
TOKEN_TILE = 512
RWKV_BATCH_ROWS = 1
MOE_BLOCK = 512
EXPERT_SPLIT = 1
VMEM_LIMIT = 56 * 1024 * 1024


def _bdot(a, b):
    return jnp.dot(a.astype(BF16), b.astype(BF16), preferred_element_type=F32)


def _split2(a):
    hi = a.astype(BF16)
    lo = (a - hi.astype(F32)).astype(BF16)
    return hi, lo


def _dot_lhs2(a, b_bf16):
    hi, lo = _split2(a)
    return (jnp.dot(hi, b_bf16, preferred_element_type=F32)
            + jnp.dot(lo, b_bf16, preferred_element_type=F32))


def _dot3(a, b):
    ah, al = _split2(a)
    bh, bl = _split2(b)
    return (jnp.dot(ah, bh, preferred_element_type=F32)
            + jnp.dot(al, bh, preferred_element_type=F32)
            + jnp.dot(ah, bl, preferred_element_type=F32))


def _params(*sem, flags=None):
    return pltpu.CompilerParams(dimension_semantics=sem, vmem_limit_bytes=VMEM_LIMIT, flags=flags)


def _ada_kernel(c_ref, w_ref, b_ref, o_ref):
    c = c_ref[...]
    act = c * jax.nn.sigmoid(c)
    o_ref[...] = _dot3(act, w_ref[...]) + b_ref[...]


def _ada(c, w, b):
    B, D = c.shape
    N = w.shape[1]
    tn = 1536
    return pl.pallas_call(
        _ada_kernel,
        grid=(N // tn,),
        in_specs=[pl.BlockSpec((B, D), lambda j: (0, 0)),
                  pl.BlockSpec((D, tn), lambda j: (0, j)),
                  pl.BlockSpec((1, tn), lambda j: (0, j))],
        out_specs=pl.BlockSpec((B, tn), lambda j: (0, j)),
        out_shape=jax.ShapeDtypeStruct((B, N), F32),
        compiler_params=_params("arbitrary"),
        name="ada",
    )(c, w, b.reshape(1, N))


def _inproj_kernel(x_ref, g_ref, sc_ref, sh_ref, w_ref, cos_ref, sin_ref,
                   q_ref, kc_ref, ksw_ref, vc_ref, vsw_ref, gl_ref, rw_ref):
    x = x_ref[0]
    ms = jnp.mean(x * x, axis=-1, keepdims=True)
    h = x * lax.rsqrt(ms + RMS_EPS) * g_ref[...]
    h = h * sc_ref[0] + sh_ref[0]
    hb = h.astype(BF16)
    R = ROPE_COLS
    main = jnp.dot(hb, w_ref[:, 0:R], preferred_element_type=F32)
    rot = jnp.dot(hb, w_ref[:, R:2 * R], preferred_element_type=F32)
    roped = main * cos_ref[...] + rot * sin_ref[...]
    q_ref[0] = roped[:, 0:NSA_WIDTH].T.astype(BF16)
    kc_ref[0] = roped[:, NSA_WIDTH:NSA_WIDTH + LANE]
    ksw_ref[0] = roped[:, NSA_WIDTH + LANE:R].astype(BF16)
    rest = jnp.dot(hb, w_ref[:, 2 * R:2 * R + REST_COLS], preferred_element_type=F32)
    vc_ref[0] = rest[:, 0:LANE]
    vsw_ref[0] = rest[:, LANE:3 * LANE].T.astype(BF16)
    gl_ref[0] = rest[:, 3 * LANE:4 * LANE].T
    rw_ref[0] = jnp.dot(hb, w_ref[:, 2 * R + REST_COLS:], preferred_element_type=F32)


def _inproj(x, gain, sc1, sh, w_all, cos_t, sin_t):
    B, S, D = x.shape
    tm = min(256, S)
    NW = w_all.shape[1]
    R = ROPE_COLS
    tok = lambda n: pl.BlockSpec((1, tm, n), lambda b, i: (b, i, 0))
    tok_t = lambda n: pl.BlockSpec((1, n, tm), lambda b, i: (b, 0, i))
    vec = pl.BlockSpec((1, 1, D), lambda b, i: (b, 0, 0))
    outs = [(NSA_WIDTH, BF16, True), (LANE, F32, False), (2 * LANE, BF16, False), (LANE, F32, False),
            (2 * LANE, BF16, True), (LANE, F32, True), (RW_COLS, F32, False)]
    return pl.pallas_call(
        _inproj_kernel,
        grid=(B, S // tm),
        in_specs=[tok(D), pl.BlockSpec((1, D), lambda b, i: (0, 0)), vec, vec,
                  pl.BlockSpec((D, NW), lambda b, i: (0, 0)),
                  pl.BlockSpec((tm, R), lambda b, i: (i, 0)),
                  pl.BlockSpec((tm, R), lambda b, i: (i, 0))],
        out_specs=[tok_t(n) if tr else tok(n) for n, _, tr in outs],
        out_shape=[jax.ShapeDtypeStruct((B, n, S) if tr else (B, S, n), dt) for n, dt, tr in outs],
        compiler_params=_params("parallel", "arbitrary"),
        name="inproj",
    )(x, gain.reshape(1, D), sc1, sh, w_all, cos_t, sin_t)


def _gelu_tanh(x):
    return 0.5 * x * (1.0 + jnp.tanh(math.sqrt(2.0 / math.pi) * (x + 0.044715 * (x * x * x))))


def _compress_kernel(r_ref, w1g_ref, w1_ref, w2_ref, pe_ref, o_ref, *, transposed):
    R = r_ref[0].astype(BF16)
    nb = R.shape[0]
    top = jnp.dot(R, w1g_ref[0, 0].astype(BF16), preferred_element_type=F32)
    bot = jnp.dot(R, w1g_ref[0, 1].astype(BF16), preferred_element_type=F32)
    pe_term = jnp.dot(pe_ref[...].astype(BF16), w1_ref[...].astype(BF16), preferred_element_type=F32)[0:1]
    bot_next = pltpu.roll(bot, nb - 1, 0)
    hid = _gelu_tanh(top + bot_next + pe_term)
    if transposed:
        out = jnp.dot(w2_ref[...].astype(BF16), hid.T.astype(BF16), preferred_element_type=F32)
    else:
        out = jnp.dot(hid.astype(BF16), w2_ref[...].astype(BF16), preferred_element_type=F32)
    o_ref[0, 0] = out.astype(BF16)


def _compress(tok, w1, w2, pe, transposed):
    B, S, GD = tok.shape
    G, Dh = NSA_KV_GROUPS, HEAD_DIM
    nb = S // CMP_STRIDE
    hid = w1.shape[-1]
    width = CMP_STRIDE * GD
    rr = tok.reshape(B, nb, width)
    w1r = w1.reshape(2, CMP_STRIDE, Dh, hid)
    w1g = jnp.zeros((G, 2, CMP_STRIDE, G, Dh, hid), w1.dtype)
    for g in range(G):
        w1g = w1g.at[g, :, :, g].set(w1r)
    w1g = w1g.reshape(G, 2, width, hid)
    oshape = (Dh, nb) if transposed else (nb, Dh)
    return pl.pallas_call(
        functools.partial(_compress_kernel, transposed=transposed),
        grid=(B, G),
        in_specs=[pl.BlockSpec((1, nb, width), lambda b, g: (b, 0, 0)),
                  pl.BlockSpec((1, 2, width, hid), lambda b, g: (g, 0, 0, 0)),
                  pl.BlockSpec(w1.shape, lambda b, g: (0, 0)),
                  pl.BlockSpec(w2.shape, lambda b, g: (0, 0)),
                  pl.BlockSpec(pe.shape, lambda b, g: (0, 0))],
        out_specs=pl.BlockSpec((1, 1) + oshape, lambda b, g: (b, g, 0, 0)),
        out_shape=jax.ShapeDtypeStruct((B, G) + oshape, BF16),
        compiler_params=_params("arbitrary", "arbitrary"),
        name="compress_v" if transposed else "compress_k",
    )(rr, w1g, w1, w2, pe)


def _lanes4(a):
    return jnp.concatenate([a, a, a, a], axis=1)


def _nsa_kernel(qt_ref, ksw_ref, vt_ref, kcmp_ref, vcmpt_ref, ovt_ref, glt_ref, o_ref, selb_ref, *, seq_len):
    QB, Dh, HPG = Q_BLOCK, HEAD_DIM, NSA_HPG
    qi = pl.program_id(1)
    s0 = qi * QB
    ns = seq_len // SLC_BLOCK
    ncp = seq_len // CMP_STRIDE
    n_sel = min(SLC_TOPK, ns)
    kt_slc = min(SLC_KT, seq_len)
    t_row = s0 + lax.broadcasted_iota(I32, (1, QB), 1)
    t4 = _lanes4(t_row)
    gates = jax.nn.sigmoid(glt_ref[0])
    blk = lax.broadcasted_iota(I32, (ns, QB), 0)
    cur = lax.shift_right_logical(t_row, 6)
    forced = (blk == 0) | (blk == cur) | (blk == cur - 1)
    future = blk * SLC_BLOCK > t_row
    cmp_last = lax.broadcasted_iota(I32, (ncp, 1), 0) * CMP_STRIDE + (CMP_BLOCK - 1)
    cmask = cmp_last <= t4
    zeros_q = jnp.zeros((Dh, HPG * QB), BF16)
    G = NSA_KV_GROUPS
    NL = G * HPG * QB
    qgs = [jnp.concatenate([qt_ref[0, (g * HPG + h) * Dh:(g * HPG + h + 1) * Dh, :] for h in range(HPG)], axis=1)
           for g in range(G)]
    qpads = [jnp.concatenate([qgs[g], zeros_q] if g == 0 else [zeros_q, qgs[g]], axis=0) for g in range(G)]

    def cmp_branch(rows):
        def run():
            outs = []
            for g in range(G):
                sc = jnp.dot(kcmp_ref[0, g, 0:rows, :], qgs[g], preferred_element_type=F32)
                sc = jnp.where(cmask[0:rows], sc, NEG_INF)
                m = jnp.max(sc, axis=0, keepdims=True)
                p = jnp.where(cmask[0:rows], jnp.exp2(sc - m), 0.0)
                l = jnp.sum(p, axis=0, keepdims=True)
                p = p * (1.0 / jnp.maximum(l, 1e-30))
                o_cmp = jnp.dot(vcmpt_ref[0, g, :, 0:rows], p.astype(BF16), preferred_element_type=F32)
                psum = p[:, 0:QB] + p[:, QB:2 * QB] + p[:, 2 * QB:3 * QB] + p[:, 3 * QB:4 * QB]
                p_hi, p_lo = _split2(psum)
                imp = (jnp.dot(ovt_ref[:, 0:rows], p_hi, preferred_element_type=F32)
                       + jnp.dot(ovt_ref[:, 0:rows], p_lo, preferred_element_type=F32))
                outs += [o_cmp, imp]
            return tuple(outs)
        return run

    n_tier = CMP_TIERS if ncp % (CMP_TIERS * 16) == 0 else 1
    tier_rows = ncp // n_tier
    n_visible = (s0 + QB - CMP_BLOCK) // CMP_STRIDE + 1
    tier = jnp.clip((n_visible - 1) // tier_rows, 0, n_tier - 1)
    cmp_out = lax.switch(tier, [cmp_branch((k + 1) * tier_rows) for k in range(n_tier)])
    o_cmps = [cmp_out[2 * g] for g in range(G)]
    imps = [jnp.where(future, NEG_INF, jnp.where(forced, FORCED, cmp_out[2 * g + 1])) for g in range(G)]

    vals = jnp.concatenate(imps, axis=1)
    blk2 = jnp.concatenate([blk] * G, axis=1)
    sel = jnp.zeros((ns, G * QB), F32)
    for _ in range(n_sel):
        mx = jnp.max(vals, axis=0, keepdims=True)
        idx = jnp.min(jnp.where(vals == mx, blk2, ns), axis=0, keepdims=True)
        pick = blk2 == idx
        sel = jnp.where(pick, 1.0, sel)
        vals = jnp.where(pick, LOWEST, vals)
    for g in range(G):
        selb_ref[g] = jnp.where(sel[:, g * QB:(g + 1) * QB] > 0.5, 0.0, NEG_INF)

    q_slabs = [jnp.concatenate(qpads, axis=1)[:, sb * FLASH_LANES:(sb + 1) * FLASH_LANES]
               for sb in range(NL // FLASH_LANES)]
    n_slab = len(q_slabs)

    def load_tile(k_lane0, vrow0, k0, width, bias_of_group):
        ones = jnp.ones((ACC_ROWS - Dh, width), BF16)
        kt = ksw_ref[0, pl.ds(k0, width), k_lane0:k_lane0 + LANE]
        vaug = [jnp.concatenate([vt_ref[0, vrow0 + g * Dh:vrow0 + (g + 1) * Dh, pl.ds(k0, width)], ones], axis=0)
                for g in range(G)]
        bias = [bias_of_group(g) for g in range(G)]
        return kt, vaug, bias

    group_lanes = HPG * QB
    groups_per_slab = max(FLASH_LANES // group_lanes, 1)
    lanes_per_part = FLASH_LANES // groups_per_slab

    def flash_tiles(tiles, carry):
        m, acc = list(carry[0]), list(carry[1])
        steps = [(ti, sb) for ti in range(len(tiles)) for sb in range(n_slab)]
        scores = {}
        for i in range(len(steps) + FLASH_DEPTH):
            if i < len(steps):
                ti, sb = steps[i]
                scores[i] = jnp.dot(tiles[ti][0], q_slabs[sb], preferred_element_type=F32)
            j = i - FLASH_DEPTH
            if j >= 0:
                ti, sb = steps[j]
                _, vaug, bias = tiles[ti]
                g0 = sb * FLASH_LANES // group_lanes
                slab_bias = jnp.concatenate(
                    [bias[g0 + p] for p in range(groups_per_slab) for _ in range(lanes_per_part // QB)], axis=1)
                s = scores.pop(j) + slab_bias
                m_new = jnp.maximum(m[sb], jnp.max(s, axis=0, keepdims=True))
                alpha = jnp.exp2(m[sb] - m_new)
                pj = jnp.exp2(s - m_new).astype(BF16)
                pv = [jnp.dot(vaug[g0 + p], pj[:, p * lanes_per_part:(p + 1) * lanes_per_part],
                              preferred_element_type=F32) for p in range(groups_per_slab)]
                acc[sb] = alpha * acc[sb] + (pv[0] if len(pv) == 1 else jnp.concatenate(pv, axis=1))
                m[sb] = m_new
        return tuple(m), tuple(acc)

    def slc_tile(j):
        k0 = pl.multiple_of(j * kt_slc, kt_slc)
        nblk = kt_slc // SLC_BLOCK
        causal = k0 + lax.broadcasted_iota(I32, (kt_slc, 1), 0) <= t_row

        def bias(g):
            blk_bias = jnp.concatenate(
                [jnp.broadcast_to(selb_ref[g, pl.ds(j * nblk + i, 1), :], (SLC_BLOCK, QB)) for i in range(nblk)],
                axis=0)
            return jnp.where(causal, blk_bias, NEG_INF)

        return load_tile(0, 0, k0, kt_slc, bias)

    n_all = seq_len // kt_slc
    unroll = math.gcd(SLC_UNROLL, n_all)

    def slc_body(jj, carry):
        return flash_tiles([slc_tile(jj * unroll + u) for u in range(unroll)], carry)

    init = (tuple(jnp.full((1, FLASH_LANES), NEG_INF, F32) for _ in range(n_slab)),
            tuple(jnp.zeros((ACC_ROWS, FLASH_LANES), F32) for _ in range(n_slab)))
    n_tiles = (s0 + QB - 1) // kt_slc + 1
    _, acc_s = lax.fori_loop(0, (n_tiles + unroll - 1) // unroll, slc_body, init)
    acc_s = jnp.concatenate(acc_s, axis=1)
    o_slc = acc_s[0:Dh] * (1.0 / acc_s[Dh:Dh + 1])

    win_keys = min(WINDOW + QB, seq_len)
    k0 = pl.multiple_of(jnp.maximum(qi - WINDOW // QB, 0) * QB, QB)
    kpos = k0 + lax.broadcasted_iota(I32, (win_keys, 1), 0)
    wbias = jnp.where((kpos <= t_row) & (kpos > t_row - WINDOW), 0.0, NEG_INF)
    _, acc_w = flash_tiles([load_tile(LANE, LANE, k0, win_keys, lambda g: wbias)], init)
    acc_w = jnp.concatenate(acc_w, axis=1)
    o_win = acc_w[0:Dh] * (1.0 / acc_w[Dh:Dh + 1])

    for g in range(G):
        for h in range(HPG):
            c0 = (g * HPG + h) * 3
            cols = slice((g * HPG + h) * QB, (g * HPG + h + 1) * QB)
            o = (gates[c0:c0 + 1, :] * o_cmps[g][:, h * QB:(h + 1) * QB] + gates[c0 + 1:c0 + 2, :] * o_slc[:, cols]
                 + gates[c0 + 2:c0 + 3, :] * o_win[:, cols])
            o_ref[0, (g * HPG + h) * Dh:(g * HPG + h + 1) * Dh, :] = o


def _nsa(qt, ksw, vt, kcmp, vcmpt, overlap_t, glt):
    B, _, S = qt.shape
    ncp = S // CMP_STRIDE
    ns = S // SLC_BLOCK
    G = NSA_KV_GROUPS
    return pl.pallas_call(
        functools.partial(_nsa_kernel, seq_len=S),
        grid=(B, S // Q_BLOCK),
        in_specs=[pl.BlockSpec((1, NSA_WIDTH, Q_BLOCK), lambda b, i: (b, 0, i)),
                  pl.BlockSpec((1, S, 2 * LANE), lambda b, i: (b, 0, 0)),
                  pl.BlockSpec((1, 2 * LANE, S), lambda b, i: (b, 0, 0)),
                  pl.BlockSpec((1, G, ncp, HEAD_DIM), lambda b, i: (b, 0, 0, 0)),
                  pl.BlockSpec((1, G, HEAD_DIM, ncp), lambda b, i: (b, 0, 0, 0)),
                  pl.BlockSpec((ns, ncp), lambda b, i: (0, 0)),
                  pl.BlockSpec((1, LANE, Q_BLOCK), lambda b, i: (b, 0, i))],
        out_specs=pl.BlockSpec((1, NSA_WIDTH, Q_BLOCK), lambda b, i: (b, 0, i)),
        out_shape=jax.ShapeDtypeStruct((B, NSA_WIDTH, S), F32),
        scratch_shapes=[pltpu.VMEM((G, ns, Q_BLOCK), F32)],
        compiler_params=_params("parallel", "arbitrary"),
        name="nsa",
    )(qt, ksw, vt, kcmp, vcmpt, overlap_t, glt)


def _rwkv_prep_kernel(f_ref, p_ref, mu_ref, w0_ref, w2_ref, a0_ref, a2_ref, g2_ref, kk_ref, ka_ref,
                      bd_ref, r_o, lw_o, k_o, v_o, kk_o, a_o, g_o):
    W = RWKV_WIDTH
    i = pl.program_id(1)
    f = f_ref[0]
    tm = f.shape[0]
    prev_last = jnp.where(i > 0, p_ref[0, 7:8, :], 0.0)
    rolled = pltpu.roll(f, 1, 0)
    row = lax.broadcasted_iota(I32, (tm, 1), 0)
    prev = jnp.where(row == 0, prev_last, rolled)
    f = f + (prev - f) * mu_ref[...]
    r, k, v = f[:, 0:W], f[:, W:2 * W], f[:, 2 * W:3 * W]
    wd = f[:, 3 * W:3 * W + LANE]
    ad = f[:, 3 * W + LANE:3 * W + 2 * LANE]
    gd = f[:, 3 * W + 2 * LANE:3 * W + 3 * LANE]
    lw_o[0] = -DECAY_SCALE * jax.nn.sigmoid(w0_ref[...] + _dot3(jnp.tanh(wd), w2_ref[...]))
    a = jax.nn.sigmoid(a0_ref[...] + _dot3(ad, a2_ref[...]))
    g_o[0] = _dot3(jax.nn.sigmoid(gd), g2_ref[...])
    kk = k * kk_ref[...]
    ss = _dot_lhs2(kk * kk, bd_ref[...])
    kk_o[0] = kk / jnp.maximum(jnp.sqrt(ss), 1e-12)
    k_o[0] = k * (1.0 + (a - 1.0) * ka_ref[...])
    r_o[0] = r
    v_o[0] = v
    a_o[0] = a


def _rwkv_prep(rw, mu, w0, w2, a0, a2, g2, k_k, k_a, bd):
    B, S, C = rw.shape
    W = RWKV_WIDTH
    tm = min(TOKEN_TILE, S)
    row = lambda n: pl.BlockSpec((1, n), lambda b, i: (0, 0))
    mat = pl.BlockSpec((LANE, W), lambda b, i: (0, 0))
    tok = pl.BlockSpec((1, tm, W), lambda b, i: (b, i, 0))
    return pl.pallas_call(
        _rwkv_prep_kernel,
        grid=(B, S // tm),
        in_specs=[pl.BlockSpec((1, tm, C), lambda b, i: (b, i, 0)),
                  pl.BlockSpec((1, 8, C), lambda b, i: (b, jnp.maximum(i * (tm // 8) - 1, 0), 0)),
                  row(C), row(W), mat, row(W), mat, mat, row(W), row(W),
                  pl.BlockSpec((W, W), lambda b, i: (0, 0))],
        out_specs=[tok] * 7,
        out_shape=[jax.ShapeDtypeStruct((B, S, W), F32)] * 7,
        compiler_params=_params("parallel", "arbitrary"),
        name="rwkv_prep",
    )(rw, rw, mu, w0, w2, a0, a2, g2, k_k, k_a, bd)


def _rwkv_scan_kernel(r_ref, lw_ref, k_ref, v_ref, kk_ref, a_ref, g_ref, lnw_ref, lnb_ref, rk_ref,
                      o_ref, st_ref):
    Dh = HEAD_DIM

    @pl.when(pl.program_id(1) == 0)
    def _():
        st_ref[...] = jnp.zeros_like(st_ref)

    NB, C = r_ref.shape[0], r_ref.shape[1]
    ri = lax.broadcasted_iota(I32, (C, C), 0)
    ci = lax.broadcasted_iota(I32, (C, C), 1)
    incl = ri >= ci
    strict = ri > ci
    tri = incl.astype(BF16)
    eye = (ri == ci).astype(F32)
    n_double = max(int(math.log2(C)) - 1, 0)

    def scaled_operands(bb):
        r, lw, k, kk, a = (ref[bb] for ref in (r_ref, lw_ref, k_ref, kk_ref, a_ref))
        l_hi = lw.astype(BF16)
        l_r1 = lw - l_hi.astype(F32)
        l_mid = l_r1.astype(BF16)
        l_lo = (l_r1 - l_mid.astype(F32)).astype(BF16)
        L = (jnp.dot(tri, l_hi, preferred_element_type=F32) + jnp.dot(tri, l_mid, preferred_element_type=F32)
             + jnp.dot(tri, l_lo, preferred_element_type=F32))
        e_l = jnp.exp(L)
        e_inv = jnp.exp(-L)
        e_end = jnp.exp(L[C - 1:C, :] - L)
        b = kk * a
        x1 = jnp.concatenate([kk * jnp.exp(L - lw), r * e_l], axis=0).astype(BF16)
        x2t = jnp.concatenate([k * e_inv, b * e_inv], axis=0).T.astype(BF16)
        zt = jnp.concatenate([k * e_end, -(b * e_end)], axis=0).T.astype(BF16)
        gt = jnp.concatenate([e_l, e_l], axis=0).T
        return x1, x2t, zt, gt

    ops = [scaled_operands(bb) for bb in range(NB)]
    chains = [(bb, h) for bb in range(NB) for h in range(RWKV_HEADS)]
    sls = [slice(h * Dh, (h + 1) * Dh) for _, h in chains]
    h0s = [st_ref[bb, h] for bb, h in chains]
    vhs = [v_ref[bb][:, sl] for (bb, _), sl in zip(chains, sls)]
    x1s = [ops[bb][0][:, sl] for (bb, _), sl in zip(chains, sls)]
    amats = [jnp.dot(x1, ops[bb][1][sl, :], preferred_element_type=F32)
             for x1, (bb, _), sl in zip(x1s, chains, sls)]
    p0s = [jnp.dot(x1, h0.astype(BF16), preferred_element_type=F32) for x1, h0 in zip(x1s, h0s)]
    a_kk = [jnp.where(strict, am[0:C, 0:C], 0.0).astype(BF16) for am in amats]
    a_rkb = [jnp.concatenate([jnp.where(incl, am[C:2 * C, 0:C], 0.0), jnp.where(incl, -am[C:2 * C, C:2 * C], 0.0)],
                             axis=1).astype(BF16) for am in amats]
    nmats = [jnp.where(strict, -am[0:C, C:2 * C], 0.0) for am in amats]
    tinvs = [eye + nm for nm in nmats]
    for _ in range(n_double):
        nmats = [_bdot(nm, nm) for nm in nmats]
        tinvs = [ti + _bdot(ti, nm) for ti, nm in zip(tinvs, nmats)]
    akv = [jnp.dot(ak, vh.astype(BF16), preferred_element_type=F32) for ak, vh in zip(a_kk, vhs)]
    us = [_bdot(ti, p0[0:C] + av) for ti, p0, av in zip(tinvs, p0s, akv)]
    vus = [jnp.concatenate([vh, u], axis=0).astype(BF16) for vh, u in zip(vhs, us)]
    ys = [p0[C:2 * C] + jnp.dot(ar, vu, preferred_element_type=F32) for p0, ar, vu in zip(p0s, a_rkb, vus)]
    for i, (bb, h) in enumerate(chains):
        _, _, zt, gt = ops[bb]
        st_ref[bb, h] = gt[sls[i], C - 1:C] * h0s[i] + jnp.dot(zt[sls[i], :], vus[i], preferred_element_type=F32)
    for i, (bb, h) in enumerate(chains):
        sl, y = sls[i], ys[i]
        mean = jnp.mean(y, axis=-1, keepdims=True)
        yc = y - mean
        var = jnp.mean(yc * yc, axis=-1, keepdims=True)
        yn = yc * lax.rsqrt(var + GN_EPS) * lnw_ref[:, sl] + lnb_ref[:, sl]
        bonus = jnp.sum(r_ref[bb][:, sl] * k_ref[bb][:, sl] * rk_ref[:, sl], axis=-1, keepdims=True) * vhs[i]
        o_ref[bb, :, sl] = (yn + bonus) * g_ref[bb][:, sl]


def _rwkv_scan(r, lw, k, v, kk, a, g, ln_w, ln_b, r_k):
    B, S, W = r.shape
    C = min(RWKV_CHUNK, S)
    nb = math.gcd(RWKV_BATCH_ROWS, B)
    tok = pl.BlockSpec((nb, C, W), lambda b, i: (b, i, 0))
    row = pl.BlockSpec((1, W), lambda b, i: (0, 0))
    return pl.pallas_call(
        _rwkv_scan_kernel,
        grid=(B // nb, S // C),
        in_specs=[tok] * 7 + [row] * 3,
        out_specs=tok,
        out_shape=jax.ShapeDtypeStruct((B, S, W), F32),
        scratch_shapes=[pltpu.VMEM((nb, RWKV_HEADS, HEAD_DIM, HEAD_DIM), F32)],
        compiler_params=_params("parallel", "arbitrary"),
        name="rwkv_scan",
    )(r, lw, k, v, kk, a, g, ln_w, ln_b, r_k)


def _rms(x, g):
    return x * lax.rsqrt(jnp.mean(x * x, axis=-1, keepdims=True) + RMS_EPS) * g


def _outproj_kernel(on_ref, or_ref, x_ref, gn_ref, wt_ref, wb_ref, gpost_ref, gm_ref, gpre_ref,
                    scf_ref, shf_ref, rw_ref, rb_ref, x1_ref, h2_ref, idx_ref, wgt_ref):
    ont = on_ref[0]
    ont = ont * lax.rsqrt(jnp.mean(ont * ont, axis=0, keepdims=True) + RMS_EPS) * gn_ref[...]
    on = ont.T
    mixed = (jnp.dot(on.astype(BF16), wt_ref[...], preferred_element_type=F32)
             + jnp.dot(or_ref[0].astype(BF16), wb_ref[...], preferred_element_type=F32))
    x1 = x_ref[0] + gm_ref[0] * _rms(mixed, gpost_ref[...])
    x1_ref[0] = x1
    h2 = _rms(x1, gpre_ref[...]) * scf_ref[0] + shf_ref[0]
    h2_ref[0] = h2
    logits = _dot3(h2, rw_ref[...]) + rb_ref[...]
    lane = lax.broadcasted_iota(I32, logits.shape, 1)
    vals = logits
    top_v, top_i = [], []
    for _ in range(TOP_K):
        mx = jnp.max(vals, axis=-1, keepdims=True)
        ix = jnp.min(jnp.where(vals == mx, lane, LANE), axis=-1, keepdims=True)
        top_v.append(mx)
        top_i.append(ix)
        vals = jnp.where(lane == ix, LOWEST, vals)
    ex = [jnp.exp(tv - top_v[0]) for tv in top_v]
    inv = 1.0 / (ex[0] + ex[1] + ex[2] + ex[3])
    idx = jnp.zeros(logits.shape, I32)
    wgt = jnp.zeros(logits.shape, F32)
    for kk in range(TOP_K):
        idx = jnp.where(lane == kk, top_i[kk], idx)
        wgt = jnp.where(lane == kk, ex[kk] * inv, wgt)
    idx_ref[0] = idx
    wgt_ref[0] = wgt


def _outproj(o_nsa, o_rwkv, x, gn, w_top, w_bot, g_post, g_m, g_pre, sc1_f, sh_f, router_w, router_b):
    B, S, D = x.shape
    tm = min(TOKEN_TILE, S)
    W = o_rwkv.shape[-1]
    tok = lambda n: pl.BlockSpec((1, tm, n), lambda b, i: (b, i, 0))
    vec = pl.BlockSpec((1, 1, D), lambda b, i: (b, 0, 0))
    row = lambda n: pl.BlockSpec((1, n), lambda b, i: (0, 0))
    return pl.pallas_call(
        _outproj_kernel,
        grid=(B, S // tm),
        in_specs=[pl.BlockSpec((1, W, tm), lambda b, i: (b, 0, i)), tok(W), tok(D),
                  pl.BlockSpec((W, 1), lambda b, i: (0, 0)),
                  pl.BlockSpec((W, D), lambda b, i: (0, 0)), pl.BlockSpec((W, D), lambda b, i: (0, 0)),
                  row(D), vec, row(D), vec, vec,
                  pl.BlockSpec((D, LANE), lambda b, i: (0, 0)), row(LANE)],
        out_specs=[tok(D), tok(D), tok(LANE), tok(LANE)],
        out_shape=[jax.ShapeDtypeStruct((B, S, D), F32), jax.ShapeDtypeStruct((B, S, D), F32),
                   jax.ShapeDtypeStruct((B, S, LANE), I32), jax.ShapeDtypeStruct((B, S, LANE), F32)],
        compiler_params=_params("parallel", "arbitrary"),
        name="outproj",
    )(o_nsa, o_rwkv, x, gn, w_top, w_bot, g_post, g_m, g_pre, sc1_f, sh_f, router_w, router_b)


def _slot_runs_kernel(start_ref, nvalid_ref, tok_ref, o_ref):
    rows = MOE_BLOCK // LANE
    lane = lax.broadcasted_iota(I32, (rows, LANE), 1)
    within = lax.broadcasted_iota(I32, (rows, LANE), 0) * LANE + lane

    def body(b, carry):
        s = start_ref[b]
        off = jnp.bitwise_and(s, LANE - 1)
        win = tok_ref[pl.ds(lax.shift_right_logical(s, 7), rows + 1), :]
        win = pltpu.roll(win, jnp.bitwise_and(LANE - off, LANE - 1), 1)
        run = jnp.where(lane < LANE - off, win[0:rows], win[1:rows + 1])
        o_ref[pl.ds(pl.multiple_of(b * rows, rows), rows), :] = jnp.where(within < nvalid_ref[b], run, 0)
        return carry

    lax.fori_loop(0, start_ref.shape[0], body, 0)


def _slot_runs(run_start, nvalid, tok_sorted):
    nb = run_start.shape[0]
    rows = MOE_BLOCK // LANE
    n_rows = -(-(tok_sorted.shape[0] // LANE + rows + 1) // 8) * 8
    tok2d = jnp.pad(tok_sorted, (0, n_rows * LANE - tok_sorted.shape[0])).reshape(n_rows, LANE)
    grid_spec = pltpu.PrefetchScalarGridSpec(
        num_scalar_prefetch=2,
        grid=(1,),
        in_specs=[pl.BlockSpec((n_rows, LANE), lambda i, s, n: (0, 0))],
        out_specs=pl.BlockSpec((nb * rows, LANE), lambda i, s, n: (0, 0)),
    )
    out = pl.pallas_call(
        _slot_runs_kernel,
        grid_spec=grid_spec,
        out_shape=jax.ShapeDtypeStruct((nb * rows, LANE), I32),
        compiler_params=_params("arbitrary"),
        name="slot_runs",
    )(run_start, nvalid, tok2d)
    return out.reshape(nb * MOE_BLOCK)


def _expert_kernel(be_ref, nu_ref, x_ref, wgu_ref, bgu_ref, wd_ref, bd_ref, o_ref, wgu_bf, wd_bf):
    F = wd_ref.shape[1]
    i = pl.program_id(0)
    active = i < nu_ref[0]
    new_expert = (i == 0) | (be_ref[i] != be_ref[jnp.maximum(i - 1, 0)])

    @pl.when(active & new_expert)
    def _():
        wgu_bf[...] = wgu_ref[0].astype(BF16)
        wd_bf[...] = wd_ref[0].astype(BF16)

    @pl.when(active)
    def _():
        rows = x_ref.shape[0] // EXPERT_SPLIT
        subs = [slice(j * rows, (j + 1) * rows) for j in range(EXPERT_SPLIT)]
        gus = [jnp.dot(x_ref[sl, :].astype(BF16), wgu_bf[...], preferred_element_type=F32) + bgu_ref[0] for sl in subs]
        acts = []
        for gu in gus:
            gate = jnp.minimum(gu[:, 0:F], SWIGLU_LIMIT)
            up = jnp.clip(gu[:, F:2 * F], -SWIGLU_LIMIT, SWIGLU_LIMIT)
            acts.append(((up + 1.0) * (gate * jax.nn.sigmoid(gate * SWIGLU_ALPHA))).astype(BF16))
        for sl, act in zip(subs, acts):
            out = jnp.dot(act, wd_bf[...], preferred_element_type=F32) + bd_ref[0]
            o_ref[sl, :] = out.astype(o_ref.dtype)

    @pl.when(jnp.logical_not(active))
    def _():
        o_ref[...] = jnp.zeros_like(o_ref)


def _experts(blk_expert, n_used, x_sorted, w_gu, b_gu, w_d, b_d):
    P, D = x_sorted.shape
    E, _, F2 = w_gu.shape
    F = F2 // 2
    nb = P // MOE_BLOCK
    grid_spec = pltpu.PrefetchScalarGridSpec(
        num_scalar_prefetch=2,
        grid=(nb,),
        in_specs=[pl.BlockSpec((MOE_BLOCK, D), lambda i, be, nu: (i, 0)),
                  pl.BlockSpec((1, D, F2), lambda i, be, nu: (be[i], 0, 0)),
                  pl.BlockSpec((1, 1, F2), lambda i, be, nu: (be[i], 0, 0)),
                  pl.BlockSpec((1, F, D), lambda i, be, nu: (be[i], 0, 0)),
                  pl.BlockSpec((1, 1, D), lambda i, be, nu: (be[i], 0, 0))],
        out_specs=pl.BlockSpec((MOE_BLOCK, D), lambda i, be, nu: (i, 0)),
        scratch_shapes=[pltpu.VMEM((D, F2), BF16), pltpu.VMEM((F, D), BF16)],
    )
    return pl.pallas_call(
        _expert_kernel,
        grid_spec=grid_spec,
        out_shape=jax.ShapeDtypeStruct((P, D), F32),
        compiler_params=_params("arbitrary"),
        name="experts",
    )(blk_expert, n_used, x_sorted, w_gu, b_gu.reshape(E, 1, F2), w_d, b_d.reshape(E, 1, D))


def _combine_kernel(y_ref, w_ref, x1_ref, gpost_ref, gf_ref, o_ref):
    w = w_ref[0]
    y = (w[:, 0:1] * y_ref[0].astype(F32) + w[:, 1:2] * y_ref[1].astype(F32)
         + w[:, 2:3] * y_ref[2].astype(F32) + w[:, 3:4] * y_ref[3].astype(F32))
    o_ref[0] = x1_ref[0] + gf_ref[0] * _rms(y, gpost_ref[...])


def _combine(y4, wgt, x1, g_post, g_f):
    B, S, D = x1.shape
    tm = min(TOKEN_TILE, S)
    nt = S // tm
    return pl.pallas_call(
        _combine_kernel,
        grid=(B, nt),
        in_specs=[pl.BlockSpec((TOP_K, tm, D), lambda b, i: (0, b * nt + i, 0)),
                  pl.BlockSpec((1, tm, LANE), lambda b, i: (b, i, 0)),
                  pl.BlockSpec((1, tm, D), lambda b, i: (b, i, 0)),
                  pl.BlockSpec((1, D), lambda b, i: (0, 0)),
                  pl.BlockSpec((1, 1, D), lambda b, i: (b, 0, 0))],
        out_specs=pl.BlockSpec((1, tm, D), lambda b, i: (b, i, 0)),
        out_shape=jax.ShapeDtypeStruct((B, S, D), F32),
        compiler_params=_params("parallel", "arbitrary"),
        name="combine",
    )(y4, wgt, x1, g_post, g_f)


def _rot_cols(w):
    d, n = w.shape
    w4 = w.reshape(d, n // HEAD_DIM, 2, HEAD_DIM // 2)
    return jnp.concatenate([-w4[:, :, 1:2], w4[:, :, 0:1]], axis=2).reshape(d, n)


def _pad_cols(w, n):
    return jnp.pad(w, ((0, 0), (0, n - w.shape[1])))


def _pad_rows(w, n):
    return jnp.pad(w, ((0, n - w.shape[0]), (0, 0)))


def _layer(x, ada, l, mix_pre_norm, mix_post_norm, ffn_pre_norm, ffn_post_norm, w_in, cmp_k_pe, cmp_k_w1,
           cmp_k_w2, cmp_v_pe, cmp_v_w1, cmp_v_w2, nsa_out_norm, rwkv_mu, rwkv_w0, rwkv_w2, rwkv_a0, rwkv_a2,
           rwkv_g2, rwkv_k_k, rwkv_k_a, rwkv_r_k, rwkv_ln_w, rwkv_ln_b, w_out, router_w, router_b,
           expert_w_gate_up, expert_b_gate_up, expert_w_down, expert_b_down):
    B, S, D = x.shape
    T = B * S
    G, Dh, W = NSA_KV_GROUPS, HEAD_DIM, RWKV_WIDTH
    sh_m, sc_m, g_m, sh_f, sc_f, g_f = [a.reshape(B, 1, D) for a in jnp.split(ada, 6, axis=-1)]

    wi = w_in[l]
    KV = NSA_KV_WIDTH
    o = NSA_WIDTH
    wq, wkc, wvc, wks, wvs, wkw, wvw = (wi[:, 0:o],) + tuple(wi[:, o + j * KV:o + (j + 1) * KV] for j in range(6))
    wgl = wi[:, o + 6 * KV:o + 6 * KV + 3 * NSA_HEADS]
    wr = wi[:, o + 6 * KV + 3 * NSA_HEADS:]
    w_rope = jnp.concatenate([wq, wkc, wks, wkw], axis=1)
    w_rest = _pad_cols(jnp.concatenate([wvc, wvs, wvw, wgl], axis=1), REST_COLS)
    lora = [(3 * W, RWKV_W_LORA), (3 * W + RWKV_W_LORA, RWKV_A_LORA),
            (3 * W + RWKV_W_LORA + RWKV_A_LORA, RWKV_G_LORA)]
    pad_lora = lambda a: jnp.concatenate([_pad_cols(a[..., s:s + n], LANE) for s, n in lora], axis=-1)
    w_rw = jnp.concatenate([wr[:, 0:3 * W], pad_lora(wr)], axis=1)
    w_all = jnp.concatenate([w_rope, _rot_cols(w_rope), w_rest, w_rw], axis=1).astype(BF16)
    mu_row = rwkv_mu[l].reshape(1, -1)
    mu = jnp.concatenate([mu_row[:, 0:3 * W], pad_lora(mu_row)], axis=1)

    half = Dh // 2
    inv_freq = ROPE_THETA ** (-jnp.arange(half, dtype=F32) / half)
    ang = jnp.arange(S, dtype=F32)[:, None] * inv_freq[None, :]
    n_rope_heads = ROPE_COLS // Dh
    col_scale = jnp.where(jnp.arange(ROPE_COLS) < NSA_WIDTH, Dh ** -0.5 * math.log2(math.e), 1.0).astype(F32)
    cos_t = jnp.tile(jnp.cos(ang), (1, 2 * n_rope_heads)) * col_scale
    sin_t = jnp.tile(jnp.sin(ang), (1, 2 * n_rope_heads)) * col_scale

    qt, kc, ksw, vc, vt, glt, rw = _inproj(x, mix_pre_norm[l], 1.0 + sc_m, sh_m, w_all, cos_t, sin_t)

    nb = S // CMP_STRIDE
    pe8 = lambda pe: jnp.broadcast_to(pe.reshape(1, -1), (8, pe.size))
    kcmp = _compress(kc, cmp_k_w1[l], cmp_k_w2[l], pe8(cmp_k_pe[l]), transposed=False)
    vcmpt = _compress(vc, cmp_v_w1[l], cmp_v_w2[l].T, pe8(cmp_v_pe[l]), transposed=True)
    ns = S // SLC_BLOCK
    c0 = jnp.arange(nb)[None, :] * CMP_STRIDE
    b0 = jnp.arange(ns)[:, None] * SLC_BLOCK
    ov = jnp.maximum(jnp.minimum(c0 + CMP_BLOCK, b0 + SLC_BLOCK) - jnp.maximum(c0, b0), 0)
    overlap_t = (ov.astype(F32) / CMP_BLOCK).astype(BF16)
    o_nsa_t = _nsa(qt, ksw, vt, kcmp, vcmpt, overlap_t, glt)

    bd = (jnp.arange(W)[:, None] // Dh == jnp.arange(W)[None, :] // Dh).astype(BF16)
    row = lambda a: a.reshape(1, -1)
    r, lw, k, v, kk, a, g = _rwkv_prep(
        rw, mu, row(rwkv_w0[l]), _pad_rows(rwkv_w2[l], LANE), row(rwkv_a0[l]), _pad_rows(rwkv_a2[l], LANE),
        _pad_rows(rwkv_g2[l], LANE), row(rwkv_k_k[l]), row(rwkv_k_a[l]), bd)
    o_rwkv = _rwkv_scan(r, lw, k, v, kk, a, g, row(rwkv_ln_w[l]), row(rwkv_ln_b[l]), row(rwkv_r_k[l]))

    wo = w_out[l].astype(BF16)
    rb = jnp.concatenate([router_b[l], jnp.full((LANE - N_EXPERTS,), NEG_INF, F32)]).reshape(1, LANE)
    x1, h2, idx, wgt = _outproj(o_nsa_t, o_rwkv, x, nsa_out_norm[l].reshape(-1, 1), wo[0:NSA_WIDTH], wo[NSA_WIDTH:],
                                row(mix_post_norm[l]), g_m, row(ffn_pre_norm[l]), 1.0 + sc_f, sh_f,
                                _pad_cols(router_w[l], LANE), rb)

    n_assign = T * TOP_K
    idx4 = idx.reshape(T, LANE)[:, 0:TOP_K]
    e_ids = jnp.arange(N_EXPERTS, dtype=I32)
    hot = idx4[:, :, None] == e_ids
    onehot = hot.astype(I32).sum(axis=1)
    csum = jnp.cumsum(onehot, axis=0)
    counts = csum[-1]
    starts = jnp.cumsum(counts) - counts
    padded = ((counts + MOE_BLOCK - 1) // MOE_BLOCK) * MOE_BLOCK
    pad_ends = jnp.cumsum(padded)
    pad_starts = pad_ends - padded
    pos = jnp.where(hot, (csum - onehot + pad_starts)[:, None, :], 0).sum(axis=-1).astype(I32)
    n_blocks = -(-n_assign // MOE_BLOCK) + N_EXPERTS
    P = n_blocks * MOE_BLOCK
    blk_start = jnp.arange(n_blocks, dtype=I32) * MOE_BLOCK
    blk_expert = jnp.minimum((pad_ends[None, :] <= blk_start[:, None]).sum(axis=1), N_EXPERTS - 1).astype(I32)
    n_used = (pad_ends[-1] // MOE_BLOCK).astype(I32).reshape(1)
    order = jnp.argsort(idx4.reshape(-1))
    blk_off = blk_start - pad_starts[blk_expert]
    run_start = jnp.minimum(starts[blk_expert] + blk_off, n_assign).astype(I32)
    nvalid = jnp.clip(counts[blk_expert] - blk_off, 0, MOE_BLOCK).astype(I32)
    slot_tok = _slot_runs(run_start, nvalid, (order // TOP_K).astype(I32))

    x_sorted = h2.reshape(T, D)[slot_tok]
    y_sorted = _experts(blk_expert, n_used, x_sorted, expert_w_gate_up[l], expert_b_gate_up[l],
                        expert_w_down[l], expert_b_down[l])
    y4 = y_sorted[pos.T]
    return _combine(y4, wgt, x1, row(ffn_post_norm[l]), g_f)


def kernel(x, c, ada_w, ada_b, mix_pre_norm, mix_post_norm, ffn_pre_norm, ffn_post_norm, w_in, cmp_k_pe, cmp_k_w1, cmp_k_w2, cmp_v_pe, cmp_v_w1, cmp_v_w2, nsa_out_norm, rwkv_mu, rwkv_w0, rwkv_w2, rwkv_a0, rwkv_a2, rwkv_g2, rwkv_k_k, rwkv_k_a, rwkv_r_k, rwkv_ln_w, rwkv_ln_b, w_out, router_w, router_b, expert_w_gate_up, expert_b_gate_up, expert_w_down, expert_b_down):
    for l in range(ada_w.shape[0]):
        ada = _ada(c, ada_w[l], ada_b[l])
        x = _layer(x, ada, l, mix_pre_norm, mix_post_norm, ffn_pre_norm, ffn_post_norm, w_in, cmp_k_pe,
                   cmp_k_w1, cmp_k_w2, cmp_v_pe, cmp_v_w1, cmp_v_w2, nsa_out_norm, rwkv_mu, rwkv_w0, rwkv_w2,
                   rwkv_a0, rwkv_a2, rwkv_g2, rwkv_k_k, rwkv_k_a, rwkv_r_k, rwkv_ln_w, rwkv_ln_b, w_out,
                   router_w, router_b, expert_w_gate_up, expert_b_gate_up, expert_w_down, expert_b_down)
    return x
```

```python
import functools
import math

import jax
import jax.numpy as jnp
from jax import lax
from jax.experimental import pallas as pl
from jax.experimental.pallas import tpu as pltpu

F32 = jnp.float32
BF16 = jnp.bfloat16
I32 = jnp.int32

HEAD_DIM = 64
NSA_HEADS = 8
NSA_KV_GROUPS = 2
NSA_HPG = NSA_HEADS // NSA_KV_GROUPS
NSA_WIDTH = NSA_HEADS * HEAD_DIM
NSA_KV_WIDTH = NSA_KV_GROUPS * HEAD_DIM
CMP_BLOCK = 32
CMP_STRIDE = 16
SLC_BLOCK = 64
SLC_TOPK = 16
WINDOW = 512
Q_BLOCK = 256
RWKV_HEADS = 8
RWKV_WIDTH = RWKV_HEADS * HEAD_DIM
RWKV_W_LORA = 32
RWKV_A_LORA = 32
RWKV_G_LORA = 96
DECAY_SCALE = math.exp(-0.5)
GN_EPS = 64e-5
N_EXPERTS = 32
TOP_K = 4
SWIGLU_LIMIT = 7.0
SWIGLU_ALPHA = 1.702
ROPE_THETA = 10000.0
RMS_EPS = 1e-6
NEG_INF = -1e30
FORCED = 1e9
LOWEST = -3e38

LANE = 128
ROPE_COLS = NSA_WIDTH + 3 * NSA_KV_WIDTH
REST_COLS = 4 * LANE
RW_COLS = 3 * RWKV_WIDTH + 3 * LANE
SLC_KT = 512
SLC_UNROLL = 1
FLASH_LANES = 1024
FLASH_DEPTH = 1
CMP_TIERS = 4
ACC_ROWS = HEAD_DIM + 16
RWKV_CHUNK = 128
TOKEN_TILE = 512
RWKV_BATCH_ROWS = 1
MOE_BLOCK = 512
EXPERT_SPLIT = 1
VMEM_LIMIT = 56 * 1024 * 1024


def _bdot(a, b):
    return jnp.dot(a.astype(BF16), b.astype(BF16), preferred_element_type=F32)


def _split2(a):
    hi = a.astype(BF16)
    lo = (a - hi.astype(F32)).astype(BF16)
    return hi, lo


def _dot_lhs2(a, b_bf16):
    hi, lo = _split2(a)
    return (jnp.dot(hi, b_bf16, preferred_element_type=F32)
            + jnp.dot(lo, b_bf16, preferred_element_type=F32))


def _dot3(a, b):
    ah, al = _split2(a)
    bh, bl = _split2(b)
    return (jnp.dot(ah, bh, preferred_element_type=F32)
            + jnp.dot(al, bh, preferred_element_type=F32)
            + jnp.dot(ah, bl, preferred_element_type=F32))


def _params(*sem, flags=None):
    return pltpu.CompilerParams(dimension_semantics=sem, vmem_limit_bytes=VMEM_LIMIT, flags=flags)


def _ada_kernel(c_ref, w_ref, b_ref, o_ref):
    c = c_ref[...]
    act = c * jax.nn.sigmoid(c)
    o_ref[...] = _dot3(act, w_ref[...]) + b_ref[...]


def _ada(c, w, b):
    B, D = c.shape
    N = w.shape[1]
    tn = 1536
    return pl.pallas_call(
        _ada_kernel,
        grid=(N // tn,),
        in_specs=[pl.BlockSpec((B, D), lambda j: (0, 0)),
                  pl.BlockSpec((D, tn), lambda j: (0, j)),
                  pl.BlockSpec((1, tn), lambda j: (0, j))],
        out_specs=pl.BlockSpec((B, tn), lambda j: (0, j)),
        out_shape=jax.ShapeDtypeStruct((B, N), F32),
        compiler_params=_params("arbitrary"),
        name="ada",
    )(c, w, b.reshape(1, N))


def _inproj_kernel(x_ref, g_ref, sc_ref, sh_ref, w_ref, cos_ref, sin_ref,
                   q_ref, kc_ref, ksw_ref, vc_ref, vsw_ref, gl_ref, rw_ref):
    x = x_ref[0]
    ms = jnp.mean(x * x, axis=-1, keepdims=True)
    h = x * lax.rsqrt(ms + RMS_EPS) * g_ref[...]
    h = h * sc_ref[0] + sh_ref[0]
    hb = h.astype(BF16)
    R = ROPE_COLS
    main = jnp.dot(hb, w_ref[:, 0:R], preferred_element_type=F32)
    rot = jnp.dot(hb, w_ref[:, R:2 * R], preferred_element_type=F32)
    roped = main * cos_ref[...] + rot * sin_ref[...]
    q_ref[0] = roped[:, 0:NSA_WIDTH].T.astype(BF16)
    kc_ref[0] = roped[:, NSA_WIDTH:NSA_WIDTH + LANE]
    ksw_ref[0] = roped[:, NSA_WIDTH + LANE:R].astype(BF16)
    rest = jnp.dot(hb, w_ref[:, 2 * R:2 * R + REST_COLS], preferred_element_type=F32)
    vc_ref[0] = rest[:, 0:LANE]
    vsw_ref[0] = rest[:, LANE:3 * LANE].T.astype(BF16)
    gl_ref[0] = rest[:, 3 * LANE:4 * LANE].T
    rw_ref[0] = jnp.dot(hb, w_ref[:, 2 * R + REST_COLS:], preferred_element_type=F32)


def _inproj(x, gain, sc1, sh, w_all, cos_t, sin_t):
    B, S, D = x.shape
    tm = min(256, S)
    NW = w_all.shape[1]
    R = ROPE_COLS
    tok = lambda n: pl.BlockSpec((1, tm, n), lambda b, i: (b, i, 0))
    tok_t = lambda n: pl.BlockSpec((1, n, tm), lambda b, i: (b, 0, i))
    vec = pl.BlockSpec((1, 1, D), lambda b, i: (b, 0, 0))
    outs = [(NSA_WIDTH, BF16, True), (LANE, F32, False), (2 * LANE, BF16, False), (LANE, F32, False),
            (2 * LANE, BF16, True), (LANE, F32, True), (RW_COLS, F32, False)]
    return pl.pallas_call(
        _inproj_kernel,
        grid=(B, S // tm),
        in_specs=[tok(D), pl.BlockSpec((1, D), lambda b, i: (0, 0)), vec, vec,
                  pl.BlockSpec((D, NW), lambda b, i: (0, 0)),
                  pl.BlockSpec((tm, R), lambda b, i: (i, 0)),
                  pl.BlockSpec((tm, R), lambda b, i: (i, 0))],
        out_specs=[tok_t(n) if tr else tok(n) for n, _, tr in outs],
        out_shape=[jax.ShapeDtypeStruct((B, n, S) if tr else (B, S, n), dt) for n, dt, tr in outs],
        compiler_params=_params("parallel", "arbitrary"),
        name="inproj",
    )(x, gain.reshape(1, D), sc1, sh, w_all, cos_t, sin_t)


def _gelu_tanh(x):
    return 0.5 * x * (1.0 + jnp.tanh(math.sqrt(2.0 / math.pi) * (x + 0.044715 * (x * x * x))))


def _compress_kernel(r_ref, w1g_ref, w1_ref, w2_ref, pe_ref, o_ref, *, transposed):
    R = r_ref[0].astype(BF16)
    nb = R.shape[0]
    top = jnp.dot(R, w1g_ref[0, 0].astype(BF16), preferred_element_type=F32)
    bot = jnp.dot(R, w1g_ref[0, 1].astype(BF16), preferred_element_type=F32)
    pe_term = jnp.dot(pe_ref[...].astype(BF16), w1_ref[...].astype(BF16), preferred_element_type=F32)[0:1]
    bot_next = pltpu.roll(bot, nb - 1, 0)
    hid = _gelu_tanh(top + bot_next + pe_term)
    if transposed:
        out = jnp.dot(w2_ref[...].astype(BF16), hid.T.astype(BF16), preferred_element_type=F32)
    else:
        out = jnp.dot(hid.astype(BF16), w2_ref[...].astype(BF16), preferred_element_type=F32)
    o_ref[0, 0] = out.astype(BF16)


def _compress(tok, w1, w2, pe, transposed):
    B, S, GD = tok.shape
    G, Dh = NSA_KV_GROUPS, HEAD_DIM
    nb = S // CMP_STRIDE
    hid = w1.shape[-1]
    width = CMP_STRIDE * GD
    rr = tok.reshape(B, nb, width)
    w1r = w1.reshape(2, CMP_STRIDE, Dh, hid)
    w1g = jnp.zeros((G, 2, CMP_STRIDE, G, Dh, hid), w1.dtype)
    for g in range(G):
        w1g = w1g.at[g, :, :, g].set(w1r)
    w1g = w1g.reshape(G, 2, width, hid)
    oshape = (Dh, nb) if transposed else (nb, Dh)
    return pl.pallas_call(
        functools.partial(_compress_kernel, transposed=transposed),
        grid=(B, G),
        in_specs=[pl.BlockSpec((1, nb, width), lambda b, g: (b, 0, 0)),
                  pl.BlockSpec((1, 2, width, hid), lambda b, g: (g, 0, 0, 0)),
                  pl.BlockSpec(w1.shape, lambda b, g: (0, 0)),
                  pl.BlockSpec(w2.shape, lambda b, g: (0, 0)),
                  pl.BlockSpec(pe.shape, lambda b, g: (0, 0))],
        out_specs=pl.BlockSpec((1, 1) + oshape, lambda b, g: (b, g, 0, 0)),
        out_shape=jax.ShapeDtypeStruct((B, G) + oshape, BF16),
        compiler_params=_params("arbitrary", "arbitrary"),
        name="compress_v" if transposed else "compress_k",
    )(rr, w1g, w1, w2, pe)


def _lanes4(a):
    return jnp.concatenate([a, a, a, a], axis=1)


def _nsa_kernel(qt_ref, ksw_ref, vt_ref, kcmp_ref, vcmpt_ref, ovt_ref, glt_ref, o_ref, *, seq_len):
    QB, Dh, HPG = Q_BLOCK, HEAD_DIM, NSA_HPG
    qi = pl.program_id(1)
    s0 = qi * QB
    ns = seq_len // SLC_BLOCK
    ncp = seq_len // CMP_STRIDE
    n_sel = min(SLC_TOPK, ns)
    kt_slc = min(SLC_KT, seq_len)
    t_row = s0 + lax.broadcasted_iota(I32, (1, QB), 1)
    t4 = _lanes4(t_row)
    gates = jax.nn.sigmoid(glt_ref[0])
    blk = lax.broadcasted_iota(I32, (ns, QB), 0)
    cur = lax.shift_right_logical(t_row, 6)
    forced = (blk == 0) | (blk == cur) | (blk == cur - 1)
    future = blk * SLC_BLOCK > t_row
    cmp_last = lax.broadcasted_iota(I32, (ncp, 1), 0) * CMP_STRIDE + (CMP_BLOCK - 1)
    cmask = cmp_last <= t4
    zeros_q = jnp.zeros((Dh, HPG * QB), BF16)
    G = NSA_KV_GROUPS
    NL = G * HPG * QB
    qgs = [jnp.concatenate([qt_ref[0, (g * HPG + h) * Dh:(g * HPG + h + 1) * Dh, :] for h in range(HPG)], axis=1)
           for g in range(G)]
    qpads = [jnp.concatenate([qgs[g], zeros_q] if g == 0 else [zeros_q, qgs[g]], axis=0) for g in range(G)]

    def cmp_branch(rows):
        def run():
            outs = []
            for g in range(G):
                sc = jnp.dot(kcmp_ref[0, g, 0:rows, :], qgs[g], preferred_element_type=F32)
                sc = jnp.where(cmask[0:rows], sc, NEG_INF)
                m = jnp.max(sc, axis=0, keepdims=True)
                p = jnp.where(cmask[0:rows], jnp.exp2(sc - m), 0.0)
                l = jnp.sum(p, axis=0, keepdims=True)
                p = p * (1.0 / jnp.maximum(l, 1e-30))
                o_cmp = jnp.dot(vcmpt_ref[0, g, :, 0:rows], p.astype(BF16), preferred_element_type=F32)
                psum = p[:, 0:QB] + p[:, QB:2 * QB] + p[:, 2 * QB:3 * QB] + p[:, 3 * QB:4 * QB]
                p_hi, p_lo = _split2(psum)
                imp = (jnp.dot(ovt_ref[:, 0:rows], p_hi, preferred_element_type=F32)
                       + jnp.dot(ovt_ref[:, 0:rows], p_lo, preferred_element_type=F32))
                outs += [o_cmp, imp]
            return tuple(outs)
        return run

    n_tier = CMP_TIERS if ncp % (CMP_TIERS * 16) == 0 else 1
    tier_rows = ncp // n_tier
    n_visible = (s0 + QB - CMP_BLOCK) // CMP_STRIDE + 1
    tier = jnp.clip((n_visible - 1) // tier_rows, 0, n_tier - 1)
    cmp_out = lax.switch(tier, [cmp_branch((k + 1) * tier_rows) for k in range(n_tier)])
    o_cmps = [cmp_out[2 * g] for g in range(G)]
    imps = [jnp.where(future, NEG_INF, jnp.where(forced, FORCED, cmp_out[2 * g + 1])) for g in range(G)]

    vals = jnp.concatenate(imps, axis=1)
    blk2 = jnp.concatenate([blk] * G, axis=1)
    sel = jnp.zeros((ns, G * QB), F32)
    for _ in range(n_sel):
        mx = jnp.max(vals, axis=0, keepdims=True)
        idx = jnp.min(jnp.where(vals == mx, blk2, ns), axis=0, keepdims=True)
        pick = blk2 == idx
        sel = jnp.where(pick, 1.0, sel)
        vals = jnp.where(pick, LOWEST, vals)
    sel_bias = jnp.where(sel > 0.5, 0.0, NEG_INF).astype(BF16)
    sel_rows = jnp.concatenate([sel_bias[:, g * QB:(g + 1) * QB] for g in range(G) for _ in range(HPG)], axis=1)
    n_slab = NL // FLASH_LANES
    q_slabs = [jnp.concatenate(qpads, axis=1)[:, sb * FLASH_LANES:(sb + 1) * FLASH_LANES]
               for sb in range(n_slab)]
    qsel_slabs = [jnp.concatenate([q_slabs[sb], sel_rows[:, sb * FLASH_LANES:(sb + 1) * FLASH_LANES]], axis=0)
                  for sb in range(n_slab)]

    def load_tile(k_lane0, vrow0, k0, width, bias, first_block=None):
        ones = jnp.ones((ACC_ROWS - Dh, width), BF16)
        kt = ksw_ref[0, pl.ds(k0, width), k_lane0:k_lane0 + LANE]
        if first_block is not None:
            key_blk = first_block + lax.shift_right_logical(lax.broadcasted_iota(I32, (width, ns), 0), 6)
            onehot = (lax.broadcasted_iota(I32, (width, ns), 1) == key_blk).astype(BF16)
            kt = jnp.concatenate([kt, onehot], axis=1)
        vaug = [jnp.concatenate([vt_ref[0, vrow0 + g * Dh:vrow0 + (g + 1) * Dh, pl.ds(k0, width)], ones], axis=0)
                for g in range(G)]
        return kt, vaug, bias, first_block is not None

    group_lanes = HPG * QB
    groups_per_slab = max(FLASH_LANES // group_lanes, 1)
    lanes_per_part = FLASH_LANES // groups_per_slab

    def flash_tiles(tiles, carry):
        m, acc = list(carry[0]), list(carry[1])
        steps = [(ti, sb) for ti in range(len(tiles)) for sb in range(n_slab)]
        scores = {}
        for i in range(len(steps) + FLASH_DEPTH):
            if i < len(steps):
                ti, sb = steps[i]
                rhs = qsel_slabs[sb] if tiles[ti][3] else q_slabs[sb]
                scores[i] = jnp.dot(tiles[ti][0], rhs, preferred_element_type=F32)
            j = i - FLASH_DEPTH
            if j >= 0:
                ti, sb = steps[j]
                _, vaug, bias, _ = tiles[ti]
                g0 = sb * FLASH_LANES // group_lanes
                s = scores.pop(j)
                if bias is not None:
                    s = s + jnp.concatenate([bias] * (FLASH_LANES // QB), axis=1)
                m_new = jnp.maximum(m[sb], jnp.max(s, axis=0, keepdims=True))
                alpha = jnp.exp2(m[sb] - m_new)
                pj = jnp.exp2(s - m_new).astype(BF16)
                pv = [jnp.dot(vaug[g0 + p], pj[:, p * lanes_per_part:(p + 1) * lanes_per_part],
                              preferred_element_type=F32) for p in range(groups_per_slab)]
                acc[sb] = alpha * acc[sb] + (pv[0] if len(pv) == 1 else jnp.concatenate(pv, axis=1))
                m[sb] = m_new
        return tuple(m), tuple(acc)

    def slc_tile(j, causal_mask):
        k0 = pl.multiple_of(j * kt_slc, kt_slc)
        bias = None
        if causal_mask:
            bias = jnp.where(k0 + lax.broadcasted_iota(I32, (kt_slc, 1), 0) <= t_row, 0.0, NEG_INF)
        return load_tile(0, 0, k0, kt_slc, bias, first_block=j * (kt_slc // SLC_BLOCK))

    assert kt_slc % QB == 0 or seq_len == kt_slc
    init = (tuple(jnp.full((1, FLASH_LANES), NEG_INF, F32) for _ in range(n_slab)),
            tuple(jnp.zeros((ACC_ROWS, FLASH_LANES), F32) for _ in range(n_slab)))
    n_full = (s0 + 1) // kt_slc
    carry = lax.fori_loop(0, n_full, lambda j, c: flash_tiles([slc_tile(j, False)], c), init)
    _, acc_s = flash_tiles([slc_tile(n_full, True)], carry)
    acc_s = jnp.concatenate(acc_s, axis=1)
    o_slc = acc_s[0:Dh] * (1.0 / acc_s[Dh:Dh + 1])

    win_keys = min(WINDOW + QB, seq_len)
    k0 = pl.multiple_of(jnp.maximum(qi - WINDOW // QB, 0) * QB, QB)
    kpos = k0 + lax.broadcasted_iota(I32, (win_keys, 1), 0)
    wbias = jnp.where((kpos <= t_row) & (kpos > t_row - WINDOW), 0.0, NEG_INF)
    _, acc_w = flash_tiles([load_tile(LANE, LANE, k0, win_keys, wbias)], init)
    acc_w = jnp.concatenate(acc_w, axis=1)
    o_win = acc_w[0:Dh] * (1.0 / acc_w[Dh:Dh + 1])

    for g in range(G):
        for h in range(HPG):
            c0 = (g * HPG + h) * 3
            cols = slice((g * HPG + h) * QB, (g * HPG + h + 1) * QB)
            o = (gates[c0:c0 + 1, :] * o_cmps[g][:, h * QB:(h + 1) * QB] + gates[c0 + 1:c0 + 2, :] * o_slc[:, cols]
                 + gates[c0 + 2:c0 + 3, :] * o_win[:, cols])
            o_ref[0, (g * HPG + h) * Dh:(g * HPG + h + 1) * Dh, :] = o


def _nsa(qt, ksw, vt, kcmp, vcmpt, overlap_t, glt):
    B, _, S = qt.shape
    ncp = S // CMP_STRIDE
    ns = S // SLC_BLOCK
    G = NSA_KV_GROUPS
    return pl.pallas_call(
        functools.partial(_nsa_kernel, seq_len=S),
        grid=(B, S // Q_BLOCK),
        in_specs=[pl.BlockSpec((1, NSA_WIDTH, Q_BLOCK), lambda b, i: (b, 0, i)),
                  pl.BlockSpec((1, S, 2 * LANE), lambda b, i: (b, 0, 0)),
                  pl.BlockSpec((1, 2 * LANE, S), lambda b, i: (b, 0, 0)),
                  pl.BlockSpec((1, G, ncp, HEAD_DIM), lambda b, i: (b, 0, 0, 0)),
                  pl.BlockSpec((1, G, HEAD_DIM, ncp), lambda b, i: (b, 0, 0, 0)),
                  pl.BlockSpec((ns, ncp), lambda b, i: (0, 0)),
                  pl.BlockSpec((1, LANE, Q_BLOCK), lambda b, i: (b, 0, i))],
        out_specs=pl.BlockSpec((1, NSA_WIDTH, Q_BLOCK), lambda b, i: (b, 0, i)),
        out_shape=jax.ShapeDtypeStruct((B, NSA_WIDTH, S), F32),
        compiler_params=_params("parallel", "arbitrary"),
        name="nsa",
    )(qt, ksw, vt, kcmp, vcmpt, overlap_t, glt)


def _rwkv_prep_kernel(f_ref, p_ref, mu_ref, w0_ref, w2_ref, a0_ref, a2_ref, g2_ref, kk_ref, ka_ref,
                      bd_ref, r_o, lw_o, k_o, v_o, kk_o, a_o, g_o):
    W = RWKV_WIDTH
    i = pl.program_id(1)
    f = f_ref[0]
    tm = f.shape[0]
    prev_last = jnp.where(i > 0, p_ref[0, 7:8, :], 0.0)
    rolled = pltpu.roll(f, 1, 0)
    row = lax.broadcasted_iota(I32, (tm, 1), 0)
    prev = jnp.where(row == 0, prev_last, rolled)
    f = f + (prev - f) * mu_ref[...]
    r, k, v = f[:, 0:W], f[:, W:2 * W], f[:, 2 * W:3 * W]
    wd = f[:, 3 * W:3 * W + LANE]
    ad = f[:, 3 * W + LANE:3 * W + 2 * LANE]
    gd = f[:, 3 * W + 2 * LANE:3 * W + 3 * LANE]
    lw_o[0] = -DECAY_SCALE * jax.nn.sigmoid(w0_ref[...] + _dot3(jnp.tanh(wd), w2_ref[...]))
    a = jax.nn.sigmoid(a0_ref[...] + _dot3(ad, a2_ref[...]))
    g_o[0] = _dot3(jax.nn.sigmoid(gd), g2_ref[...])
    kk = k * kk_ref[...]
    ss = _dot_lhs2(kk * kk, bd_ref[...])
    kk_o[0] = kk / jnp.maximum(jnp.sqrt(ss), 1e-12)
    k_o[0] = k * (1.0 + (a - 1.0) * ka_ref[...])
    r_o[0] = r
    v_o[0] = v
    a_o[0] = a


def _rwkv_prep(rw, mu, w0, w2, a0, a2, g2, k_k, k_a, bd):
    B, S, C = rw.shape
    W = RWKV_WIDTH
    tm = min(TOKEN_TILE, S)
    row = lambda n: pl.BlockSpec((1, n), lambda b, i: (0, 0))
    mat = pl.BlockSpec((LANE, W), lambda b, i: (0, 0))
    tok = pl.BlockSpec((1, tm, W), lambda b, i: (b, i, 0))
    return pl.pallas_call(
        _rwkv_prep_kernel,
        grid=(B, S // tm),
        in_specs=[pl.BlockSpec((1, tm, C), lambda b, i: (b, i, 0)),
                  pl.BlockSpec((1, 8, C), lambda b, i: (b, jnp.maximum(i * (tm // 8) - 1, 0), 0)),
                  row(C), row(W), mat, row(W), mat, mat, row(W), row(W),
                  pl.BlockSpec((W, W), lambda b, i: (0, 0))],
        out_specs=[tok] * 7,
        out_shape=[jax.ShapeDtypeStruct((B, S, W), F32)] * 7,
        compiler_params=_params("parallel", "arbitrary"),
        name="rwkv_prep",
    )(rw, rw, mu, w0, w2, a0, a2, g2, k_k, k_a, bd)


def _rwkv_scan_kernel(r_ref, lw_ref, k_ref, v_ref, kk_ref, a_ref, g_ref, lnw_ref, lnb_ref, rk_ref,
                      o_ref, st_ref):
    Dh = HEAD_DIM

    @pl.when(pl.program_id(1) == 0)
    def _():
        st_ref[...] = jnp.zeros_like(st_ref)

    NB, C = r_ref.shape[0], r_ref.shape[1]
    ri = lax.broadcasted_iota(I32, (C, C), 0)
    ci = lax.broadcasted_iota(I32, (C, C), 1)
    incl = ri >= ci
    strict = ri > ci
    tri = incl.astype(BF16)
    eye = (ri == ci).astype(F32)
    n_double = max(int(math.log2(C)) - 1, 0)

    def scaled_operands(bb):
        r, lw, k, kk, a = (ref[bb] for ref in (r_ref, lw_ref, k_ref, kk_ref, a_ref))
        l_hi = lw.astype(BF16)
        l_r1 = lw - l_hi.astype(F32)
        l_mid = l_r1.astype(BF16)
        l_lo = (l_r1 - l_mid.astype(F32)).astype(BF16)
        L = (jnp.dot(tri, l_hi, preferred_element_type=F32) + jnp.dot(tri, l_mid, preferred_element_type=F32)
             + jnp.dot(tri, l_lo, preferred_element_type=F32))
        e_l = jnp.exp(L)
        e_inv = jnp.exp(-L)
        e_end = jnp.exp(L[C - 1:C, :] - L)
        b = kk * a
        x1 = jnp.concatenate([kk * jnp.exp(L - lw), r * e_l], axis=0).astype(BF16)
        x2t = jnp.concatenate([k * e_inv, b * e_inv], axis=0).T.astype(BF16)
        zt = jnp.concatenate([k * e_end, -(b * e_end)], axis=0).T.astype(BF16)
        gt = jnp.concatenate([e_l, e_l], axis=0).T
        return x1, x2t, zt, gt

    ops = [scaled_operands(bb) for bb in range(NB)]
    chains = [(bb, h) for bb in range(NB) for h in range(RWKV_HEADS)]
    sls = [slice(h * Dh, (h + 1) * Dh) for _, h in chains]
    h0s = [st_ref[bb, h] for bb, h in chains]
    vhs = [v_ref[bb][:, sl] for (bb, _), sl in zip(chains, sls)]
    x1s = [ops[bb][0][:, sl] for (bb, _), sl in zip(chains, sls)]
    amats = [jnp.dot(x1, ops[bb][1][sl, :], preferred_element_type=F32)
             for x1, (bb, _), sl in zip(x1s, chains, sls)]
    p0s = [jnp.dot(x1, h0.astype(BF16), preferred_element_type=F32) for x1, h0 in zip(x1s, h0s)]
    a_kk = [jnp.where(strict, am[0:C, 0:C], 0.0).astype(BF16) for am in amats]
    a_rkb = [jnp.concatenate([jnp.where(incl, am[C:2 * C, 0:C], 0.0), jnp.where(incl, -am[C:2 * C, C:2 * C], 0.0)],
                             axis=1).astype(BF16) for am in amats]
    nmats = [jnp.where(strict, -am[0:C, C:2 * C], 0.0) for am in amats]
    tinvs = [eye + nm for nm in nmats]
    for _ in range(n_double):
        nmats = [_bdot(nm, nm) for nm in nmats]
        tinvs = [ti + _bdot(ti, nm) for ti, nm in zip(tinvs, nmats)]
    akv = [jnp.dot(ak, vh.astype(BF16), preferred_element_type=F32) for ak, vh in zip(a_kk, vhs)]
    us = [_bdot(ti, p0[0:C] + av) for ti, p0, av in zip(tinvs, p0s, akv)]
    vus = [jnp.concatenate([vh, u], axis=0).astype(BF16) for vh, u in zip(vhs, us)]
    ys = [p0[C:2 * C] + jnp.dot(ar, vu, preferred_element_type=F32) for p0, ar, vu in zip(p0s, a_rkb, vus)]
    for i, (bb, h) in enumerate(chains):
        _, _, zt, gt = ops[bb]
        st_ref[bb, h] = gt[sls[i], C - 1:C] * h0s[i] + jnp.dot(zt[sls[i], :], vus[i], preferred_element_type=F32)
    for i, (bb, h) in enumerate(chains):
        sl, y = sls[i], ys[i]
        mean = jnp.mean(y, axis=-1, keepdims=True)
        yc = y - mean
        var = jnp.mean(yc * yc, axis=-1, keepdims=True)
        yn = yc * lax.rsqrt(var + GN_EPS) * lnw_ref[:, sl] + lnb_ref[:, sl]
        bonus = jnp.sum(r_ref[bb][:, sl] * k_ref[bb][:, sl] * rk_ref[:, sl], axis=-1, keepdims=True) * vhs[i]
        o_ref[bb, :, sl] = (yn + bonus) * g_ref[bb][:, sl]


def _rwkv_scan(r, lw, k, v, kk, a, g, ln_w, ln_b, r_k):
    B, S, W = r.shape
    C = min(RWKV_CHUNK, S)
    nb = math.gcd(RWKV_BATCH_ROWS, B)
    tok = pl.BlockSpec((nb, C, W), lambda b, i: (b, i, 0))
    row = pl.BlockSpec((1, W), lambda b, i: (0, 0))
    return pl.pallas_call(
        _rwkv_scan_kernel,
        grid=(B // nb, S // C),
        in_specs=[tok] * 7 + [row] * 3,
        out_specs=tok,
        out_shape=jax.ShapeDtypeStruct((B, S, W), F32),
        scratch_shapes=[pltpu.VMEM((nb, RWKV_HEADS, HEAD_DIM, HEAD_DIM), F32)],
        compiler_params=_params("parallel", "arbitrary"),
        name="rwkv_scan",
    )(r, lw, k, v, kk, a, g, ln_w, ln_b, r_k)


def _rms(x, g):
    return x * lax.rsqrt(jnp.mean(x * x, axis=-1, keepdims=True) + RMS_EPS) * g


def _outproj_kernel(on_ref, or_ref, x_ref, gn_ref, wt_ref, wb_ref, gpost_ref, gm_ref, gpre_ref,
                    scf_ref, shf_ref, rw_ref, rb_ref, x1_ref, h2_ref, idx_ref, wgt_ref):
    ont = on_ref[0]
    ont = ont * lax.rsqrt(jnp.mean(ont * ont, axis=0, keepdims=True) + RMS_EPS) * gn_ref[...]
    on = ont.T
    mixed = (jnp.dot(on.astype(BF16), wt_ref[...], preferred_element_type=F32)
             + jnp.dot(or_ref[0].astype(BF16), wb_ref[...], preferred_element_type=F32))
    x1 = x_ref[0] + gm_ref[0] * _rms(mixed, gpost_ref[...])
    x1_ref[0] = x1
    h2 = _rms(x1, gpre_ref[...]) * scf_ref[0] + shf_ref[0]
    h2_ref[0] = h2
    logits = _dot3(h2, rw_ref[...]) + rb_ref[...]
    lane = lax.broadcasted_iota(I32, logits.shape, 1)
    vals = logits
    top_v, top_i = [], []
    for _ in range(TOP_K):
        mx = jnp.max(vals, axis=-1, keepdims=True)
        ix = jnp.min(jnp.where(vals == mx, lane, LANE), axis=-1, keepdims=True)
        top_v.append(mx)
        top_i.append(ix)
        vals = jnp.where(lane == ix, LOWEST, vals)
    ex = [jnp.exp(tv - top_v[0]) for tv in top_v]
    inv = 1.0 / (ex[0] + ex[1] + ex[2] + ex[3])
    idx = jnp.zeros(logits.shape, I32)
    wgt = jnp.zeros(logits.shape, F32)
    for kk in range(TOP_K):
        idx = jnp.where(lane == kk, top_i[kk], idx)
        wgt = jnp.where(lane == kk, ex[kk] * inv, wgt)
    idx_ref[0] = idx
    wgt_ref[0] = wgt


def _outproj(o_nsa, o_rwkv, x, gn, w_top, w_bot, g_post, g_m, g_pre, sc1_f, sh_f, router_w, router_b):
    B, S, D = x.shape
    tm = min(TOKEN_TILE, S)
    W = o_rwkv.shape[-1]
    tok = lambda n: pl.BlockSpec((1, tm, n), lambda b, i: (b, i, 0))
    vec = pl.BlockSpec((1, 1, D), lambda b, i: (b, 0, 0))
    row = lambda n: pl.BlockSpec((1, n), lambda b, i: (0, 0))
    return pl.pallas_call(
        _outproj_kernel,
        grid=(B, S // tm),
        in_specs=[pl.BlockSpec((1, W, tm), lambda b, i: (b, 0, i)), tok(W), tok(D),
                  pl.BlockSpec((W, 1), lambda b, i: (0, 0)),
                  pl.BlockSpec((W, D), lambda b, i: (0, 0)), pl.BlockSpec((W, D), lambda b, i: (0, 0)),
                  row(D), vec, row(D), vec, vec,
                  pl.BlockSpec((D, LANE), lambda b, i: (0, 0)), row(LANE)],
        out_specs=[tok(D), tok(D), tok(LANE), tok(LANE)],
        out_shape=[jax.ShapeDtypeStruct((B, S, D), F32), jax.ShapeDtypeStruct((B, S, D), F32),
                   jax.ShapeDtypeStruct((B, S, LANE), I32), jax.ShapeDtypeStruct((B, S, LANE), F32)],
        compiler_params=_params("parallel", "arbitrary"),
        name="outproj",
    )(o_nsa, o_rwkv, x, gn, w_top, w_bot, g_post, g_m, g_pre, sc1_f, sh_f, router_w, router_b)


def _slot_runs_kernel(start_ref, nvalid_ref, tok_ref, o_ref):
    rows = MOE_BLOCK // LANE
    lane = lax.broadcasted_iota(I32, (rows, LANE), 1)
    within = lax.broadcasted_iota(I32, (rows, LANE), 0) * LANE + lane

    def body(b, carry):
        s = start_ref[b]
        off = jnp.bitwise_and(s, LANE - 1)
        win = tok_ref[pl.ds(lax.shift_right_logical(s, 7), rows + 1), :]
        win = pltpu.roll(win, jnp.bitwise_and(LANE - off, LANE - 1), 1)
        run = jnp.where(lane < LANE - off, win[0:rows], win[1:rows + 1])
        o_ref[pl.ds(pl.multiple_of(b * rows, rows), rows), :] = jnp.where(within < nvalid_ref[b], run, 0)
        return carry

    lax.fori_loop(0, start_ref.shape[0], body, 0)


def _slot_runs(run_start, nvalid, tok_sorted):
    nb = run_start.shape[0]
    rows = MOE_BLOCK // LANE
    n_rows = -(-(tok_sorted.shape[0] // LANE + rows + 1) // 8) * 8
    tok2d = jnp.pad(tok_sorted, (0, n_rows * LANE - tok_sorted.shape[0])).reshape(n_rows, LANE)
    grid_spec = pltpu.PrefetchScalarGridSpec(
        num_scalar_prefetch=2,
        grid=(1,),
        in_specs=[pl.BlockSpec((n_rows, LANE), lambda i, s, n: (0, 0))],
        out_specs=pl.BlockSpec((nb * rows, LANE), lambda i, s, n: (0, 0)),
    )
    out = pl.pallas_call(
        _slot_runs_kernel,
        grid_spec=grid_spec,
        out_shape=jax.ShapeDtypeStruct((nb * rows, LANE), I32),
        compiler_params=_params("arbitrary"),
        name="slot_runs",
    )(run_start, nvalid, tok2d)
    return out.reshape(nb * MOE_BLOCK)


def _expert_kernel(be_ref, nu_ref, x_ref, wgu_ref, bgu_ref, wd_ref, bd_ref, o_ref, wgu_bf, wd_bf):
    F = wd_ref.shape[1]
    i = pl.program_id(0)
    active = i < nu_ref[0]
    new_expert = (i == 0) | (be_ref[i] != be_ref[jnp.maximum(i - 1, 0)])

    @pl.when(active & new_expert)
    def _():
        wgu_bf[...] = wgu_ref[0].astype(BF16)
        wd_bf[...] = wd_ref[0].astype(BF16)

    @pl.when(active)
    def _():
        rows = x_ref.shape[0] // EXPERT_SPLIT
        subs = [slice(j * rows, (j + 1) * rows) for j in range(EXPERT_SPLIT)]
        gus = [jnp.dot(x_ref[sl, :].astype(BF16), wgu_bf[...], preferred_element_type=F32) + bgu_ref[0] for sl in subs]
        acts = []
        for gu in gus:
            gate = jnp.minimum(gu[:, 0:F], SWIGLU_LIMIT)
            up = jnp.clip(gu[:, F:2 * F], -SWIGLU_LIMIT, SWIGLU_LIMIT)
            acts.append(((up + 1.0) * (gate * jax.nn.sigmoid(gate * SWIGLU_ALPHA))).astype(BF16))
        for sl, act in zip(subs, acts):
            out = jnp.dot(act, wd_bf[...], preferred_element_type=F32) + bd_ref[0]
            o_ref[sl, :] = out.astype(o_ref.dtype)

    @pl.when(jnp.logical_not(active))
    def _():
        o_ref[...] = jnp.zeros_like(o_ref)


def _experts(blk_expert, n_used, x_sorted, w_gu, b_gu, w_d, b_d):
    P, D = x_sorted.shape
    E, _, F2 = w_gu.shape
    F = F2 // 2
    nb = P // MOE_BLOCK
    grid_spec = pltpu.PrefetchScalarGridSpec(
        num_scalar_prefetch=2,
        grid=(nb,),
        in_specs=[pl.BlockSpec((MOE_BLOCK, D), lambda i, be, nu: (i, 0)),
                  pl.BlockSpec((1, D, F2), lambda i, be, nu: (be[i], 0, 0)),
                  pl.BlockSpec((1, 1, F2), lambda i, be, nu: (be[i], 0, 0)),
                  pl.BlockSpec((1, F, D), lambda i, be, nu: (be[i], 0, 0)),
                  pl.BlockSpec((1, 1, D), lambda i, be, nu: (be[i], 0, 0))],
        out_specs=pl.BlockSpec((MOE_BLOCK, D), lambda i, be, nu: (i, 0)),
        scratch_shapes=[pltpu.VMEM((D, F2), BF16), pltpu.VMEM((F, D), BF16)],
    )
    return pl.pallas_call(
        _expert_kernel,
        grid_spec=grid_spec,
        out_shape=jax.ShapeDtypeStruct((P, D), F32),
        compiler_params=_params("arbitrary"),
        name="experts",
    )(blk_expert, n_used, x_sorted, w_gu, b_gu.reshape(E, 1, F2), w_d, b_d.reshape(E, 1, D))


def _combine_kernel(y_ref, w_ref, x1_ref, gpost_ref, gf_ref, o_ref):
    w = w_ref[0]
    y = (w[:, 0:1] * y_ref[0].astype(F32) + w[:, 1:2] * y_ref[1].astype(F32)
         + w[:, 2:3] * y_ref[2].astype(F32) + w[:, 3:4] * y_ref[3].astype(F32))
    o_ref[0] = x1_ref[0] + gf_ref[0] * _rms(y, gpost_ref[...])


def _combine(y4, wgt, x1, g_post, g_f):
    B, S, D = x1.shape
    tm = min(TOKEN_TILE, S)
    nt = S // tm
    return pl.pallas_call(
        _combine_kernel,
        grid=(B, nt),
        in_specs=[pl.BlockSpec((TOP_K, tm, D), lambda b, i: (0, b * nt + i, 0)),
                  pl.BlockSpec((1, tm, LANE), lambda b, i: (b, i, 0)),
                  pl.BlockSpec((1, tm, D), lambda b, i: (b, i, 0)),
                  pl.BlockSpec((1, D), lambda b, i: (0, 0)),
                  pl.BlockSpec((1, 1, D), lambda b, i: (b, 0, 0))],
        out_specs=pl.BlockSpec((1, tm, D), lambda b, i: (b, i, 0)),
        out_shape=jax.ShapeDtypeStruct((B, S, D), F32),
        compiler_params=_params("parallel", "arbitrary"),
        name="combine",
    )(y4, wgt, x1, g_post, g_f)


def _rot_cols(w):
    d, n = w.shape
    w4 = w.reshape(d, n // HEAD_DIM, 2, HEAD_DIM // 2)
    return jnp.concatenate([-w4[:, :, 1:2], w4[:, :, 0:1]], axis=2).reshape(d, n)


def _pad_cols(w, n):
    return jnp.pad(w, ((0, 0), (0, n - w.shape[1])))


def _pad_rows(w, n):
    return jnp.pad(w, ((0, n - w.shape[0]), (0, 0)))


def _layer(x, ada, l, mix_pre_norm, mix_post_norm, ffn_pre_norm, ffn_post_norm, w_in, cmp_k_pe, cmp_k_w1,
           cmp_k_w2, cmp_v_pe, cmp_v_w1, cmp_v_w2, nsa_out_norm, rwkv_mu, rwkv_w0, rwkv_w2, rwkv_a0, rwkv_a2,
           rwkv_g2, rwkv_k_k, rwkv_k_a, rwkv_r_k, rwkv_ln_w, rwkv_ln_b, w_out, router_w, router_b,
           expert_w_gate_up, expert_b_gate_up, expert_w_down, expert_b_down):
    B, S, D = x.shape
    T = B * S
    G, Dh, W = NSA_KV_GROUPS, HEAD_DIM, RWKV_WIDTH
    sh_m, sc_m, g_m, sh_f, sc_f, g_f = [a.reshape(B, 1, D) for a in jnp.split(ada, 6, axis=-1)]

    wi = w_in[l]
    KV = NSA_KV_WIDTH
    o = NSA_WIDTH
    wq, wkc, wvc, wks, wvs, wkw, wvw = (wi[:, 0:o],) + tuple(wi[:, o + j * KV:o + (j + 1) * KV] for j in range(6))
    wgl = wi[:, o + 6 * KV:o + 6 * KV + 3 * NSA_HEADS]
    wr = wi[:, o + 6 * KV + 3 * NSA_HEADS:]
    w_rope = jnp.concatenate([wq, wkc, wks, wkw], axis=1)
    w_rest = _pad_cols(jnp.concatenate([wvc, wvs, wvw, wgl], axis=1), REST_COLS)
    lora = [(3 * W, RWKV_W_LORA), (3 * W + RWKV_W_LORA, RWKV_A_LORA),
            (3 * W + RWKV_W_LORA + RWKV_A_LORA, RWKV_G_LORA)]
    pad_lora = lambda a: jnp.concatenate([_pad_cols(a[..., s:s + n], LANE) for s, n in lora], axis=-1)
    w_rw = jnp.concatenate([wr[:, 0:3 * W], pad_lora(wr)], axis=1)
    w_all = jnp.concatenate([w_rope, _rot_cols(w_rope), w_rest, w_rw], axis=1).astype(BF16)
    mu_row = rwkv_mu[l].reshape(1, -1)
    mu = jnp.concatenate([mu_row[:, 0:3 * W], pad_lora(mu_row)], axis=1)

    half = Dh // 2
    inv_freq = ROPE_THETA ** (-jnp.arange(half, dtype=F32) / half)
    ang = jnp.arange(S, dtype=F32)[:, None] * inv_freq[None, :]
    n_rope_heads = ROPE_COLS // Dh
    col_scale = jnp.where(jnp.arange(ROPE_COLS) < NSA_WIDTH, Dh ** -0.5 * math.log2(math.e), 1.0).astype(F32)
    cos_t = jnp.tile(jnp.cos(ang), (1, 2 * n_rope_heads)) * col_scale
    sin_t = jnp.tile(jnp.sin(ang), (1, 2 * n_rope_heads)) * col_scale

    qt, kc, ksw, vc, vt, glt, rw = _inproj(x, mix_pre_norm[l], 1.0 + sc_m, sh_m, w_all, cos_t, sin_t)

    nb = S // CMP_STRIDE
    pe8 = lambda pe: jnp.broadcast_to(pe.reshape(1, -1), (8, pe.size))
    kcmp = _compress(kc, cmp_k_w1[l], cmp_k_w2[l], pe8(cmp_k_pe[l]), transposed=False)
    vcmpt = _compress(vc, cmp_v_w1[l], cmp_v_w2[l].T, pe8(cmp_v_pe[l]), transposed=True)
    ns = S // SLC_BLOCK
    c0 = jnp.arange(nb)[None, :] * CMP_STRIDE
    b0 = jnp.arange(ns)[:, None] * SLC_BLOCK
    ov = jnp.maximum(jnp.minimum(c0 + CMP_BLOCK, b0 + SLC_BLOCK) - jnp.maximum(c0, b0), 0)
    overlap_t = (ov.astype(F32) / CMP_BLOCK).astype(BF16)
    o_nsa_t = _nsa(qt, ksw, vt, kcmp, vcmpt, overlap_t, glt)

    bd = (jnp.arange(W)[:, None] // Dh == jnp.arange(W)[None, :] // Dh).astype(BF16)
    row = lambda a: a.reshape(1, -1)
    r, lw, k, v, kk, a, g = _rwkv_prep(
        rw, mu, row(rwkv_w0[l]), _pad_rows(rwkv_w2[l], LANE), row(rwkv_a0[l]), _pad_rows(rwkv_a2[l], LANE),
        _pad_rows(rwkv_g2[l], LANE), row(rwkv_k_k[l]), row(rwkv_k_a[l]), bd)
    o_rwkv = _rwkv_scan(r, lw, k, v, kk, a, g, row(rwkv_ln_w[l]), row(rwkv_ln_b[l]), row(rwkv_r_k[l]))

    wo = w_out[l].astype(BF16)
    rb = jnp.concatenate([router_b[l], jnp.full((LANE - N_EXPERTS,), NEG_INF, F32)]).reshape(1, LANE)
    x1, h2, idx, wgt = _outproj(o_nsa_t, o_rwkv, x, nsa_out_norm[l].reshape(-1, 1), wo[0:NSA_WIDTH], wo[NSA_WIDTH:],
                                row(mix_post_norm[l]), g_m, row(ffn_pre_norm[l]), 1.0 + sc_f, sh_f,
                                _pad_cols(router_w[l], LANE), rb)

    n_assign = T * TOP_K
    idx4 = idx.reshape(T, LANE)[:, 0:TOP_K]
    e_ids = jnp.arange(N_EXPERTS, dtype=I32)
    hot = idx4[:, :, None] == e_ids
    onehot = hot.astype(I32).sum(axis=1)
    csum = jnp.cumsum(onehot, axis=0)
    counts = csum[-1]
    starts = jnp.cumsum(counts) - counts
    padded = ((counts + MOE_BLOCK - 1) // MOE_BLOCK) * MOE_BLOCK
    pad_ends = jnp.cumsum(padded)
    pad_starts = pad_ends - padded
    pos = jnp.where(hot, (csum - onehot + pad_starts)[:, None, :], 0).sum(axis=-1).astype(I32)
    n_blocks = -(-n_assign // MOE_BLOCK) + N_EXPERTS
    P = n_blocks * MOE_BLOCK
    blk_start = jnp.arange(n_blocks, dtype=I32) * MOE_BLOCK
    blk_expert = jnp.minimum((pad_ends[None, :] <= blk_start[:, None]).sum(axis=1), N_EXPERTS - 1).astype(I32)
    n_used = (pad_ends[-1] // MOE_BLOCK).astype(I32).reshape(1)
    order = jnp.argsort(idx4.reshape(-1))
    blk_off = blk_start - pad_starts[blk_expert]
    run_start = jnp.minimum(starts[blk_expert] + blk_off, n_assign).astype(I32)
    nvalid = jnp.clip(counts[blk_expert] - blk_off, 0, MOE_BLOCK).astype(I32)
    slot_tok = _slot_runs(run_start, nvalid, (order // TOP_K).astype(I32))

    x_sorted = h2.reshape(T, D)[slot_tok]
    y_sorted = _experts(blk_expert, n_used, x_sorted, expert_w_gate_up[l], expert_b_gate_up[l],
                        expert_w_down[l], expert_b_down[l])
    y4 = y_sorted[pos.T]
    return _combine(y4, wgt, x1, row(ffn_post_norm[l]), g_f)


def kernel(x, c, ada_w, ada_b, mix_pre_norm, mix_post_norm, ffn_pre_norm, ffn_post_norm, w_in, cmp_k_pe, cmp_k_w1, cmp_k_w2, cmp_v_pe, cmp_v_w1, cmp_v_w2, nsa_out_norm, rwkv_mu, rwkv_w0, rwkv_w2, rwkv_a0, rwkv_a2, rwkv_g2, rwkv_k_k, rwkv_k_a, rwkv_r_k, rwkv_ln_w, rwkv_ln_b, w_out, router_w, router_b, expert_w_gate_up, expert_b_gate_up, expert_w_down, expert_b_down):
    for l in range(ada_w.shape[0]):
        ada = _ada(c, ada_w[l], ada_b[l])
        x = _layer(x, ada, l, mix_pre_norm, mix_post_norm, ffn_pre_norm, ffn_post_norm, w_in, cmp_k_pe,
                   cmp_k_w1, cmp_k_w2, cmp_v_pe, cmp_v_w1, cmp_v_w2, nsa_out_norm, rwkv_mu, rwkv_w0, rwkv_w2,
                   rwkv_a0, rwkv_a2, rwkv_g2, rwkv_k_k, rwkv_k_a, rwkv_r_k, rwkv_ln_w, rwkv_ln_b, w_out,
                   router_w, router_b, expert_w_gate_up, expert_b_gate_up, expert_w_down, expert_b_down)
    return x
```

```python
import functools
import math

import jax
import jax.numpy as jnp
from jax import lax
from jax.experimental import pallas as pl
from jax.experimental.pallas import tpu as pltpu

F32 = jnp.float32
BF16 = jnp.bfloat16
I32 = jnp.int32

HEAD_DIM = 64
NSA_HEADS = 8
NSA_KV_GROUPS = 2
NSA_HPG = NSA_HEADS // NSA_KV_GROUPS
NSA_WIDTH = NSA_HEADS * HEAD_DIM
NSA_KV_WIDTH = NSA_KV_GROUPS * HEAD_DIM
CMP_BLOCK = 32
CMP_STRIDE = 16
SLC_BLOCK = 64
SLC_TOPK = 16
WINDOW = 512
Q_BLOCK = 256
RWKV_HEADS = 8
RWKV_WIDTH = RWKV_HEADS * HEAD_DIM
RWKV_W_LORA = 32
RWKV_A_LORA = 32
RWKV_G_LORA = 96
DECAY_SCALE = math.exp(-0.5)
GN_EPS = 64e-5
N_EXPERTS = 32
TOP_K = 4
SWIGLU_LIMIT = 7.0
SWIGLU_ALPHA = 1.702
ROPE_THETA = 10000.0
RMS_EPS = 1e-6
NEG_INF = -1e30
FORCED = 1e9
LOWEST = -3e38

LANE = 128
ROPE_COLS = NSA_WIDTH + 3 * NSA_KV_WIDTH
REST_COLS = 4 * LANE
RW_COLS = 3 * RWKV_WIDTH + 3 * LANE
SLC_KT = 512
FLASH_LANES = 1024
FLASH_DEPTH = 1
CMP_TIERS = 4
ACC_ROWS = HEAD_DIM + 16
RWKV_CHUNK = 128
TOKEN_TILE = 512
RWKV_BATCH_ROWS = 1
MOE_BLOCK = 512
VMEM_LIMIT = 56 * 1024 * 1024


def _bdot(a, b):
    return jnp.dot(a.astype(BF16), b.astype(BF16), preferred_element_type=F32)


def _split2(a):
    hi = a.astype(BF16)
    lo = (a - hi.astype(F32)).astype(BF16)
    return hi, lo


def _dot_lhs2(a, b_bf16):
    hi, lo = _split2(a)
    return (jnp.dot(hi, b_bf16, preferred_element_type=F32)
            + jnp.dot(lo, b_bf16, preferred_element_type=F32))


def _dot3(a, b):
    ah, al = _split2(a)
    bh, bl = _split2(b)
    return (jnp.dot(ah, bh, preferred_element_type=F32)
            + jnp.dot(al, bh, preferred_element_type=F32)
            + jnp.dot(ah, bl, preferred_element_type=F32))


def _params(*sem, flags=None):
    return pltpu.CompilerParams(dimension_semantics=sem, vmem_limit_bytes=VMEM_LIMIT, flags=flags)


def _ada_kernel(c_ref, w_ref, b_ref, o_ref):
    c = c_ref[...]
    act = c * jax.nn.sigmoid(c)
    o_ref[...] = _dot3(act, w_ref[...]) + b_ref[...]


def _ada(c, w, b):
    B, D = c.shape
    N = w.shape[1]
    tn = 1536
    return pl.pallas_call(
        _ada_kernel,
        grid=(N // tn,),
        in_specs=[pl.BlockSpec((B, D), lambda j: (0, 0)),
                  pl.BlockSpec((D, tn), lambda j: (0, j)),
                  pl.BlockSpec((1, tn), lambda j: (0, j))],
        out_specs=pl.BlockSpec((B, tn), lambda j: (0, j)),
        out_shape=jax.ShapeDtypeStruct((B, N), F32),
        compiler_params=_params("arbitrary"),
        name="ada",
    )(c, w, b.reshape(1, N))


def _inproj_kernel(x_ref, g_ref, sc_ref, sh_ref, w_ref, cos_ref, sin_ref,
                   q_ref, kc_ref, ksw_ref, vc_ref, vsw_ref, gl_ref, rw_ref):
    x = x_ref[0]
    ms = jnp.mean(x * x, axis=-1, keepdims=True)
    h = x * lax.rsqrt(ms + RMS_EPS) * g_ref[...]
    h = h * sc_ref[0] + sh_ref[0]
    hb = h.astype(BF16)
    R = ROPE_COLS
    main = jnp.dot(hb, w_ref[:, 0:R], preferred_element_type=F32)
    rot = jnp.dot(hb, w_ref[:, R:2 * R], preferred_element_type=F32)
    roped = main * cos_ref[...] + rot * sin_ref[...]
    q_ref[0] = roped[:, 0:NSA_WIDTH].T.astype(BF16)
    kc_ref[0] = roped[:, NSA_WIDTH:NSA_WIDTH + LANE]
    ksw_ref[0] = roped[:, NSA_WIDTH + LANE:R].astype(BF16)
    rest = jnp.dot(hb, w_ref[:, 2 * R:2 * R + REST_COLS], preferred_element_type=F32)
    vc_ref[0] = rest[:, 0:LANE]
    vsw_ref[0] = rest[:, LANE:3 * LANE].T.astype(BF16)
    gl_ref[0] = rest[:, 3 * LANE:4 * LANE].T
    rw_ref[0] = jnp.dot(hb, w_ref[:, 2 * R + REST_COLS:], preferred_element_type=F32)


def _inproj(x, gain, sc1, sh, w_all, cos_t, sin_t):
    B, S, D = x.shape
    tm = min(256, S)
    NW = w_all.shape[1]
    R = ROPE_COLS
    tok = lambda n: pl.BlockSpec((1, tm, n), lambda b, i: (b, i, 0))
    tok_t = lambda n: pl.BlockSpec((1, n, tm), lambda b, i: (b, 0, i))
    vec = pl.BlockSpec((1, 1, D), lambda b, i: (b, 0, 0))
    outs = [(NSA_WIDTH, BF16, True), (LANE, F32, False), (2 * LANE, BF16, False), (LANE, F32, False),
            (2 * LANE, BF16, True), (LANE, F32, True), (RW_COLS, F32, False)]
    return pl.pallas_call(
        _inproj_kernel,
        grid=(B, S // tm),
        in_specs=[tok(D), pl.BlockSpec((1, D), lambda b, i: (0, 0)), vec, vec,
                  pl.BlockSpec((D, NW), lambda b, i: (0, 0)),
                  pl.BlockSpec((tm, R), lambda b, i: (i, 0)),
                  pl.BlockSpec((tm, R), lambda b, i: (i, 0))],
        out_specs=[tok_t(n) if tr else tok(n) for n, _, tr in outs],
        out_shape=[jax.ShapeDtypeStruct((B, n, S) if tr else (B, S, n), dt) for n, dt, tr in outs],
        compiler_params=_params("parallel", "arbitrary"),
        name="inproj",
    )(x, gain.reshape(1, D), sc1, sh, w_all, cos_t, sin_t)


def _gelu_tanh(x):
    return 0.5 * x * (1.0 + jnp.tanh(math.sqrt(2.0 / math.pi) * (x + 0.044715 * (x * x * x))))


def _compress_kernel(r_ref, w1g_ref, w1_ref, w2_ref, pe_ref, o_ref, *, transposed):
    R = r_ref[0].astype(BF16)
    nb = R.shape[0]
    top = jnp.dot(R, w1g_ref[0, 0].astype(BF16), preferred_element_type=F32)
    bot = jnp.dot(R, w1g_ref[0, 1].astype(BF16), preferred_element_type=F32)
    pe_term = jnp.dot(pe_ref[...].astype(BF16), w1_ref[...].astype(BF16), preferred_element_type=F32)[0:1]
    bot_next = pltpu.roll(bot, nb - 1, 0)
    hid = _gelu_tanh(top + bot_next + pe_term)
    if transposed:
        out = jnp.dot(w2_ref[...].astype(BF16), hid.T.astype(BF16), preferred_element_type=F32)
    else:
        out = jnp.dot(hid.astype(BF16), w2_ref[...].astype(BF16), preferred_element_type=F32)
    o_ref[0, 0] = out.astype(BF16)


def _compress(tok, w1, w2, pe, transposed):
    B, S, GD = tok.shape
    G, Dh = NSA_KV_GROUPS, HEAD_DIM
    nb = S // CMP_STRIDE
    hid = w1.shape[-1]
    width = CMP_STRIDE * GD
    rr = tok.reshape(B, nb, width)
    w1r = w1.reshape(2, CMP_STRIDE, Dh, hid)
    w1g = jnp.zeros((G, 2, CMP_STRIDE, G, Dh, hid), w1.dtype)
    for g in range(G):
        w1g = w1g.at[g, :, :, g].set(w1r)
    w1g = w1g.reshape(G, 2, width, hid)
    oshape = (Dh, nb) if transposed else (nb, Dh)
    return pl.pallas_call(
        functools.partial(_compress_kernel, transposed=transposed),
        grid=(B, G),
        in_specs=[pl.BlockSpec((1, nb, width), lambda b, g: (b, 0, 0)),
                  pl.BlockSpec((1, 2, width, hid), lambda b, g: (g, 0, 0, 0)),
                  pl.BlockSpec(w1.shape, lambda b, g: (0, 0)),
                  pl.BlockSpec(w2.shape, lambda b, g: (0, 0)),
                  pl.BlockSpec(pe.shape, lambda b, g: (0, 0))],
        out_specs=pl.BlockSpec((1, 1) + oshape, lambda b, g: (b, g, 0, 0)),
        out_shape=jax.ShapeDtypeStruct((B, G) + oshape, BF16),
        compiler_params=_params("arbitrary", "arbitrary"),
        name="compress_v" if transposed else "compress_k",
    )(rr, w1g, w1, w2, pe)


def _lanes4(a):
    return jnp.concatenate([a, a, a, a], axis=1)


def _nsa_kernel(qt_ref, ksw_ref, vt_ref, kcmp_ref, vcmpt_ref, ovt_ref, glt_ref, o_ref, *, seq_len):
    QB, Dh, HPG = Q_BLOCK, HEAD_DIM, NSA_HPG
    qi = pl.program_id(1)
    s0 = qi * QB
    ns = seq_len // SLC_BLOCK
    ncp = seq_len // CMP_STRIDE
    n_sel = min(SLC_TOPK, ns)
    kt_slc = min(SLC_KT, seq_len)
    t_row = s0 + lax.broadcasted_iota(I32, (1, QB), 1)
    t4 = _lanes4(t_row)
    gates = jax.nn.sigmoid(glt_ref[0])
    blk = lax.broadcasted_iota(I32, (ns, QB), 0)
    cur = lax.shift_right_logical(t_row, 6)
    forced = (blk == 0) | (blk == cur) | (blk == cur - 1)
    future = blk * SLC_BLOCK > t_row
    cmp_last = lax.broadcasted_iota(I32, (ncp, 1), 0) * CMP_STRIDE + (CMP_BLOCK - 1)
    cmask = cmp_last <= t4
    zeros_q = jnp.zeros((Dh, HPG * QB), BF16)
    G = NSA_KV_GROUPS
    NL = G * HPG * QB
    qgs = [jnp.concatenate([qt_ref[0, (g * HPG + h) * Dh:(g * HPG + h + 1) * Dh, :] for h in range(HPG)], axis=1)
           for g in range(G)]
    qpads = [jnp.concatenate([qgs[g], zeros_q] if g == 0 else [zeros_q, qgs[g]], axis=0) for g in range(G)]

    def cmp_branch(rows):
        def run():
            outs = []
            for g in range(G):
                sc = jnp.dot(kcmp_ref[0, g, 0:rows, :], qgs[g], preferred_element_type=F32)
                sc = jnp.where(cmask[0:rows], sc, NEG_INF)
                m = jnp.max(sc, axis=0, keepdims=True)
                p = jnp.where(cmask[0:rows], jnp.exp2(sc - m), 0.0)
                l = jnp.sum(p, axis=0, keepdims=True)
                p = p * (1.0 / jnp.maximum(l, 1e-30))
                o_cmp = jnp.dot(vcmpt_ref[0, g, :, 0:rows], p.astype(BF16), preferred_element_type=F32)
                psum = p[:, 0:QB] + p[:, QB:2 * QB] + p[:, 2 * QB:3 * QB] + p[:, 3 * QB:4 * QB]
                p_hi, p_lo = _split2(psum)
                imp = (jnp.dot(ovt_ref[:, 0:rows], p_hi, preferred_element_type=F32)
                       + jnp.dot(ovt_ref[:, 0:rows], p_lo, preferred_element_type=F32))
                outs += [o_cmp, imp]
            return tuple(outs)
        return run

    n_tier = CMP_TIERS if ncp % (CMP_TIERS * 16) == 0 else 1
    tier_rows = ncp // n_tier
    n_visible = (s0 + QB - CMP_BLOCK) // CMP_STRIDE + 1
    tier = jnp.clip((n_visible - 1) // tier_rows, 0, n_tier - 1)
    cmp_out = lax.switch(tier, [cmp_branch((k + 1) * tier_rows) for k in range(n_tier)])
    o_cmps = [cmp_out[2 * g] for g in range(G)]
    imps = [jnp.where(future, NEG_INF, jnp.where(forced, FORCED, cmp_out[2 * g + 1])) for g in range(G)]

    vals = jnp.concatenate(imps, axis=1)
    blk2 = jnp.concatenate([blk] * G, axis=1)
    sel = jnp.zeros((ns, G * QB), F32)
    for _ in range(n_sel):
        mx = jnp.max(vals, axis=0, keepdims=True)
        idx = jnp.min(jnp.where(vals == mx, blk2, ns), axis=0, keepdims=True)
        pick = blk2 == idx
        sel = jnp.where(pick, 1.0, sel)
        vals = jnp.where(pick, LOWEST, vals)
    sel_bias = jnp.where(sel > 0.5, 0.0, NEG_INF).astype(BF16)
    sel_rows = jnp.concatenate([sel_bias[:, g * QB:(g + 1) * QB] for g in range(G) for _ in range(HPG)], axis=1)
    n_slab = NL // FLASH_LANES
    q_slabs = [jnp.concatenate(qpads, axis=1)[:, sb * FLASH_LANES:(sb + 1) * FLASH_LANES]
               for sb in range(n_slab)]
    qsel_slabs = [jnp.concatenate([q_slabs[sb], sel_rows[:, sb * FLASH_LANES:(sb + 1) * FLASH_LANES]], axis=0)
                  for sb in range(n_slab)]

    def load_tile(k_lane0, vrow0, k0, width, bias, first_block=None):
        ones = jnp.ones((ACC_ROWS - Dh, width), BF16)
        kt = ksw_ref[0, pl.ds(k0, width), k_lane0:k_lane0 + LANE]
        if first_block is not None:
            key_blk = first_block + lax.shift_right_logical(lax.broadcasted_iota(I32, (width, ns), 0), 6)
            onehot = (lax.broadcasted_iota(I32, (width, ns), 1) == key_blk).astype(BF16)
            kt = jnp.concatenate([kt, onehot], axis=1)
        vaug = [jnp.concatenate([vt_ref[0, vrow0 + g * Dh:vrow0 + (g + 1) * Dh, pl.ds(k0, width)], ones], axis=0)
                for g in range(G)]
        return kt, vaug, bias, first_block is not None

    group_lanes = HPG * QB
    groups_per_slab = max(FLASH_LANES // group_lanes, 1)
    lanes_per_part = FLASH_LANES // groups_per_slab

    def flash_tiles(tiles, carry):
        m, acc = list(carry[0]), list(carry[1])
        steps = [(ti, sb) for ti in range(len(tiles)) for sb in range(n_slab)]
        scores = {}
        for i in range(len(steps) + FLASH_DEPTH):
            if i < len(steps):
                ti, sb = steps[i]
                rhs = qsel_slabs[sb] if tiles[ti][3] else q_slabs[sb]
                scores[i] = jnp.dot(tiles[ti][0], rhs, preferred_element_type=F32)
            j = i - FLASH_DEPTH
            if j >= 0:
                ti, sb = steps[j]
                _, vaug, bias, _ = tiles[ti]
                g0 = sb * FLASH_LANES // group_lanes
                s = scores.pop(j)
                if bias is not None:
                    s = s + jnp.concatenate([bias] * (FLASH_LANES // QB), axis=1)
                m_new = jnp.maximum(m[sb], jnp.max(s, axis=0, keepdims=True))
                alpha = jnp.exp2(m[sb] - m_new)
                pj = jnp.exp2(s - m_new).astype(BF16)
                pv = [jnp.dot(vaug[g0 + p], pj[:, p * lanes_per_part:(p + 1) * lanes_per_part],
                              preferred_element_type=F32) for p in range(groups_per_slab)]
                acc[sb] = alpha * acc[sb] + (pv[0] if len(pv) == 1 else jnp.concatenate(pv, axis=1))
                m[sb] = m_new
        return tuple(m), tuple(acc)

    def slc_tile(j, causal_mask):
        k0 = pl.multiple_of(j * kt_slc, kt_slc)
        bias = None
        if causal_mask:
            bias = jnp.where(k0 + lax.broadcasted_iota(I32, (kt_slc, 1), 0) <= t_row, 0.0, NEG_INF)
        return load_tile(0, 0, k0, kt_slc, bias, first_block=j * (kt_slc // SLC_BLOCK))

    assert kt_slc % QB == 0 or seq_len == kt_slc
    init = (tuple(jnp.full((1, FLASH_LANES), NEG_INF, F32) for _ in range(n_slab)),
            tuple(jnp.zeros((ACC_ROWS, FLASH_LANES), F32) for _ in range(n_slab)))
    n_full = (s0 + 1) // kt_slc
    carry = lax.fori_loop(0, n_full, lambda j, c: flash_tiles([slc_tile(j, False)], c), init)
    _, acc_s = flash_tiles([slc_tile(n_full, True)], carry)
    acc_s = jnp.concatenate(acc_s, axis=1)
    o_slc = acc_s[0:Dh] * (1.0 / acc_s[Dh:Dh + 1])

    win_keys = min(WINDOW + QB, seq_len)
    k0 = pl.multiple_of(jnp.maximum(qi - WINDOW // QB, 0) * QB, QB)
    kpos = k0 + lax.broadcasted_iota(I32, (win_keys, 1), 0)
    wbias = jnp.where((kpos <= t_row) & (kpos > t_row - WINDOW), 0.0, NEG_INF)
    _, acc_w = flash_tiles([load_tile(LANE, LANE, k0, win_keys, wbias)], init)
    acc_w = jnp.concatenate(acc_w, axis=1)
    o_win = acc_w[0:Dh] * (1.0 / acc_w[Dh:Dh + 1])

    for g in range(G):
        for h in range(HPG):
            c0 = (g * HPG + h) * 3
            cols = slice((g * HPG + h) * QB, (g * HPG + h + 1) * QB)
            o = (gates[c0:c0 + 1, :] * o_cmps[g][:, h * QB:(h + 1) * QB] + gates[c0 + 1:c0 + 2, :] * o_slc[:, cols]
                 + gates[c0 + 2:c0 + 3, :] * o_win[:, cols])
            o_ref[0, (g * HPG + h) * Dh:(g * HPG + h + 1) * Dh, :] = o


def _nsa(qt, ksw, vt, kcmp, vcmpt, overlap_t, glt):
    B, _, S = qt.shape
    ncp = S // CMP_STRIDE
    ns = S // SLC_BLOCK
    G = NSA_KV_GROUPS
    return pl.pallas_call(
        functools.partial(_nsa_kernel, seq_len=S),
        grid=(B, S // Q_BLOCK),
        in_specs=[pl.BlockSpec((1, NSA_WIDTH, Q_BLOCK), lambda b, i: (b, 0, i)),
                  pl.BlockSpec((1, S, 2 * LANE), lambda b, i: (b, 0, 0)),
                  pl.BlockSpec((1, 2 * LANE, S), lambda b, i: (b, 0, 0)),
                  pl.BlockSpec((1, G, ncp, HEAD_DIM), lambda b, i: (b, 0, 0, 0)),
                  pl.BlockSpec((1, G, HEAD_DIM, ncp), lambda b, i: (b, 0, 0, 0)),
                  pl.BlockSpec((ns, ncp), lambda b, i: (0, 0)),
                  pl.BlockSpec((1, LANE, Q_BLOCK), lambda b, i: (b, 0, i))],
        out_specs=pl.BlockSpec((1, NSA_WIDTH, Q_BLOCK), lambda b, i: (b, 0, i)),
        out_shape=jax.ShapeDtypeStruct((B, NSA_WIDTH, S), F32),
        compiler_params=_params("parallel", "arbitrary"),
        name="nsa",
    )(qt, ksw, vt, kcmp, vcmpt, overlap_t, glt)


def _rwkv_prep_kernel(f_ref, p_ref, mu_ref, w0_ref, w2_ref, a0_ref, a2_ref, g2_ref, kk_ref, ka_ref,
                      bd_ref, r_o, lw_o, k_o, v_o, kk_o, a_o, g_o):
    W = RWKV_WIDTH
    i = pl.program_id(1)
    f = f_ref[0]
    tm = f.shape[0]
    prev_last = jnp.where(i > 0, p_ref[0, 7:8, :], 0.0)
    rolled = pltpu.roll(f, 1, 0)
    row = lax.broadcasted_iota(I32, (tm, 1), 0)
    prev = jnp.where(row == 0, prev_last, rolled)
    f = f + (prev - f) * mu_ref[...]
    r, k, v = f[:, 0:W], f[:, W:2 * W], f[:, 2 * W:3 * W]
    wd = f[:, 3 * W:3 * W + LANE]
    ad = f[:, 3 * W + LANE:3 * W + 2 * LANE]
    gd = f[:, 3 * W + 2 * LANE:3 * W + 3 * LANE]
    lw_o[0] = -DECAY_SCALE * jax.nn.sigmoid(w0_ref[...] + _dot3(jnp.tanh(wd), w2_ref[...]))
    a = jax.nn.sigmoid(a0_ref[...] + _dot3(ad, a2_ref[...]))
    g_o[0] = _dot3(jax.nn.sigmoid(gd), g2_ref[...])
    kk = k * kk_ref[...]
    ss = _dot_lhs2(kk * kk, bd_ref[...])
    kk_o[0] = kk / jnp.maximum(jnp.sqrt(ss), 1e-12)
    k_o[0] = k * (1.0 + (a - 1.0) * ka_ref[...])
    r_o[0] = r
    v_o[0] = v
    a_o[0] = a


def _rwkv_prep(rw, mu, w0, w2, a0, a2, g2, k_k, k_a, bd):
    B, S, C = rw.shape
    W = RWKV_WIDTH
    tm = min(TOKEN_TILE, S)
    row = lambda n: pl.BlockSpec((1, n), lambda b, i: (0, 0))
    mat = pl.BlockSpec((LANE, W), lambda b, i: (0, 0))
    tok = pl.BlockSpec((1, tm, W), lambda b, i: (b, i, 0))
    return pl.pallas_call(
        _rwkv_prep_kernel,
        grid=(B, S // tm),
        in_specs=[pl.BlockSpec((1, tm, C), lambda b, i: (b, i, 0)),
                  pl.BlockSpec((1, 8, C), lambda b, i: (b, jnp.maximum(i * (tm // 8) - 1, 0), 0)),
                  row(C), row(W), mat, row(W), mat, mat, row(W), row(W),
                  pl.BlockSpec((W, W), lambda b, i: (0, 0))],
        out_specs=[tok] * 7,
        out_shape=[jax.ShapeDtypeStruct((B, S, W), F32)] * 7,
        compiler_params=_params("parallel", "arbitrary"),
        name="rwkv_prep",
    )(rw, rw, mu, w0, w2, a0, a2, g2, k_k, k_a, bd)


def _rwkv_scan_kernel(r_ref, lw_ref, k_ref, v_ref, kk_ref, a_ref, g_ref, lnw_ref, lnb_ref, rk_ref,
                      o_ref, st_ref):
    Dh = HEAD_DIM

    @pl.when(pl.program_id(1) == 0)
    def _():
        st_ref[...] = jnp.zeros_like(st_ref)

    NB, C = r_ref.shape[0], r_ref.shape[1]
    ri = lax.broadcasted_iota(I32, (C, C), 0)
    ci = lax.broadcasted_iota(I32, (C, C), 1)
    incl = ri >= ci
    strict = ri > ci
    tri = incl.astype(BF16)
    eye = (ri == ci).astype(F32)
    n_double = max(int(math.log2(C)) - 1, 0)

    def scaled_operands(bb):
        r, lw, k, kk, a = (ref[bb] for ref in (r_ref, lw_ref, k_ref, kk_ref, a_ref))
        l_hi = lw.astype(BF16)
        l_r1 = lw - l_hi.astype(F32)
        l_mid = l_r1.astype(BF16)
        l_lo = (l_r1 - l_mid.astype(F32)).astype(BF16)
        L = (jnp.dot(tri, l_hi, preferred_element_type=F32) + jnp.dot(tri, l_mid, preferred_element_type=F32)
             + jnp.dot(tri, l_lo, preferred_element_type=F32))
        e_l = jnp.exp(L)
        e_inv = jnp.exp(-L)
        e_end = jnp.exp(L[C - 1:C, :] - L)
        b = kk * a
        x1 = jnp.concatenate([kk * jnp.exp(L - lw), r * e_l], axis=0).astype(BF16)
        x2t = jnp.concatenate([k * e_inv, b * e_inv], axis=0).T.astype(BF16)
        zt = jnp.concatenate([k * e_end, -(b * e_end)], axis=0).T.astype(BF16)
        gt = jnp.concatenate([e_l, e_l], axis=0).T
        return x1, x2t, zt, gt

    ops = [scaled_operands(bb) for bb in range(NB)]
    chains = [(bb, h) for bb in range(NB) for h in range(RWKV_HEADS)]
    sls = [slice(h * Dh, (h + 1) * Dh) for _, h in chains]
    h0s = [st_ref[bb, h] for bb, h in chains]
    vhs = [v_ref[bb][:, sl] for (bb, _), sl in zip(chains, sls)]
    x1s = [ops[bb][0][:, sl] for (bb, _), sl in zip(chains, sls)]
    amats = [jnp.dot(x1, ops[bb][1][sl, :], preferred_element_type=F32)
             for x1, (bb, _), sl in zip(x1s, chains, sls)]
    p0s = [jnp.dot(x1, h0.astype(BF16), preferred_element_type=F32) for x1, h0 in zip(x1s, h0s)]
    a_kk = [jnp.where(strict, am[0:C, 0:C], 0.0).astype(BF16) for am in amats]
    a_rkb = [jnp.concatenate([jnp.where(incl, am[C:2 * C, 0:C], 0.0), jnp.where(incl, -am[C:2 * C, C:2 * C], 0.0)],
                             axis=1).astype(BF16) for am in amats]
    nmats = [jnp.where(strict, -am[0:C, C:2 * C], 0.0) for am in amats]
    tinvs = [eye + nm for nm in nmats]
    for _ in range(n_double):
        nmats = [_bdot(nm, nm) for nm in nmats]
        tinvs = [ti + _bdot(ti, nm) for ti, nm in zip(tinvs, nmats)]
    akv = [jnp.dot(ak, vh.astype(BF16), preferred_element_type=F32) for ak, vh in zip(a_kk, vhs)]
    us = [_bdot(ti, p0[0:C] + av) for ti, p0, av in zip(tinvs, p0s, akv)]
    vus = [jnp.concatenate([vh, u], axis=0).astype(BF16) for vh, u in zip(vhs, us)]
    ys = [p0[C:2 * C] + jnp.dot(ar, vu, preferred_element_type=F32) for p0, ar, vu in zip(p0s, a_rkb, vus)]
    for i, (bb, h) in enumerate(chains):
        _, _, zt, gt = ops[bb]
        st_ref[bb, h] = gt[sls[i], C - 1:C] * h0s[i] + jnp.dot(zt[sls[i], :], vus[i], preferred_element_type=F32)
    for i, (bb, h) in enumerate(chains):
        sl, y = sls[i], ys[i]
        mean = jnp.mean(y, axis=-1, keepdims=True)
        yc = y - mean
        var = jnp.mean(yc * yc, axis=-1, keepdims=True)
        yn = yc * lax.rsqrt(var + GN_EPS) * lnw_ref[:, sl] + lnb_ref[:, sl]
        bonus = jnp.sum(r_ref[bb][:, sl] * k_ref[bb][:, sl] * rk_ref[:, sl], axis=-1, keepdims=True) * vhs[i]
        o_ref[bb, :, sl] = (yn + bonus) * g_ref[bb][:, sl]


def _rwkv_scan(r, lw, k, v, kk, a, g, ln_w, ln_b, r_k):
    B, S, W = r.shape
    C = min(RWKV_CHUNK, S)
    nb = math.gcd(RWKV_BATCH_ROWS, B)
    tok = pl.BlockSpec((nb, C, W), lambda b, i: (b, i, 0))
    row = pl.BlockSpec((1, W), lambda b, i: (0, 0))
    return pl.pallas_call(
        _rwkv_scan_kernel,
        grid=(B // nb, S // C),
        in_specs=[tok] * 7 + [row] * 3,
        out_specs=tok,
        out_shape=jax.ShapeDtypeStruct((B, S, W), F32),
        scratch_shapes=[pltpu.VMEM((nb, RWKV_HEADS, HEAD_DIM, HEAD_DIM), F32)],
        compiler_params=_params("parallel", "arbitrary"),
        name="rwkv_scan",
    )(r, lw, k, v, kk, a, g, ln_w, ln_b, r_k)


def _rms(x, g):
    return x * lax.rsqrt(jnp.mean(x * x, axis=-1, keepdims=True) + RMS_EPS) * g


def _outproj_kernel(on_ref, or_ref, x_ref, gn_ref, wt_ref, wb_ref, gpost_ref, gm_ref, gpre_ref,
                    scf_ref, shf_ref, rw_ref, rb_ref, x1_ref, h2_ref, idx_ref, wgt_ref):
    ont = on_ref[0]
    ont = ont * lax.rsqrt(jnp.mean(ont * ont, axis=0, keepdims=True) + RMS_EPS) * gn_ref[...]
    on = ont.T
    mixed = (jnp.dot(on.astype(BF16), wt_ref[...], preferred_element_type=F32)
             + jnp.dot(or_ref[0].astype(BF16), wb_ref[...], preferred_element_type=F32))
    x1 = x_ref[0] + gm_ref[0] * _rms(mixed, gpost_ref[...])
    x1_ref[0] = x1
    h2 = _rms(x1, gpre_ref[...]) * scf_ref[0] + shf_ref[0]
    h2_ref[0] = h2
    logits = _dot3(h2, rw_ref[...]) + rb_ref[...]
    lane = lax.broadcasted_iota(I32, logits.shape, 1)
    vals = logits
    top_v, top_i = [], []
    for _ in range(TOP_K):
        mx = jnp.max(vals, axis=-1, keepdims=True)
        ix = jnp.min(jnp.where(vals == mx, lane, LANE), axis=-1, keepdims=True)
        top_v.append(mx)
        top_i.append(ix)
        vals = jnp.where(lane == ix, LOWEST, vals)
    ex = [jnp.exp(tv - top_v[0]) for tv in top_v]
    inv = 1.0 / (ex[0] + ex[1] + ex[2] + ex[3])
    idx = jnp.zeros(logits.shape, I32)
    wgt = jnp.zeros(logits.shape, F32)
    for kk in range(TOP_K):
        idx = jnp.where(lane == kk, top_i[kk], idx)
        wgt = jnp.where(lane == kk, ex[kk] * inv, wgt)
    idx_ref[0] = idx
    wgt_ref[0] = wgt


def _outproj(o_nsa, o_rwkv, x, gn, w_top, w_bot, g_post, g_m, g_pre, sc1_f, sh_f, router_w, router_b):
    B, S, D = x.shape
    tm = min(TOKEN_TILE, S)
    W = o_rwkv.shape[-1]
    tok = lambda n: pl.BlockSpec((1, tm, n), lambda b, i: (b, i, 0))
    vec = pl.BlockSpec((1, 1, D), lambda b, i: (b, 0, 0))
    row = lambda n: pl.BlockSpec((1, n), lambda b, i: (0, 0))
    return pl.pallas_call(
        _outproj_kernel,
        grid=(B, S // tm),
        in_specs=[pl.BlockSpec((1, W, tm), lambda b, i: (b, 0, i)), tok(W), tok(D),
                  pl.BlockSpec((W, 1), lambda b, i: (0, 0)),
                  pl.BlockSpec((W, D), lambda b, i: (0, 0)), pl.BlockSpec((W, D), lambda b, i: (0, 0)),
                  row(D), vec, row(D), vec, vec,
                  pl.BlockSpec((D, LANE), lambda b, i: (0, 0)), row(LANE)],
        out_specs=[tok(D), tok(D), tok(LANE), tok(LANE)],
        out_shape=[jax.ShapeDtypeStruct((B, S, D), F32), jax.ShapeDtypeStruct((B, S, D), F32),
                   jax.ShapeDtypeStruct((B, S, LANE), I32), jax.ShapeDtypeStruct((B, S, LANE), F32)],
        compiler_params=_params("parallel", "arbitrary"),
        name="outproj",
    )(o_nsa, o_rwkv, x, gn, w_top, w_bot, g_post, g_m, g_pre, sc1_f, sh_f, router_w, router_b)


def _slot_runs_kernel(start_ref, nvalid_ref, tok_ref, o_ref):
    rows = MOE_BLOCK // LANE
    lane = lax.broadcasted_iota(I32, (rows, LANE), 1)
    within = lax.broadcasted_iota(I32, (rows, LANE), 0) * LANE + lane

    def body(b, carry):
        s = start_ref[b]
        off = jnp.bitwise_and(s, LANE - 1)
        win = tok_ref[pl.ds(lax.shift_right_logical(s, 7), rows + 1), :]
        win = pltpu.roll(win, jnp.bitwise_and(LANE - off, LANE - 1), 1)
        run = jnp.where(lane < LANE - off, win[0:rows], win[1:rows + 1])
        o_ref[pl.ds(pl.multiple_of(b * rows, rows), rows), :] = jnp.where(within < nvalid_ref[b], run, 0)
        return carry

    lax.fori_loop(0, start_ref.shape[0], body, 0)


def _slot_runs(run_start, nvalid, tok_sorted):
    nb = run_start.shape[0]
    rows = MOE_BLOCK // LANE
    n_rows = -(-(tok_sorted.shape[0] // LANE + rows + 1) // 8) * 8
    tok2d = jnp.pad(tok_sorted, (0, n_rows * LANE - tok_sorted.shape[0])).reshape(n_rows, LANE)
    grid_spec = pltpu.PrefetchScalarGridSpec(
        num_scalar_prefetch=2,
        grid=(1,),
        in_specs=[pl.BlockSpec((n_rows, LANE), lambda i, s, n: (0, 0))],
        out_specs=pl.BlockSpec((nb * rows, LANE), lambda i, s, n: (0, 0)),
    )
    out = pl.pallas_call(
        _slot_runs_kernel,
        grid_spec=grid_spec,
        out_shape=jax.ShapeDtypeStruct((nb * rows, LANE), I32),
        compiler_params=_params("arbitrary"),
        name="slot_runs",
    )(run_start, nvalid, tok2d)
    return out.reshape(nb * MOE_BLOCK)


def _expert_kernel(be_ref, nu_ref, x_ref, wgu_ref, bgu_ref, wd_ref, bd_ref, o_ref, wgu_bf, wd_bf):
    F = wd_ref.shape[1]
    i = pl.program_id(0)
    active = i < nu_ref[0]
    new_expert = (i == 0) | (be_ref[i] != be_ref[jnp.maximum(i - 1, 0)])

    @pl.when(active & new_expert)
    def _():
        wgu_bf[...] = wgu_ref[0].astype(BF16)
        wd_bf[...] = wd_ref[0].astype(BF16)

    @pl.when(active)
    def _():
        gu = jnp.dot(x_ref[...].astype(BF16), wgu_bf[...], preferred_element_type=F32) + bgu_ref[0]
        gate = jnp.minimum(gu[:, 0:F], SWIGLU_LIMIT)
        up = jnp.clip(gu[:, F:2 * F], -SWIGLU_LIMIT, SWIGLU_LIMIT)
        act = ((up + 1.0) * (gate * jax.nn.sigmoid(gate * SWIGLU_ALPHA))).astype(BF16)
        out = jnp.dot(act, wd_bf[...], preferred_element_type=F32) + bd_ref[0]
        o_ref[...] = out.astype(o_ref.dtype)

    @pl.when(jnp.logical_not(active))
    def _():
        o_ref[...] = jnp.zeros_like(o_ref)


def _experts(blk_expert, n_used, x_sorted, w_gu, b_gu, w_d, b_d):
    P, D = x_sorted.shape
    E, _, F2 = w_gu.shape
    F = F2 // 2
    nb = P // MOE_BLOCK
    grid_spec = pltpu.PrefetchScalarGridSpec(
        num_scalar_prefetch=2,
        grid=(nb,),
        in_specs=[pl.BlockSpec((MOE_BLOCK, D), lambda i, be, nu: (i, 0)),
                  pl.BlockSpec((1, D, F2), lambda i, be, nu: (be[i], 0, 0)),
                  pl.BlockSpec((1, 1, F2), lambda i, be, nu: (be[i], 0, 0)),
                  pl.BlockSpec((1, F, D), lambda i, be, nu: (be[i], 0, 0)),
                  pl.BlockSpec((1, 1, D), lambda i, be, nu: (be[i], 0, 0))],
        out_specs=pl.BlockSpec((MOE_BLOCK, D), lambda i, be, nu: (i, 0)),
        scratch_shapes=[pltpu.VMEM((D, F2), BF16), pltpu.VMEM((F, D), BF16)],
    )
    return pl.pallas_call(
        _expert_kernel,
        grid_spec=grid_spec,
        out_shape=jax.ShapeDtypeStruct((P, D), F32),
        compiler_params=_params("arbitrary"),
        name="experts",
    )(blk_expert, n_used, x_sorted, w_gu, b_gu.reshape(E, 1, F2), w_d, b_d.reshape(E, 1, D))


def _combine_kernel(y_ref, w_ref, x1_ref, gpost_ref, gf_ref, o_ref):
    w = w_ref[0]
    y = (w[:, 0:1] * y_ref[0].astype(F32) + w[:, 1:2] * y_ref[1].astype(F32)
         + w[:, 2:3] * y_ref[2].astype(F32) + w[:, 3:4] * y_ref[3].astype(F32))
    o_ref[0] = x1_ref[0] + gf_ref[0] * _rms(y, gpost_ref[...])


def _combine(y4, wgt, x1, g_post, g_f):
    B, S, D = x1.shape
    tm = min(TOKEN_TILE, S)
    nt = S // tm
    return pl.pallas_call(
        _combine_kernel,
        grid=(B, nt),
        in_specs=[pl.BlockSpec((TOP_K, tm, D), lambda b, i: (0, b * nt + i, 0)),
                  pl.BlockSpec((1, tm, LANE), lambda b, i: (b, i, 0)),
                  pl.BlockSpec((1, tm, D), lambda b, i: (b, i, 0)),
                  pl.BlockSpec((1, D), lambda b, i: (0, 0)),
                  pl.BlockSpec((1, 1, D), lambda b, i: (b, 0, 0))],
        out_specs=pl.BlockSpec((1, tm, D), lambda b, i: (b, i, 0)),
        out_shape=jax.ShapeDtypeStruct((B, S, D), F32),
        compiler_params=_params("parallel", "arbitrary"),
        name="combine",
    )(y4, wgt, x1, g_post, g_f)


def _rot_cols(w):
    d, n = w.shape
    w4 = w.reshape(d, n // HEAD_DIM, 2, HEAD_DIM // 2)
    return jnp.concatenate([-w4[:, :, 1:2], w4[:, :, 0:1]], axis=2).reshape(d, n)


def _pad_cols(w, n):
    return jnp.pad(w, ((0, 0), (0, n - w.shape[1])))


def _pad_rows(w, n):
    return jnp.pad(w, ((0, n - w.shape[0]), (0, 0)))


def _layer(x, ada, l, mix_pre_norm, mix_post_norm, ffn_pre_norm, ffn_post_norm, w_in, cmp_k_pe, cmp_k_w1,
           cmp_k_w2, cmp_v_pe, cmp_v_w1, cmp_v_w2, nsa_out_norm, rwkv_mu, rwkv_w0, rwkv_w2, rwkv_a0, rwkv_a2,
           rwkv_g2, rwkv_k_k, rwkv_k_a, rwkv_r_k, rwkv_ln_w, rwkv_ln_b, w_out, router_w, router_b,
           expert_w_gate_up, expert_b_gate_up, expert_w_down, expert_b_down):
    B, S, D = x.shape
    T = B * S
    G, Dh, W = NSA_KV_GROUPS, HEAD_DIM, RWKV_WIDTH
    sh_m, sc_m, g_m, sh_f, sc_f, g_f = [a.reshape(B, 1, D) for a in jnp.split(ada, 6, axis=-1)]

    wi = w_in[l]
    KV = NSA_KV_WIDTH
    o = NSA_WIDTH
    wq, wkc, wvc, wks, wvs, wkw, wvw = (wi[:, 0:o],) + tuple(wi[:, o + j * KV:o + (j + 1) * KV] for j in range(6))
    wgl = wi[:, o + 6 * KV:o + 6 * KV + 3 * NSA_HEADS]
    wr = wi[:, o + 6 * KV + 3 * NSA_HEADS:]
    w_rope = jnp.concatenate([wq, wkc, wks, wkw], axis=1)
    w_rest = _pad_cols(jnp.concatenate([wvc, wvs, wvw, wgl], axis=1), REST_COLS)
    lora = [(3 * W, RWKV_W_LORA), (3 * W + RWKV_W_LORA, RWKV_A_LORA),
            (3 * W + RWKV_W_LORA + RWKV_A_LORA, RWKV_G_LORA)]
    pad_lora = lambda a: jnp.concatenate([_pad_cols(a[..., s:s + n], LANE) for s, n in lora], axis=-1)
    w_rw = jnp.concatenate([wr[:, 0:3 * W], pad_lora(wr)], axis=1)
    w_all = jnp.concatenate([w_rope, _rot_cols(w_rope), w_rest, w_rw], axis=1).astype(BF16)
    mu_row = rwkv_mu[l].reshape(1, -1)
    mu = jnp.concatenate([mu_row[:, 0:3 * W], pad_lora(mu_row)], axis=1)

    half = Dh // 2
    inv_freq = ROPE_THETA ** (-jnp.arange(half, dtype=F32) / half)
    ang = jnp.arange(S, dtype=F32)[:, None] * inv_freq[None, :]
    n_rope_heads = ROPE_COLS // Dh
    col_scale = jnp.where(jnp.arange(ROPE_COLS) < NSA_WIDTH, Dh ** -0.5 * math.log2(math.e), 1.0).astype(F32)
    cos_t = jnp.tile(jnp.cos(ang), (1, 2 * n_rope_heads)) * col_scale
    sin_t = jnp.tile(jnp.sin(ang), (1, 2 * n_rope_heads)) * col_scale

    qt, kc, ksw, vc, vt, glt, rw = _inproj(x, mix_pre_norm[l], 1.0 + sc_m, sh_m, w_all, cos_t, sin_t)

    nb = S // CMP_STRIDE
    pe8 = lambda pe: jnp.broadcast_to(pe.reshape(1, -1), (8, pe.size))
    kcmp = _compress(kc, cmp_k_w1[l], cmp_k_w2[l], pe8(cmp_k_pe[l]), transposed=False)
    vcmpt = _compress(vc, cmp_v_w1[l], cmp_v_w2[l].T, pe8(cmp_v_pe[l]), transposed=True)
    ns = S // SLC_BLOCK
    c0 = jnp.arange(nb)[None, :] * CMP_STRIDE
    b0 = jnp.arange(ns)[:, None] * SLC_BLOCK
    ov = jnp.maximum(jnp.minimum(c0 + CMP_BLOCK, b0 + SLC_BLOCK) - jnp.maximum(c0, b0), 0)
    overlap_t = (ov.astype(F32) / CMP_BLOCK).astype(BF16)
    o_nsa_t = _nsa(qt, ksw, vt, kcmp, vcmpt, overlap_t, glt)

    bd = (jnp.arange(W)[:, None] // Dh == jnp.arange(W)[None, :] // Dh).astype(BF16)
    row = lambda a: a.reshape(1, -1)
    r, lw, k, v, kk, a, g = _rwkv_prep(
        rw, mu, row(rwkv_w0[l]), _pad_rows(rwkv_w2[l], LANE), row(rwkv_a0[l]), _pad_rows(rwkv_a2[l], LANE),
        _pad_rows(rwkv_g2[l], LANE), row(rwkv_k_k[l]), row(rwkv_k_a[l]), bd)
    o_rwkv = _rwkv_scan(r, lw, k, v, kk, a, g, row(rwkv_ln_w[l]), row(rwkv_ln_b[l]), row(rwkv_r_k[l]))

    wo = w_out[l].astype(BF16)
    rb = jnp.concatenate([router_b[l], jnp.full((LANE - N_EXPERTS,), NEG_INF, F32)]).reshape(1, LANE)
    x1, h2, idx, wgt = _outproj(o_nsa_t, o_rwkv, x, nsa_out_norm[l].reshape(-1, 1), wo[0:NSA_WIDTH], wo[NSA_WIDTH:],
                                row(mix_post_norm[l]), g_m, row(ffn_pre_norm[l]), 1.0 + sc_f, sh_f,
                                _pad_cols(router_w[l], LANE), rb)

    n_assign = T * TOP_K
    idx4 = idx.reshape(T, LANE)[:, 0:TOP_K]
    e_ids = jnp.arange(N_EXPERTS, dtype=I32)
    hot = idx4[:, :, None] == e_ids
    onehot = hot.astype(I32).sum(axis=1)
    csum = jnp.cumsum(onehot, axis=0)
    counts = csum[-1]
    starts = jnp.cumsum(counts) - counts
    padded = ((counts + MOE_BLOCK - 1) // MOE_BLOCK) * MOE_BLOCK
    pad_ends = jnp.cumsum(padded)
    pad_starts = pad_ends - padded
    pos = jnp.where(hot, (csum - onehot + pad_starts)[:, None, :], 0).sum(axis=-1).astype(I32)
    n_blocks = -(-n_assign // MOE_BLOCK) + N_EXPERTS
    P = n_blocks * MOE_BLOCK
    blk_start = jnp.arange(n_blocks, dtype=I32) * MOE_BLOCK
    blk_expert = jnp.minimum((pad_ends[None, :] <= blk_start[:, None]).sum(axis=1), N_EXPERTS - 1).astype(I32)
    n_used = (pad_ends[-1] // MOE_BLOCK).astype(I32).reshape(1)
    order = jnp.argsort(idx4.reshape(-1))
    blk_off = blk_start - pad_starts[blk_expert]
    run_start = jnp.minimum(starts[blk_expert] + blk_off, n_assign).astype(I32)
    nvalid = jnp.clip(counts[blk_expert] - blk_off, 0, MOE_BLOCK).astype(I32)
    slot_tok = _slot_runs(run_start, nvalid, (order // TOP_K).astype(I32))

    x_sorted = h2.reshape(T, D)[slot_tok]
    y_sorted = _experts(blk_expert, n_used, x_sorted, expert_w_gate_up[l], expert_b_gate_up[l],
                        expert_w_down[l], expert_b_down[l])
    y4 = y_sorted[pos.T]
    return _combine(y4, wgt, x1, row(ffn_post_norm[l]), g_f)


def kernel(x, c, ada_w, ada_b, mix_pre_norm, mix_post_norm, ffn_pre_norm, ffn_post_norm, w_in, cmp_k_pe, cmp_k_w1, cmp_k_w2, cmp_v_pe, cmp_v_w1, cmp_v_w2, nsa_out_norm, rwkv_mu, rwkv_w0, rwkv_w2, rwkv_a0, rwkv_a2, rwkv_g2, rwkv_k_k, rwkv_k_a, rwkv_r_k, rwkv_ln_w, rwkv_ln_b, w_out, router_w, router_b, expert_w_gate_up, expert_b_gate_up, expert_w_down, expert_b_down):
    for l in range(ada_w.shape[0]):
        ada = _ada(c, ada_w[l], ada_b[l])
        x = _layer(x, ada, l, mix_pre_norm, mix_post_norm, ffn_pre_norm, ffn_post_norm, w_in, cmp_k_pe,
                   cmp_k_w1, cmp_k_w2, cmp_v_pe, cmp_v_w1, cmp_v_w2, nsa_out_norm, rwkv_mu, rwkv_w0, rwkv_w2,
                   rwkv_a0, rwkv_a2, rwkv_g2, rwkv_k_k, rwkv_k_a, rwkv_r_k, rwkv_ln_w, rwkv_ln_b, w_out,
                   router_w, router_b, expert_w_gate_up, expert_b_gate_up, expert_w_down, expert_b_down)
    return x
```
